```python
import jax, jax.numpy as jnp
from jax import lax
import numpy as np

D_MODEL = 1024
BATCH = 32
SEQ = 2048
DEPTH = 1
DEC_BATCH = 2
DEC_SEQ = 16384
PAST_LEN = 128

GRID_W = 64
HEAD_DIM = 64
RWKV_WIDTH = D_MODEL // 2
RWKV_HEADS = RWKV_WIDTH // HEAD_DIM
ATT_WIDTH = D_MODEL - RWKV_WIDTH
ATT_Q_HEADS = ATT_WIDTH // HEAD_DIM
ATT_KV_HEADS = 2
ATT_GROUP = ATT_Q_HEADS // ATT_KV_HEADS
KV_WIDTH = ATT_KV_HEADS * HEAD_DIM
W_LORA = 64
A_LORA = 64
G_LORA = 128
D_FF = -(-8 * D_MODEL // (3 * 256)) * 256
Q_BLOCK = 128
ROPE_THETA = 10000.0
ROPE_PAIRS = HEAD_DIM // 4
NORM_EPS = 1e-6
LNX_EPS = 64e-5

RWKV_SPLIT = [RWKV_WIDTH, RWKV_WIDTH, RWKV_WIDTH, W_LORA, W_LORA, A_LORA, A_LORA, G_LORA]
RWKV_COLS = int(sum(RWKV_SPLIT))
ATT_SPLIT = [ATT_WIDTH, KV_WIDTH, KV_WIDTH]
IN_COLS = RWKV_COLS + int(sum(ATT_SPLIT))
RWKV_SPLIT_IDX = [int(i) for i in np.cumsum(RWKV_SPLIT)[:-1]]
ATT_SPLIT_IDX = [int(i) for i in np.cumsum(ATT_SPLIT)[:-1]]

kernel_name = 'hybrid_rwkv7_axial_gqa_encoder'


def rms_norm(x, g):
    xf = x.astype(jnp.float32)
    y = xf * lax.rsqrt(jnp.mean(xf * xf, axis=-1, keepdims=True) + NORM_EPS)
    return (y * g.astype(jnp.float32)).astype(x.dtype)


def centred_shift(z, mu_prev, mu_next):
    z_prev = jnp.pad(z[:, :-1], ((0, 0), (1, 0), (0, 0)))
    z_next = jnp.pad(z[:, 1:], ((0, 0), (0, 1), (0, 0)))
    return z + mu_prev * (z_prev - z) + mu_next * (z_next - z)


def wkv7_scan(r, decay, k, v, a_vec, b_vec, reverse):
    B, T, H, N = r.shape
    xs = tuple(jnp.swapaxes(t, 0, 1) for t in (r, decay, k, v, a_vec, b_vec))

    def step(S, inp):
        r_t, w_t, k_t, v_t, a_t, b_t = inp
        sa = jnp.einsum('bhvk,bhk->bhv', S, a_t)
        S = S * w_t[:, :, None, :] + sa[..., None] * b_t[:, :, None, :] + v_t[..., None] * k_t[:, :, None, :]
        return S, jnp.einsum('bhvk,bhk->bhv', S, r_t)

    S0 = jnp.zeros((B, H, N, N), jnp.float32)
    _, ys = lax.scan(step, S0, xs, reverse=reverse)
    return jnp.swapaxes(ys, 0, 1)


def rwkv7_mix(z, mu_prev, mu_next, k_k, k_a, r_k, w0_f, w_lora_f, w0_b, w_lora_b,
              a0_f, a_lora_f, a0_b, a_lora_b, g_lora, lnx_w, lnx_b):
    B, T, _ = z.shape
    f32 = jnp.float32
    zf = centred_shift(z.astype(f32), mu_prev.astype(f32), mu_next.astype(f32))
    r, k, v, wd_f, wd_b, ad_f, ad_b, gd = jnp.split(zf, RWKV_SPLIT_IDX, axis=-1)

    def heads(t):
        return t.reshape(B, T, RWKV_HEADS, HEAD_DIM)

    kk = heads(k * k_k.astype(f32))
    kk = kk * lax.rsqrt(jnp.maximum(jnp.sum(kk * kk, axis=-1, keepdims=True), 1e-12))
    r_h, v_h = heads(r), heads(v)

    def direction(wd, w0, w_lora, ad, a0, a_lora, reverse):
        w_log = -jax.nn.softplus(-(w0.astype(f32) + jnp.tanh(wd) @ w_lora.astype(f32))) - 0.5
        decay = heads(jnp.exp(-jnp.exp(w_log)))
        a = jax.nn.sigmoid(a0.astype(f32) + ad @ a_lora.astype(f32))
        kd = heads(k * (1.0 + (a - 1.0) * k_a.astype(f32)))
        y = wkv7_scan(r_h, decay, kd, v_h, -kk, kk * heads(a), reverse)
        return y, kd

    y_f, kd_f = direction(wd_f, w0_f, w_lora_f, ad_f, a0_f, a_lora_f, False)
    y_b, kd_b = direction(wd_b, w0_b, w_lora_b, ad_b, a0_b, a_lora_b, True)
    y = y_f + y_b
    mu = jnp.mean(y, axis=-1, keepdims=True)
    var = jnp.mean(jnp.square(y - mu), axis=-1, keepdims=True)
    yn = ((y - mu) * lax.rsqrt(var + LNX_EPS)).reshape(B, T, RWKV_WIDTH)
    yn = yn * lnx_w.astype(f32) + lnx_b.astype(f32)
    kb = 0.5 * (kd_f + kd_b)
    bonus = (jnp.sum(r_h * kb * r_k.astype(f32), axis=-1, keepdims=True) * v_h).reshape(B, T, RWKV_WIDTH)
    g = jax.nn.sigmoid(gd) @ g_lora.astype(f32)
    return ((yn + bonus) * g).astype(z.dtype)


def axial_angles(T):
    n_rows = T // GRID_W
    row = jnp.broadcast_to(jnp.arange(n_rows, dtype=jnp.float32)[:, None], (n_rows, GRID_W)).reshape(-1)
    col = jnp.broadcast_to(jnp.arange(GRID_W, dtype=jnp.float32)[None, :], (n_rows, GRID_W)).reshape(-1)
    inv = ROPE_THETA ** (-jnp.arange(ROPE_PAIRS, dtype=jnp.float32) / ROPE_PAIRS)
    return row[:, None] * inv, col[:, None] * inv


def rope_half(x, ang):
    c = jnp.cos(ang)[None, :, None, :].astype(x.dtype)
    s = jnp.sin(ang)[None, :, None, :].astype(x.dtype)
    x1, x2 = x[..., :ROPE_PAIRS], x[..., ROPE_PAIRS:]
    return jnp.concatenate([x1 * c - x2 * s, x1 * s + x2 * c], axis=-1)


def axial_rope(x, ang_row, ang_col):
    h = HEAD_DIM // 2
    return jnp.concatenate([rope_half(x[..., :h], ang_row), rope_half(x[..., h:], ang_col)], axis=-1)


def axial_gqa(z, q_gain, k_gain):
    B, T, _ = z.shape
    q, k, v = jnp.split(z, ATT_SPLIT_IDX, axis=-1)
    q = q.reshape(B, T, ATT_Q_HEADS, HEAD_DIM)
    k = k.reshape(B, T, ATT_KV_HEADS, HEAD_DIM)
    v = v.reshape(B, T, ATT_KV_HEADS, HEAD_DIM)
    ang_row, ang_col = axial_angles(T)
    q = axial_rope(rms_norm(q, q_gain), ang_row, ang_col)
    k = axial_rope(rms_norm(k, k_gain), ang_row, ang_col)
    nb = T // Q_BLOCK
    qb = q.reshape(B, nb, Q_BLOCK, ATT_KV_HEADS, ATT_GROUP, HEAD_DIM).transpose(1, 0, 2, 3, 4, 5)
    scale = HEAD_DIM ** -0.5

    def block(q_blk):
        s = jnp.einsum('bqhgd,bkhd->bhgqk', q_blk, k).astype(jnp.float32) * scale
        p = jax.nn.softmax(s, axis=-1).astype(v.dtype)
        return jnp.einsum('bhgqk,bkhd->bqhgd', p, v)

    o = lax.map(block, qb)
    return o.transpose(1, 0, 2, 3, 4, 5).reshape(B, T, ATT_WIDTH)


def encoder_layer(x, norm1_g, w_in, mu_prev, mu_next, k_k, k_a, r_k, w0_f, w_lora_f, w0_b, w_lora_b,
                  a0_f, a_lora_f, a0_b, a_lora_b, g_lora, lnx_w, lnx_b, q_gain, k_gain, w_out,
                  norm2_g, ffn_gate, ffn_up, ffn_down):
    h = rms_norm(x, norm1_g)
    z = h @ w_in
    y_rwkv = rwkv7_mix(z[..., :RWKV_COLS], mu_prev, mu_next, k_k, k_a, r_k, w0_f, w_lora_f, w0_b,
                       w_lora_b, a0_f, a_lora_f, a0_b, a_lora_b, g_lora, lnx_w, lnx_b)
    y_att = axial_gqa(z[..., RWKV_COLS:], q_gain, k_gain)
    x = x + jnp.concatenate([y_rwkv, y_att], axis=-1) @ w_out
    h = rms_norm(x, norm2_g)
    x = x + (jax.nn.silu(h @ ffn_gate) * (h @ ffn_up)) @ ffn_down
    return x


def trunk(x, layer_params, norm_f_g):
    for l in range(DEPTH):
        x = encoder_layer(x, *[p[l] for p in layer_params])
    return rms_norm(x, norm_f_g)


def setup_inputs(seed: int = 0) -> dict:
    key = jax.random.key(seed)
    ks = jax.random.split(key, 32)
    L, D, RW, H, N = DEPTH, D_MODEL, RWKV_WIDTH, RWKV_HEADS, HEAD_DIM
    nrm = jax.random.normal
    f32 = jnp.float32
    return {
        'x_prompt': nrm(ks[0], (BATCH, SEQ, D), f32),
        'x_sample': nrm(ks[1], (DEC_BATCH, DEC_SEQ, D), f32),
        'norm1_g': 1.0 + 0.02 * nrm(ks[2], (L, D), f32),
        'w_in': nrm(ks[3], (L, D, IN_COLS), f32) * D ** -0.5,
        'mu_prev': jax.random.uniform(ks[4], (L, RWKV_COLS), f32, 0.0, 0.4),
        'mu_next': jax.random.uniform(ks[5], (L, RWKV_COLS), f32, 0.0, 0.4),
        'k_k': 0.85 + 0.05 * nrm(ks[6], (L, RW), f32),
        'k_a': 1.0 + 0.05 * nrm(ks[7], (L, RW), f32),
        'r_k': 0.1 * nrm(ks[8], (L, H, N), f32),
        'w0_f': jax.random.uniform(ks[9], (L, RW), f32, -6.0, 1.0),
        'w_lora_f': 0.1 * nrm(ks[10], (L, W_LORA, RW), f32) * W_LORA ** -0.5,
        'w0_b': jax.random.uniform(ks[11], (L, RW), f32, -6.0, 1.0),
        'w_lora_b': 0.1 * nrm(ks[12], (L, W_LORA, RW), f32) * W_LORA ** -0.5,
        'a0_f': 0.1 * nrm(ks[13], (L, RW), f32),
        'a_lora_f': 0.1 * nrm(ks[14], (L, A_LORA, RW), f32) * A_LORA ** -0.5,
        'a0_b': 0.1 * nrm(ks[15], (L, RW), f32),
        'a_lora_b': 0.1 * nrm(ks[16], (L, A_LORA, RW), f32) * A_LORA ** -0.5,
        'g_lora': nrm(ks[17], (L, G_LORA, RW), f32) * G_LORA ** -0.5,
        'lnx_w': 1.0 + 0.02 * nrm(ks[18], (L, RW), f32),
        'lnx_b': 0.01 * nrm(ks[19], (L, RW), f32),
        'q_gain': 1.0 + 0.02 * nrm(ks[20], (L, HEAD_DIM), f32),
        'k_gain': 1.0 + 0.02 * nrm(ks[21], (L, HEAD_DIM), f32),
        'w_out': nrm(ks[22], (L, D, D), f32) * D ** -0.5,
        'norm2_g': 1.0 + 0.02 * nrm(ks[23], (L, D), f32),
        'ffn_gate': nrm(ks[24], (L, D, D_FF), f32) * D ** -0.5,
        'ffn_up': nrm(ks[25], (L, D, D_FF), f32) * D ** -0.5,
        'ffn_down': nrm(ks[26], (L, D_FF, D), f32) * D_FF ** -0.5,
        'norm_f_g': 1.0 + 0.02 * nrm(ks[27], (D,), f32),
    }


def reference(x_prompt, x_sample, norm1_g, w_in, mu_prev, mu_next, k_k, k_a, r_k, w0_f, w_lora_f,
              w0_b, w_lora_b, a0_f, a_lora_f, a0_b, a_lora_b, g_lora, lnx_w, lnx_b, q_gain, k_gain,
              w_out, norm2_g, ffn_gate, ffn_up, ffn_down, norm_f_g):
    layer_params = (norm1_g, w_in, mu_prev, mu_next, k_k, k_a, r_k, w0_f, w_lora_f, w0_b, w_lora_b,
                    a0_f, a_lora_f, a0_b, a_lora_b, g_lora, lnx_w, lnx_b, q_gain, k_gain, w_out,
                    norm2_g, ffn_gate, ffn_up, ffn_down)
    y_prompt = trunk(x_prompt, layer_params, norm_f_g)
    y_sample = trunk(x_sample, layer_params, norm_f_g)
    return (y_prompt, y_sample)
```

```python
import functools

import jax
import jax.numpy as jnp
import numpy as np
from jax import lax
from jax.experimental import pallas as pl
from jax.experimental.pallas import tpu as pltpu

F32 = jnp.float32
BF16 = jnp.bfloat16

D_MODEL = 1024
HEAD_DIM = 64
RWKV_WIDTH = 512
RWKV_HEADS = 8
ATT_WIDTH = 512
ATT_Q_HEADS = 8
ATT_KV_HEADS = 2
ATT_GROUP = 4
KV_WIDTH = 128
RWKV_COLS = 1920
ATT_COLS = 768
D_FF = 2816
GRID_W = 64
ROPE_THETA = 10000.0
ROPE_PAIRS = 16
NORM_EPS = 1e-6
LNX_EPS = 64e-5

LANES = 128
SUBLANES = 8
CHUNK = 128
Q_POS = 128
VT_ROWS = 80
VMEM_LIMIT = 56 * 1024 * 1024


def _dot(a, b):
    return jnp.dot(a, b, preferred_element_type=F32)


def _bf(x):
    return x.astype(BF16)


def _split2(x):
    hi = _bf(x)
    lo = _bf(x - hi.astype(F32))
    return hi, lo


def _split3(x):
    hi = _bf(x)
    r1 = x - hi.astype(F32)
    mid = _bf(r1)
    lo = _bf(r1 - mid.astype(F32))
    return hi, mid, lo


def _dot_exact_rhs(x, m_bf):
    hi, mid, lo = _split3(x)
    return _dot(hi, m_bf) + _dot(mid, m_bf) + _dot(lo, m_bf)


def _dot3(a, b_hi, b_lo):
    a_hi, a_lo = _split2(a)
    return _dot(a_hi, b_hi) + _dot(a_lo, b_hi) + _dot(a_hi, b_lo)


def _sigmoid(x):
    return 1.0 / (1.0 + jnp.exp(-x))


def _softplus(x):
    return jnp.maximum(x, 0.0) + jnp.log(1.0 + jnp.exp(-jnp.abs(x)))


def _params(sem):
    return pltpu.CompilerParams(dimension_semantics=sem, vmem_limit_bytes=VMEM_LIMIT)


def _inproj_kernel(x_ref, g_ref, wr_ref, wa_ref, zr_ref, za_ref):
    x = x_ref[...]
    ms = jnp.mean(x * x, axis=-1, keepdims=True)
    h = _bf(x * lax.rsqrt(ms + NORM_EPS) * g_ref[...])
    zr_ref[...] = _dot(h, wr_ref[...])
    za_ref[...] = _dot(h, wa_ref[...])


def _inproj(x2, norm1_g, w_r, w_a, tm):
    m = x2.shape[0]
    return pl.pallas_call(
        _inproj_kernel,
        grid=(m // tm,),
        in_specs=[
            pl.BlockSpec((tm, D_MODEL), lambda i: (i, 0)),
            pl.BlockSpec((1, D_MODEL), lambda i: (0, 0)),
            pl.BlockSpec((D_MODEL, RWKV_COLS), lambda i: (0, 0)),
            pl.BlockSpec((D_MODEL, ATT_COLS), lambda i: (0, 0)),
        ],
        out_specs=[
            pl.BlockSpec((tm, RWKV_COLS), lambda i: (i, 0)),
            pl.BlockSpec((tm, ATT_COLS), lambda i: (i, 0)),
        ],
        out_shape=[
            jax.ShapeDtypeStruct((m, RWKV_COLS), F32),
            jax.ShapeDtypeStruct((m, ATT_COLS), F32),
        ],
        compiler_params=_params(("parallel",)),
        name="inproj",
    )(x2, norm1_g, w_r, w_a)


def _rwkv_prep_kernel(z_ref, zp_ref, zn_ref, mup_ref, mun_ref, kk_ref, ka_ref, rk_ref,
                      w0_ref, wlh_ref, wll_ref, a0_ref, alh_ref, all_ref, glh_ref, gll_ref,
                      hs_ref,
                      r_o, v_o, kk_o, lw_o, kd_o, b_o, g_o, bg_o, *, tm, blocks_per_seq):
    i = pl.program_id(0)
    pos = i % blocks_per_seq
    z = z_ref[...]
    prev_row = jnp.where(pos == 0, 0.0, zp_ref[SUBLANES - 1:SUBLANES, :])
    next_row = jnp.where(pos == blocks_per_seq - 1, 0.0, zn_ref[0:1, :])
    rows = lax.broadcasted_iota(jnp.int32, (tm, 1), 0)
    z_prev = jnp.where(rows == 0, prev_row, pltpu.roll(z, 1, 0))
    z_next = jnp.where(rows == tm - 1, next_row, pltpu.roll(z, tm - 1, 0))
    zf = z + mup_ref[...] * (z_prev - z) + mun_ref[...] * (z_next - z)

    r = zf[:, 0:512]
    k = zf[:, 512:1024]
    v = zf[:, 1024:1536]
    wd = zf[:, 1536:1664]
    ad = zf[:, 1664:1792]
    gd = zf[:, 1792:1920]
    hs = hs_ref[...]

    kk = k * kk_ref[...]
    ss = _dot_exact_rhs(kk * kk, hs)
    kk = kk * lax.rsqrt(jnp.maximum(ss, 1e-12))

    lw_both = w0_ref[...] + _dot3(jnp.tanh(wd), wlh_ref[...], wll_ref[...])
    as_both = a0_ref[...] + _dot3(ad, alh_ref[...], all_ref[...])
    ka = ka_ref[...]
    kb = jnp.zeros_like(k)
    for d in range(2):
        w_log = -_softplus(-lw_both[:, 512 * d:512 * (d + 1)]) - 0.5
        lw_o[d] = -jnp.exp(w_log)
        a = _sigmoid(as_both[:, 512 * d:512 * (d + 1)])
        kd = k * (1.0 + (a - 1.0) * ka)
        kd_o[d] = kd
        b_o[d] = kk * a
        kb = kb + kd
    kb = 0.5 * kb
    coef = _dot_exact_rhs(r * kb * rk_ref[...], hs)
    g = _dot3(_sigmoid(gd), glh_ref[...], gll_ref[...])
    r_o[...] = r
    v_o[...] = v
    kk_o[...] = kk
    g_o[...] = g
    bg_o[...] = coef * v * g


def _rwkv_prep(z_r, T, p, tm):
    m = z_r.shape[0]
    bps = T // tm
    hb = tm // SUBLANES
    nhalo = m // SUBLANES
    full = lambda shape: pl.BlockSpec(shape, lambda i: tuple(0 for _ in shape))
    tok = pl.BlockSpec((tm, RWKV_WIDTH), lambda i: (i, 0))
    tok2 = pl.BlockSpec((2, tm, RWKV_WIDTH), lambda i: (0, i, 0))
    kern = functools.partial(_rwkv_prep_kernel, tm=tm, blocks_per_seq=bps)
    return pl.pallas_call(
        kern,
        grid=(m // tm,),
        in_specs=[
            pl.BlockSpec((tm, RWKV_COLS), lambda i: (i, 0)),
            pl.BlockSpec((SUBLANES, RWKV_COLS), lambda i: (jnp.maximum(i * hb - 1, 0), 0)),
            pl.BlockSpec((SUBLANES, RWKV_COLS), lambda i: (jnp.minimum((i + 1) * hb, nhalo - 1), 0)),
            full((1, RWKV_COLS)), full((1, RWKV_COLS)),
            full((1, 512)), full((1, 512)), full((1, 512)),
            full((1, 1024)), full((128, 1024)), full((128, 1024)),
            full((1, 1024)), full((128, 1024)), full((128, 1024)),
            full((128, 512)), full((128, 512)),
            full((512, 512)),
        ],
        out_specs=[tok, tok, tok, tok2, tok2, tok2, tok, tok],
        out_shape=[
            jax.ShapeDtypeStruct((m, 512), F32),
            jax.ShapeDtypeStruct((m, 512), F32),
            jax.ShapeDtypeStruct((m, 512), F32),
            jax.ShapeDtypeStruct((2, m, 512), F32),
            jax.ShapeDtypeStruct((2, m, 512), F32),
            jax.ShapeDtypeStruct((2, m, 512), F32),
            jax.ShapeDtypeStruct((m, 512), F32),
            jax.ShapeDtypeStruct((m, 512), F32),
        ],
        compiler_params=_params(("parallel",)),
        name="rwkv_prep",
    )(z_r, z_r, z_r, p["mu_prev"], p["mu_next"], p["k_k"], p["k_a"], p["r_k"],
      p["w0"], p["wl_hi"], p["wl_lo"], p["a0"], p["al_hi"], p["al_lo"], p["gl_hi"], p["gl_lo"],
      p["head_ones"])


def _scan_kernel(r_ref, v_ref, kk_ref, lw_ref, kd_ref, b_ref, y_ref, z_scr):
    C = CHUNK
    d = pl.program_id(1)
    c = pl.program_id(2)

    @pl.when(c == 0)
    def _():
        z_scr[...] = jnp.zeros_like(z_scr)

    row = lax.broadcasted_iota(jnp.int32, (C, C), 0)
    col = lax.broadcasted_iota(jnp.int32, (C, C), 1)
    order = (row - col) * (1 - 2 * d)
    strict = order > 0
    incl = order >= 0
    eye = (row == col).astype(F32)
    tri = incl.astype(BF16)
    lane = lax.broadcasted_iota(jnp.int32, (1, LANES), 1)
    m0 = lane < HEAD_DIM
    m1 = lane >= HEAD_DIM
    blockdiag = (row < HEAD_DIM) == (col < HEAD_DIM)

    lw = lw_ref[0]
    hi, mid, lo = _split3(lw)
    cum = _dot(tri, hi) + _dot(tri, mid) + _dot(tri, lo)
    l_end = jnp.sum(lw, axis=0, keepdims=True)
    l_half = 0.5 * l_end
    r = r_ref[...]
    v = v_ref[...]
    kd = kd_ref[0]
    b = b_ref[0]
    a_sh = -kk_ref[...] * jnp.exp(cum - lw - l_half)
    r_sh = r * jnp.exp(cum - l_half)
    e_b = jnp.exp(l_half - cum)
    b_sh = b * e_b
    k_sh = kd * e_b
    e_end = jnp.exp(l_end - cum)
    b_hat = b * e_end
    k_hat = kd * e_end
    r_full = r * jnp.exp(cum)
    e_half = jnp.exp(l_half)
    p_end = jnp.exp(l_end)

    def both_heads(x):
        return jnp.concatenate([jnp.where(m0, x, 0.0), jnp.where(m1, x, 0.0)], axis=0)

    for p in range(RWKV_HEADS // 2):
        sl = slice(LANES * p, LANES * (p + 1))
        ar = jnp.concatenate([a_sh[:, sl], r_sh[:, sl]], axis=0)
        bk = _bf(jnp.concatenate([b_sh[:, sl], k_sh[:, sl]], axis=0))
        amat = lax.dot_general(_bf(both_heads(ar)), bk, (((1,), (1,)), ((), ())),
                               preferred_element_type=F32)
        t_inv, a_ak, a_rb, a_rk = [], [], [], []
        for hh in range(2):
            ah = amat[2 * C * hh:2 * C * (hh + 1)]
            n = jnp.where(strict, ah[:C, :C], 0.0)
            a_ak.append(jnp.where(strict, ah[:C, C:], 0.0))
            a_rb.append(jnp.where(incl, ah[C:, :C], 0.0))
            a_rk.append(jnp.where(incl, ah[C:, C:], 0.0))
            n_bf = _bf(n)
            pw = _dot(n_bf, n_bf)
            t = eye + n
            for _ in range(5):
                pw_bf = _bf(pw)
                both = _dot(pw_bf, jnp.concatenate([pw_bf, _bf(t)], axis=1))
                pw = both[:, :C]
                t = t + both[:, C:]
            t = t + _dot(_bf(pw), _bf(t))
            t_inv.append(t)
        vp = v[:, sl]
        v2 = _bf(both_heads(vp))
        akrk = jnp.concatenate([jnp.concatenate(a_ak, axis=1), jnp.concatenate(a_rk, axis=1)], axis=0)
        avy = _dot(_bf(akrk), v2)
        av = avy[:C]
        y2 = avy[C:]
        a_p = a_sh[:, sl]
        rhs = jnp.concatenate([
            jnp.concatenate([jnp.where(m0, a_p, 0.0), jnp.where(m0, av, 0.0)], axis=1),
            jnp.concatenate([jnp.where(m1, a_p, 0.0), jnp.where(m1, av, 0.0)], axis=1)], axis=0)
        aw = _dot(_bf(jnp.concatenate(t_inv, axis=1)), _bf(rhs))
        a_bar = aw[:, :LANES] * e_half[:, sl]
        w2 = aw[:, LANES:]

        z = z_scr[p]
        x = _dot(_bf(jnp.concatenate([a_bar, r_full[:, sl]], axis=0)), _bf(z))
        u = x[:C] + w2
        y = x[C:] + y2 + _dot(_bf(jnp.concatenate(a_rb, axis=1)), _bf(both_heads(u)))
        y_ref[0, :, sl] = y
        bkt = jnp.concatenate([b_hat[:, sl].T, k_hat[:, sl].T], axis=1)
        uv = jnp.concatenate([u, vp], axis=0)
        pend_col = jnp.broadcast_to(p_end[:, sl], (LANES, LANES)).T
        z_new = z * pend_col + _dot(_bf(bkt), _bf(uv))
        z_scr[p] = jnp.where(blockdiag, z_new, 0.0)


def _rwkv_scan(r, v, kk, lw, kd, b, B, T):
    m = r.shape[0]
    nc = T // CHUNK

    def blk(bi, d, c):
        return bi * nc + c + d * (nc - 1 - 2 * c)

    tok = pl.BlockSpec((CHUNK, RWKV_WIDTH), lambda bi, d, c: (blk(bi, d, c), 0))
    tok2 = pl.BlockSpec((1, CHUNK, RWKV_WIDTH), lambda bi, d, c: (d, blk(bi, d, c), 0))
    return pl.pallas_call(
        _scan_kernel,
        grid=(B, 2, nc),
        in_specs=[tok, tok, tok, tok2, tok2, tok2],
        out_specs=tok2,
        out_shape=jax.ShapeDtypeStruct((2, m, RWKV_WIDTH), F32),
        scratch_shapes=[pltpu.VMEM((RWKV_HEADS // 2, LANES, LANES), F32)],
        compiler_params=_params(("parallel", "parallel", "arbitrary")),
        name="rwkv_scan",
    )(r, v, kk, lw, kd, b)


def _att_prep_kernel(z_ref, gain_ref, cos_ref, sin_ref, hs_ref, qt_o, k_o, vt_o):
    z = z_ref[...]
    qk = z[:, :640]
    ss = _dot_exact_rhs(qk * qk, hs_ref[...])
    qk = qk * lax.rsqrt(ss * (1.0 / HEAD_DIM) + NORM_EPS) * gain_ref[...]
    width = qk.shape[1]
    lane = lax.broadcasted_iota(jnp.int32, (1, width), 1)
    first = (lane % (2 * ROPE_PAIRS)) < ROPE_PAIRS
    partner = jnp.where(first, pltpu.roll(qk, width - ROPE_PAIRS, 1), pltpu.roll(qk, ROPE_PAIRS, 1))
    cos = jnp.concatenate([cos_ref[...]] * 5, axis=1)
    sin = jnp.concatenate([sin_ref[...]] * 5, axis=1)
    qk = qk * cos + partner * sin
    for j in range(4):
        st = _bf(qk[:, LANES * j:LANES * (j + 1)].T)
        h = j // 2
        g = (2 * j) % ATT_GROUP
        qt_o[0, h, :, Q_POS * g:Q_POS * (g + 1)] = st[:HEAD_DIM]
        qt_o[0, h, :, Q_POS * (g + 1):Q_POS * (g + 2)] = st[HEAD_DIM:]
    k_o[0] = _bf(qk[:, 512:640])
    vt = _bf(z[:, 640:768].T)
    ones = jnp.ones((VT_ROWS - HEAD_DIM, Q_POS), BF16)
    for h in range(ATT_KV_HEADS):
        vt_o[0, h, :HEAD_DIM, :] = vt[HEAD_DIM * h:HEAD_DIM * (h + 1)]
        vt_o[0, h, HEAD_DIM:, :] = ones


def _att_prep(z_a, B, T, p, cos_t, sin_t):
    nb = T // Q_POS
    full = lambda shape: pl.BlockSpec(shape, lambda bi, i: tuple(0 for _ in shape))
    return pl.pallas_call(
        _att_prep_kernel,
        grid=(B, nb),
        in_specs=[
            pl.BlockSpec((Q_POS, ATT_COLS), lambda bi, i: (bi * nb + i, 0)),
            full((1, 640)),
            pl.BlockSpec((Q_POS, LANES), lambda bi, i: (i, 0)),
            pl.BlockSpec((Q_POS, LANES), lambda bi, i: (i, 0)),
            full((640, 640)),
        ],
        out_specs=[
            pl.BlockSpec((1, ATT_KV_HEADS, HEAD_DIM, ATT_GROUP * Q_POS), lambda bi, i: (bi, 0, 0, i)),
            pl.BlockSpec((1, Q_POS, KV_WIDTH), lambda bi, i: (bi, i, 0)),
            pl.BlockSpec((1, ATT_KV_HEADS, VT_ROWS, Q_POS), lambda bi, i: (bi, 0, 0, i)),
        ],
        out_shape=[
            jax.ShapeDtypeStruct((B, ATT_KV_HEADS, HEAD_DIM, ATT_GROUP * T), BF16),
            jax.ShapeDtypeStruct((B, T, KV_WIDTH), BF16),
            jax.ShapeDtypeStruct((B, ATT_KV_HEADS, VT_ROWS, T), BF16),
        ],
        compiler_params=_params(("parallel", "parallel")),
        name="att_prep",
    )(z_a, p["qk_gain"], cos_t, sin_t, p["head_ones_qk"])


def _attn_kernel(qt_ref, k_ref, vt_ref, o_ref, *, n_kv, tkv):
    h = pl.program_id(1)
    ncol = ATT_GROUP * Q_POS
    qt = qt_ref[0, 0]
    rowh = lax.broadcasted_iota(jnp.int32, (KV_WIDTH, 1), 0) // HEAD_DIM
    q2 = jnp.where(rowh == h, jnp.concatenate([qt, qt], axis=0), jnp.zeros((), BF16))

    def body(j, carry):
        m, acc = carry
        start = pl.multiple_of(j * tkv, tkv)
        kc = k_ref[0, pl.ds(start, tkv), :]
        s = _dot(kc, q2)
        m_new = jnp.maximum(m, jnp.max(s, axis=0, keepdims=True))
        alpha = jnp.exp(m - m_new)
        pt = _bf(jnp.exp(s - m_new))
        vc = vt_ref[0, 0, :, pl.ds(start, tkv)]
        return m_new, acc * alpha + _dot(vc, pt)

    m0 = jnp.full((1, ncol), -jnp.inf, F32)
    acc0 = jnp.zeros((VT_ROWS, ncol), F32)
    _, acc = lax.fori_loop(0, n_kv, body, (m0, acc0))
    o = acc[:HEAD_DIM] / acc[HEAD_DIM:HEAD_DIM + 1]
    ot = jnp.concatenate([o, jnp.zeros_like(o)], axis=0).T
    for g in range(ATT_GROUP):
        o_ref[0, :, HEAD_DIM * g:HEAD_DIM * (g + 1)] = _bf(ot[Q_POS * g:Q_POS * (g + 1), :HEAD_DIM])


def _attention(qt, k, vt, B, T, tkv):
    nb = T // Q_POS
    kern = functools.partial(_attn_kernel, n_kv=T // tkv, tkv=tkv)
    return pl.pallas_call(
        kern,
        grid=(B, ATT_KV_HEADS, nb),
        in_specs=[
            pl.BlockSpec((1, 1, HEAD_DIM, ATT_GROUP * Q_POS), lambda bi, h, i: (bi, h, 0, i)),
            pl.BlockSpec((1, T, KV_WIDTH), lambda bi, h, i: (bi, 0, 0)),
            pl.BlockSpec((1, 1, VT_ROWS, T), lambda bi, h, i: (bi, h, 0, 0)),
        ],
        out_specs=pl.BlockSpec((1, Q_POS, ATT_GROUP * HEAD_DIM), lambda bi, h, i: (bi, i, h)),
        out_shape=jax.ShapeDtypeStruct((B, T, ATT_WIDTH), BF16),
        compiler_params=_params(("parallel", "parallel", "arbitrary")),
        name="attention",
    )(qt, k, vt)


def _mixout_kernel(x_ref, y_ref, g_ref, bg_ref, att_ref, lnw_ref, lnb_ref, hs_ref, wo_r_ref, wo_a_ref, o_ref):
    y = y_ref[0] + y_ref[1]
    hs = hs_ref[...]
    mu = _dot_exact_rhs(y, hs) * (1.0 / HEAD_DIM)
    dy = y - mu
    var = _dot_exact_rhs(dy * dy, hs) * (1.0 / HEAD_DIM)
    yn = dy * lax.rsqrt(var + LNX_EPS) * lnw_ref[...] + lnb_ref[...]
    yr = yn * g_ref[...] + bg_ref[...]
    o_ref[...] = x_ref[...] + _dot(_bf(yr), wo_r_ref[...]) + _dot(att_ref[...], wo_a_ref[...])


def _mixout(x2, y, g, bg, att, p, tm):
    m = x2.shape[0]
    full = lambda shape: pl.BlockSpec(shape, lambda i: tuple(0 for _ in shape))
    tok = pl.BlockSpec((tm, 512), lambda i: (i, 0))
    return pl.pallas_call(
        _mixout_kernel,
        grid=(m // tm,),
        in_specs=[
            pl.BlockSpec((tm, D_MODEL), lambda i: (i, 0)),
            pl.BlockSpec((2, tm, 512), lambda i: (0, i, 0)),
            tok, tok, tok,
            full((1, 512)), full((1, 512)), full((512, 512)),
            full((512, D_MODEL)), full((512, D_MODEL)),
        ],
        out_specs=pl.BlockSpec((tm, D_MODEL), lambda i: (i, 0)),
        out_shape=jax.ShapeDtypeStruct((m, D_MODEL), F32),
        compiler_params=_params(("parallel",)),
        name="mixout",
    )(x2, y, g, bg, att, p["lnx_w"], p["lnx_b"], p["head_ones"], p["wo_r"], p["wo_a"])


def _ffn_kernel(x_ref, g2_ref, wg_ref, wu_ref, wd_ref, gf_ref, o_ref, h_scr, acc_scr):
    j = pl.program_id(1)

    @pl.when(j == 0)
    def _():
        x = x_ref[...]
        ms = jnp.mean(x * x, axis=-1, keepdims=True)
        h_scr[...] = _bf(x * lax.rsqrt(ms + NORM_EPS) * g2_ref[...])
        acc_scr[...] = x

    h = h_scr[...]
    gate = _dot(h, wg_ref[...])
    up = _dot(h, wu_ref[...])
    act = gate * _sigmoid(gate) * up
    acc_scr[...] += _dot(_bf(act), wd_ref[...])

    @pl.when(j == pl.num_programs(1) - 1)
    def _():
        xo = acc_scr[...]
        ms = jnp.mean(xo * xo, axis=-1, keepdims=True)
        o_ref[...] = xo * lax.rsqrt(ms + NORM_EPS) * gf_ref[...]


def _ffn(x1, p, tm, tf):
    m = x1.shape[0]
    return pl.pallas_call(
        _ffn_kernel,
        grid=(m // tm, D_FF // tf),
        in_specs=[
            pl.BlockSpec((tm, D_MODEL), lambda i, j: (i, 0)),
            pl.BlockSpec((1, D_MODEL), lambda i, j: (0, 0)),
            pl.BlockSpec((D_MODEL, tf), lambda i, j: (0, j)),
            pl.BlockSpec((D_MODEL, tf), lambda i, j: (0, j)),
            pl.BlockSpec((tf, D_MODEL), lambda i, j: (j, 0)),
            pl.BlockSpec((1, D_MODEL), lambda i, j: (0, 0)),
        ],
        out_specs=pl.BlockSpec((tm, D_MODEL), lambda i, j: (i, 0)),
        out_shape=jax.ShapeDtypeStruct((m, D_MODEL), F32),
        scratch_shapes=[pltpu.VMEM((tm, D_MODEL), BF16), pltpu.VMEM((tm, D_MODEL), F32)],
        compiler_params=_params(("parallel", "arbitrary")),
        name="ffn",
    )(x1, p["norm2_g"], p["ffn_gate"], p["ffn_up"], p["ffn_down"], p["norm_f_g"])


def _rope_tables(T):
    n_rows = T // GRID_W
    t = jnp.arange(T, dtype=jnp.int32)
    row = (t // GRID_W).astype(F32)
    col = (t % GRID_W).astype(F32)
    inv = ROPE_THETA ** (-jnp.arange(ROPE_PAIRS, dtype=F32) / ROPE_PAIRS)
    ar = row[:, None] * inv
    ac = col[:, None] * inv
    cos = jnp.concatenate([jnp.cos(ar), jnp.cos(ar), jnp.cos(ac), jnp.cos(ac)], axis=1)
    sin = jnp.concatenate([-jnp.sin(ar), jnp.sin(ar), -jnp.sin(ac), jnp.sin(ac)], axis=1)
    del n_rows
    return jnp.tile(cos, (1, 2)), jnp.tile(sin, (1, 2))


def _block_diag2(a, b):
    za = jnp.zeros_like(a)
    return jnp.concatenate([jnp.concatenate([a, za], axis=1), jnp.concatenate([za, b], axis=1)], axis=0)


def _hi_lo(w):
    hi = w.astype(BF16)
    return hi, (w - hi.astype(F32)).astype(BF16)


def _prepare_params(norm1_g, w_in, mu_prev, mu_next, k_k, k_a, r_k, w0_f, w_lora_f, w0_b, w_lora_b,
                    a0_f, a_lora_f, a0_b, a_lora_b, g_lora, lnx_w, lnx_b, q_gain, k_gain, w_out,
                    norm2_g, ffn_gate, ffn_up, ffn_down, norm_f_g):
    l = 0
    p = {}
    p["norm1_g"] = norm1_g[l][None]
    p["w_r"] = w_in[l][:, :RWKV_COLS].astype(BF16)
    p["w_a"] = w_in[l][:, RWKV_COLS:].astype(BF16)
    p["mu_prev"] = mu_prev[l][None]
    p["mu_next"] = mu_next[l][None]
    p["k_k"] = k_k[l][None]
    p["k_a"] = k_a[l][None]
    p["r_k"] = r_k[l].reshape(1, RWKV_WIDTH)
    p["w0"] = jnp.concatenate([w0_f[l], w0_b[l]])[None]
    p["wl_hi"], p["wl_lo"] = _hi_lo(_block_diag2(w_lora_f[l], w_lora_b[l]))
    p["a0"] = jnp.concatenate([a0_f[l], a0_b[l]])[None]
    p["al_hi"], p["al_lo"] = _hi_lo(_block_diag2(a_lora_f[l], a_lora_b[l]))
    p["gl_hi"], p["gl_lo"] = _hi_lo(g_lora[l])
    p["lnx_w"] = lnx_w[l][None]
    p["lnx_b"] = lnx_b[l][None]
    scale = HEAD_DIM ** -0.5
    p["qk_gain"] = jnp.concatenate([jnp.tile(q_gain[l] * scale, ATT_Q_HEADS), jnp.tile(k_gain[l], ATT_KV_HEADS)])[None]
    hid = np.arange(640) // HEAD_DIM
    ones = (hid[:, None] == hid[None, :]).astype(np.float32)
    p["head_ones_qk"] = jnp.asarray(ones, BF16)
    p["head_ones"] = jnp.asarray(ones[:512, :512], BF16)
    p["wo_r"] = w_out[l][:RWKV_WIDTH].astype(BF16)
    p["wo_a"] = w_out[l][RWKV_WIDTH:].astype(BF16)
    p["norm2_g"] = norm2_g[l][None]
    p["ffn_gate"] = ffn_gate[l].astype(BF16)
    p["ffn_up"] = ffn_up[l].astype(BF16)
    p["ffn_down"] = ffn_down[l].astype(BF16)
    p["norm_f_g"] = norm_f_g[None]
    return p


def _tile(n, pref):
    t = pref
    while n % t:
        t //= 2
    return t


def _trunk(x, p):
    B, T, D = x.shape
    m = B * T
    x2 = x.reshape(m, D)
    z_r, z_a = _inproj(x2, p["norm1_g"], p["w_r"], p["w_a"], _tile(m, 512))
    r, v, kk, lw, kd, b, g, bg = _rwkv_prep(z_r, T, p, _tile(T, 256))
    y = _rwkv_scan(r, v, kk, lw, kd, b, B, T)
    cos_t, sin_t = _rope_tables(T)
    qt, k, vt = _att_prep(z_a, B, T, p, cos_t, sin_t)
    att = _attention(qt, k, vt, B, T, _tile(T, 512)).reshape(m, ATT_WIDTH)
    x1 = _mixout(x2, y, g, bg, att, p, _tile(m, 512))
    out = _ffn(x1, p, _tile(m, 512), 1408)
    return out.reshape(B, T, D)


def kernel(x_prompt, x_sample, norm1_g, w_in, mu_prev, mu_next, k_k, k_a, r_k, w0_f, w_lora_f, w0_b, w_lora_b, a0_f, a_lora_f, a0_b, a_lora_b, g_lora, lnx_w, lnx_b, q_gain, k_gain, w_out, norm2_g, ffn_gate, ffn_up, ffn_down, norm_f_g):
    p = _prepare_params(norm1_g, w_in, mu_prev, mu_next, k_k, k_a, r_k, w0_f, w_lora_f, w0_b, w_lora_b,
                        a0_f, a_lora_f, a0_b, a_lora_b, g_lora, lnx_w, lnx_b, q_gain, k_gain, w_out,
                        norm2_g, ffn_gate, ffn_up, ffn_down, norm_f_g)
    return (_trunk(x_prompt, p), _trunk(x_sample, p))
```

```python
import functools

import jax
import jax.numpy as jnp
import numpy as np
from jax import lax
from jax.experimental import pallas as pl
from jax.experimental.pallas import tpu as pltpu

F32 = jnp.float32
BF16 = jnp.bfloat16

D_MODEL = 1024
HEAD_DIM = 64
RWKV_WIDTH = 512
RWKV_HEADS = 8
ATT_WIDTH = 512
ATT_Q_HEADS = 8
ATT_KV_HEADS = 2
ATT_GROUP = 4
KV_WIDTH = 128
RWKV_COLS = 1920
ATT_COLS = 768
D_FF = 2816
GRID_W = 64
ROPE_THETA = 10000.0
ROPE_PAIRS = 16
NORM_EPS = 1e-6
LNX_EPS = 64e-5

LANES = 128
SUBLANES = 8
CHUNK = 128
Q_POS = 128
KV_UNROLL = 4
VT_ROWS = 80
VMEM_LIMIT = 56 * 1024 * 1024


def _dot(a, b):
    return jnp.dot(a, b, preferred_element_type=F32)


def _bf(x):
    return x.astype(BF16)


def _split2(x):
    hi = _bf(x)
    lo = _bf(x - hi.astype(F32))
    return hi, lo


def _split3(x):
    hi = _bf(x)
    r1 = x - hi.astype(F32)
    mid = _bf(r1)
    lo = _bf(r1 - mid.astype(F32))
    return hi, mid, lo


def _dot_exact_rhs(x, m_bf):
    hi, mid, lo = _split3(x)
    return _dot(hi, m_bf) + _dot(mid, m_bf) + _dot(lo, m_bf)


def _dot3(a, b_hi, b_lo):
    a_hi, a_lo = _split2(a)
    return _dot(a_hi, b_hi) + _dot(a_lo, b_hi) + _dot(a_hi, b_lo)


def _sigmoid(x):
    return 1.0 / (1.0 + jnp.exp(-x))


def _softplus(x):
    return jnp.maximum(x, 0.0) + jnp.log(1.0 + jnp.exp(-jnp.abs(x)))


def _params(sem):
    return pltpu.CompilerParams(dimension_semantics=sem, vmem_limit_bytes=VMEM_LIMIT)


def _inproj_kernel(x_ref, g_ref, wr_ref, wa_ref, zr_ref, za_ref):
    x = x_ref[...]
    ms = jnp.mean(x * x, axis=-1, keepdims=True)
    h = _bf(x * lax.rsqrt(ms + NORM_EPS) * g_ref[...])
    zr_ref[...] = _dot(h, wr_ref[...])
    za_ref[...] = _dot(h, wa_ref[...])


def _inproj(x2, norm1_g, w_r, w_a, tm):
    m = x2.shape[0]
    return pl.pallas_call(
        _inproj_kernel,
        grid=(m // tm,),
        in_specs=[
            pl.BlockSpec((tm, D_MODEL), lambda i: (i, 0)),
            pl.BlockSpec((1, D_MODEL), lambda i: (0, 0)),
            pl.BlockSpec((D_MODEL, RWKV_COLS), lambda i: (0, 0)),
            pl.BlockSpec((D_MODEL, ATT_COLS), lambda i: (0, 0)),
        ],
        out_specs=[
            pl.BlockSpec((tm, RWKV_COLS), lambda i: (i, 0)),
            pl.BlockSpec((tm, ATT_COLS), lambda i: (i, 0)),
        ],
        out_shape=[
            jax.ShapeDtypeStruct((m, RWKV_COLS), F32),
            jax.ShapeDtypeStruct((m, ATT_COLS), F32),
        ],
        compiler_params=_params(("parallel",)),
        name="inproj",
    )(x2, norm1_g, w_r, w_a)


def _rwkv_prep_kernel(z_ref, zp_ref, zn_ref, mup_ref, mun_ref, kk_ref, ka_ref, rk_ref,
                      w0_ref, wlh_ref, wll_ref, a0_ref, alh_ref, all_ref, glh_ref, gll_ref,
                      hs_ref,
                      r_o, v_o, kk_o, lw_o, kd_o, b_o, g_o, bg_o, *, tm, blocks_per_seq):
    i = pl.program_id(0)
    pos = i % blocks_per_seq
    z = z_ref[...]
    prev_row = jnp.where(pos == 0, 0.0, zp_ref[SUBLANES - 1:SUBLANES, :])
    next_row = jnp.where(pos == blocks_per_seq - 1, 0.0, zn_ref[0:1, :])
    rows = lax.broadcasted_iota(jnp.int32, (tm, 1), 0)
    z_prev = jnp.where(rows == 0, prev_row, pltpu.roll(z, 1, 0))
    z_next = jnp.where(rows == tm - 1, next_row, pltpu.roll(z, tm - 1, 0))
    zf = z + mup_ref[...] * (z_prev - z) + mun_ref[...] * (z_next - z)

    r = zf[:, 0:512]
    k = zf[:, 512:1024]
    v = zf[:, 1024:1536]
    wd = zf[:, 1536:1664]
    ad = zf[:, 1664:1792]
    gd = zf[:, 1792:1920]
    hs = hs_ref[...]

    kk = k * kk_ref[...]
    ss = _dot_exact_rhs(kk * kk, hs)
    kk = kk * lax.rsqrt(jnp.maximum(ss, 1e-12))

    lw_both = w0_ref[...] + _dot3(jnp.tanh(wd), wlh_ref[...], wll_ref[...])
    as_both = a0_ref[...] + _dot3(ad, alh_ref[...], all_ref[...])
    ka = ka_ref[...]
    kb = jnp.zeros_like(k)
    for d in range(2):
        w_log = -_softplus(-lw_both[:, 512 * d:512 * (d + 1)]) - 0.5
        lw_o[d] = -jnp.exp(w_log)
        a = _sigmoid(as_both[:, 512 * d:512 * (d + 1)])
        kd = k * (1.0 + (a - 1.0) * ka)
        kd_o[d] = kd
        b_o[d] = kk * a
        kb = kb + kd
    kb = 0.5 * kb
    coef = _dot_exact_rhs(r * kb * rk_ref[...], hs)
    g = _dot3(_sigmoid(gd), glh_ref[...], gll_ref[...])
    r_o[...] = r
    v_o[...] = v
    kk_o[...] = kk
    g_o[...] = g
    bg_o[...] = coef * v * g


def _rwkv_prep(z_r, T, p, tm):
    m = z_r.shape[0]
    bps = T // tm
    hb = tm // SUBLANES
    nhalo = m // SUBLANES
    full = lambda shape: pl.BlockSpec(shape, lambda i: tuple(0 for _ in shape))
    tok = pl.BlockSpec((tm, RWKV_WIDTH), lambda i: (i, 0))
    tok2 = pl.BlockSpec((2, tm, RWKV_WIDTH), lambda i: (0, i, 0))
    kern = functools.partial(_rwkv_prep_kernel, tm=tm, blocks_per_seq=bps)
    return pl.pallas_call(
        kern,
        grid=(m // tm,),
        in_specs=[
            pl.BlockSpec((tm, RWKV_COLS), lambda i: (i, 0)),
            pl.BlockSpec((SUBLANES, RWKV_COLS), lambda i: (jnp.maximum(i * hb - 1, 0), 0)),
            pl.BlockSpec((SUBLANES, RWKV_COLS), lambda i: (jnp.minimum((i + 1) * hb, nhalo - 1), 0)),
            full((1, RWKV_COLS)), full((1, RWKV_COLS)),
            full((1, 512)), full((1, 512)), full((1, 512)),
            full((1, 1024)), full((128, 1024)), full((128, 1024)),
            full((1, 1024)), full((128, 1024)), full((128, 1024)),
            full((128, 512)), full((128, 512)),
            full((512, 512)),
        ],
        out_specs=[tok, tok, tok, tok2, tok2, tok2, tok, tok],
        out_shape=[
            jax.ShapeDtypeStruct((m, 512), F32),
            jax.ShapeDtypeStruct((m, 512), F32),
            jax.ShapeDtypeStruct((m, 512), F32),
            jax.ShapeDtypeStruct((2, m, 512), F32),
            jax.ShapeDtypeStruct((2, m, 512), F32),
            jax.ShapeDtypeStruct((2, m, 512), F32),
            jax.ShapeDtypeStruct((m, 512), F32),
            jax.ShapeDtypeStruct((m, 512), F32),
        ],
        compiler_params=_params(("parallel",)),
        name="rwkv_prep",
    )(z_r, z_r, z_r, p["mu_prev"], p["mu_next"], p["k_k"], p["k_a"], p["r_k"],
      p["w0"], p["wl_hi"], p["wl_lo"], p["a0"], p["al_hi"], p["al_lo"], p["gl_hi"], p["gl_lo"],
      p["head_ones"])


def _scan_kernel(r_ref, v_ref, kk_ref, lw_ref, kd_ref, b_ref, y_ref, z_scr):
    C = CHUNK
    d = pl.program_id(1)
    c = pl.program_id(2)

    @pl.when(c == 0)
    def _():
        z_scr[...] = jnp.zeros_like(z_scr)

    row = lax.broadcasted_iota(jnp.int32, (C, C), 0)
    col = lax.broadcasted_iota(jnp.int32, (C, C), 1)
    order = (row - col) * (1 - 2 * d)
    strict = order > 0
    incl = order >= 0
    eye = (row == col).astype(F32)
    tri = incl.astype(BF16)
    lane = lax.broadcasted_iota(jnp.int32, (1, LANES), 1)
    m0 = lane < HEAD_DIM
    m1 = lane >= HEAD_DIM
    blockdiag = (row < HEAD_DIM) == (col < HEAD_DIM)

    lw = lw_ref[0]
    hi, mid, lo = _split3(lw)
    cum = _dot(tri, hi) + _dot(tri, mid) + _dot(tri, lo)
    l_end = jnp.sum(lw, axis=0, keepdims=True)
    l_half = 0.5 * l_end
    r = r_ref[...]
    v = v_ref[...]
    kd = kd_ref[0]
    b = b_ref[0]
    a_sh = -kk_ref[...] * jnp.exp(cum - lw - l_half)
    r_sh = r * jnp.exp(cum - l_half)
    e_b = jnp.exp(l_half - cum)
    b_sh = b * e_b
    k_sh = kd * e_b
    e_end = jnp.exp(l_end - cum)
    b_hat = b * e_end
    k_hat = kd * e_end
    r_full = r * jnp.exp(cum)
    e_half = jnp.exp(l_half)
    p_end = jnp.exp(l_end)

    def both_heads(x):
        return jnp.concatenate([jnp.where(m0, x, 0.0), jnp.where(m1, x, 0.0)], axis=0)

    pairs = range(RWKV_HEADS // 2)
    heads = range(RWKV_HEADS)
    lanes = [slice(LANES * p, LANES * (p + 1)) for p in pairs]

    amat = []
    for p in pairs:
        sl = lanes[p]
        ar = jnp.concatenate([a_sh[:, sl], r_sh[:, sl]], axis=0)
        bk = _bf(jnp.concatenate([b_sh[:, sl], k_sh[:, sl]], axis=0))
        amat.append(lax.dot_general(_bf(both_heads(ar)), bk, (((1,), (1,)), ((), ())),
                                    preferred_element_type=F32))
    n_bf, a_ak, a_rb, a_rk, t_inv = [], [], [], [], []
    for hd in heads:
        ah = amat[hd // 2][2 * C * (hd % 2):2 * C * (hd % 2 + 1)]
        n = jnp.where(strict, ah[:C, :C], 0.0)
        n_bf.append(_bf(n))
        t_inv.append(eye + n)
        a_ak.append(jnp.where(strict, ah[:C, C:], 0.0))
        a_rb.append(jnp.where(incl, ah[C:, :C], 0.0))
        a_rk.append(jnp.where(incl, ah[C:, C:], 0.0))
    pw = [_bf(_dot(n_bf[hd], n_bf[hd])) for hd in heads]
    avy = []
    for p in pairs:
        akrk = jnp.concatenate([jnp.concatenate(a_ak[2 * p:2 * p + 2], axis=1),
                                jnp.concatenate(a_rk[2 * p:2 * p + 2], axis=1)], axis=0)
        avy.append(_dot(_bf(akrk), _bf(both_heads(v[:, lanes[p]]))))
    for _ in range(5):
        both = [_dot(pw[hd], jnp.concatenate([pw[hd], _bf(t_inv[hd])], axis=1)) for hd in heads]
        pw = [_bf(both[hd][:, :C]) for hd in heads]
        t_inv = [t_inv[hd] + both[hd][:, C:] for hd in heads]
    last = [_dot(pw[hd], _bf(t_inv[hd])) for hd in heads]
    t_inv = [t_inv[hd] + last[hd] for hd in heads]

    aw = []
    for p in pairs:
        a_p = a_sh[:, lanes[p]]
        av = avy[p][:C]
        rhs = jnp.concatenate([
            jnp.concatenate([jnp.where(m0, a_p, 0.0), jnp.where(m0, av, 0.0)], axis=1),
            jnp.concatenate([jnp.where(m1, a_p, 0.0), jnp.where(m1, av, 0.0)], axis=1)], axis=0)
        aw.append(_dot(_bf(jnp.concatenate(t_inv[2 * p:2 * p + 2], axis=1)), _bf(rhs)))

    zs = [z_scr[p] for p in pairs]
    xs = []
    for p in pairs:
        a_bar = aw[p][:, :LANES] * e_half[:, lanes[p]]
        xs.append(_dot(_bf(jnp.concatenate([a_bar, r_full[:, lanes[p]]], axis=0)), _bf(zs[p])))
    us = [xs[p][:C] + aw[p][:, LANES:] for p in pairs]
    for p in pairs:
        y = xs[p][C:] + avy[p][C:] + _dot(_bf(jnp.concatenate(a_rb[2 * p:2 * p + 2], axis=1)),
                                           _bf(both_heads(us[p])))
        y_ref[0, :, lanes[p]] = y
    for p in pairs:
        sl = lanes[p]
        bkt = jnp.concatenate([b_hat[:, sl].T, k_hat[:, sl].T], axis=1)
        uv = jnp.concatenate([us[p], v[:, sl]], axis=0)
        pend_col = jnp.broadcast_to(p_end[:, sl], (LANES, LANES)).T
        z_new = zs[p] * pend_col + _dot(_bf(bkt), _bf(uv))
        z_scr[p] = jnp.where(blockdiag, z_new, 0.0)


def _rwkv_scan(r, v, kk, lw, kd, b, B, T):
    m = r.shape[0]
    nc = T // CHUNK

    def blk(bi, d, c):
        return bi * nc + c + d * (nc - 1 - 2 * c)

    tok = pl.BlockSpec((CHUNK, RWKV_WIDTH), lambda bi, d, c: (blk(bi, d, c), 0))
    tok2 = pl.BlockSpec((1, CHUNK, RWKV_WIDTH), lambda bi, d, c: (d, blk(bi, d, c), 0))
    return pl.pallas_call(
        _scan_kernel,
        grid=(B, 2, nc),
        in_specs=[tok, tok, tok, tok2, tok2, tok2],
        out_specs=tok2,
        out_shape=jax.ShapeDtypeStruct((2, m, RWKV_WIDTH), F32),
        scratch_shapes=[pltpu.VMEM((RWKV_HEADS // 2, LANES, LANES), F32)],
        compiler_params=_params(("parallel", "parallel", "arbitrary")),
        name="rwkv_scan",
    )(r, v, kk, lw, kd, b)


def _att_prep_kernel(z_ref, gain_ref, cos_ref, sin_ref, hs_ref, qt_o, k_o, vt_o):
    z = z_ref[...]
    qk = z[:, :640]
    ss = _dot_exact_rhs(qk * qk, hs_ref[...])
    qk = qk * lax.rsqrt(ss * (1.0 / HEAD_DIM) + NORM_EPS) * gain_ref[...]
    width = qk.shape[1]
    lane = lax.broadcasted_iota(jnp.int32, (1, width), 1)
    first = (lane % (2 * ROPE_PAIRS)) < ROPE_PAIRS
    partner = jnp.where(first, pltpu.roll(qk, width - ROPE_PAIRS, 1), pltpu.roll(qk, ROPE_PAIRS, 1))
    cos = jnp.concatenate([cos_ref[...]] * 5, axis=1)
    sin = jnp.concatenate([sin_ref[...]] * 5, axis=1)
    qk = qk * cos + partner * sin
    for j in range(4):
        st = _bf(qk[:, LANES * j:LANES * (j + 1)].T)
        h = j // 2
        g = (2 * j) % ATT_GROUP
        qt_o[0, h, :, Q_POS * g:Q_POS * (g + 1)] = st[:HEAD_DIM]
        qt_o[0, h, :, Q_POS * (g + 1):Q_POS * (g + 2)] = st[HEAD_DIM:]
    k_o[0] = _bf(qk[:, 512:640])
    vt = _bf(z[:, 640:768].T)
    ones = jnp.ones((VT_ROWS - HEAD_DIM, Q_POS), BF16)
    for h in range(ATT_KV_HEADS):
        vt_o[0, h, :HEAD_DIM, :] = vt[HEAD_DIM * h:HEAD_DIM * (h + 1)]
        vt_o[0, h, HEAD_DIM:, :] = ones


def _att_prep(z_a, B, T, p, cos_t, sin_t):
    nb = T // Q_POS
    full = lambda shape: pl.BlockSpec(shape, lambda bi, i: tuple(0 for _ in shape))
    return pl.pallas_call(
        _att_prep_kernel,
        grid=(B, nb),
        in_specs=[
            pl.BlockSpec((Q_POS, ATT_COLS), lambda bi, i: (bi * nb + i, 0)),
            full((1, 640)),
            pl.BlockSpec((Q_POS, LANES), lambda bi, i: (i, 0)),
            pl.BlockSpec((Q_POS, LANES), lambda bi, i: (i, 0)),
            full((640, 640)),
        ],
        out_specs=[
            pl.BlockSpec((1, ATT_KV_HEADS, HEAD_DIM, ATT_GROUP * Q_POS), lambda bi, i: (bi, 0, 0, i)),
            pl.BlockSpec((1, Q_POS, KV_WIDTH), lambda bi, i: (bi, i, 0)),
            pl.BlockSpec((1, ATT_KV_HEADS, VT_ROWS, Q_POS), lambda bi, i: (bi, 0, 0, i)),
        ],
        out_shape=[
            jax.ShapeDtypeStruct((B, ATT_KV_HEADS, HEAD_DIM, ATT_GROUP * T), BF16),
            jax.ShapeDtypeStruct((B, T, KV_WIDTH), BF16),
            jax.ShapeDtypeStruct((B, ATT_KV_HEADS, VT_ROWS, T), BF16),
        ],
        compiler_params=_params(("parallel", "parallel")),
        name="att_prep",
    )(z_a, p["qk_gain"], cos_t, sin_t, p["head_ones_qk"])


def _attn_kernel(qt_ref, k_ref, vt_ref, o_ref, *, n_kv, tkv, unroll):
    h = pl.program_id(1)
    ncol = ATT_GROUP * Q_POS
    qt = qt_ref[0, 0]
    rowh = lax.broadcasted_iota(jnp.int32, (KV_WIDTH, 1), 0) // HEAD_DIM
    q2 = jnp.where(rowh == h, jnp.concatenate([qt, qt], axis=0), jnp.zeros((), BF16))

    def chunk_start(j):
        return j * tkv if isinstance(j, int) else pl.multiple_of(j * tkv, tkv)

    def scores(j):
        kc = k_ref[0, pl.ds(chunk_start(j), tkv), :]
        return _dot(kc, q2)

    def update(j, m, acc, s):
        m_new = jnp.maximum(m, jnp.max(s, axis=0, keepdims=True))
        alpha = jnp.exp2(m - m_new)
        pt = _bf(jnp.exp2(s - m_new))
        vc = vt_ref[0, 0, :, pl.ds(chunk_start(j), tkv)]
        return m_new, acc * alpha + _dot(vc, pt)

    def group(base, m, acc, s, final):
        for u in range(unroll):
            s_next = None if (final and u == unroll - 1) else scores(base + u + 1)
            m, acc = update(base + u, m, acc, s)
            s = s_next
        return m, acc, s

    m = jnp.full((1, ncol), -jnp.inf, F32)
    acc = jnp.zeros((VT_ROWS, ncol), F32)
    s = scores(0)
    n_groups = n_kv // unroll
    if n_groups > 1:
        m, acc, s = lax.fori_loop(0, n_groups - 1, lambda j, c: group(j * unroll, *c, False), (m, acc, s))
    _, acc, _ = group((n_groups - 1) * unroll, m, acc, s, True)
    o = acc[:HEAD_DIM] / acc[HEAD_DIM:HEAD_DIM + 1]
    ot = jnp.concatenate([o, jnp.zeros_like(o)], axis=0).T
    for g in range(ATT_GROUP):
        o_ref[0, :, HEAD_DIM * g:HEAD_DIM * (g + 1)] = _bf(ot[Q_POS * g:Q_POS * (g + 1), :HEAD_DIM])


def _attention(qt, k, vt, B, T, tkv):
    nb = T // Q_POS
    n_kv = T // tkv
    kern = functools.partial(_attn_kernel, n_kv=n_kv, tkv=tkv, unroll=_tile(n_kv, KV_UNROLL))
    return pl.pallas_call(
        kern,
        grid=(B, ATT_KV_HEADS, nb),
        in_specs=[
            pl.BlockSpec((1, 1, HEAD_DIM, ATT_GROUP * Q_POS), lambda bi, h, i: (bi, h, 0, i)),
            pl.BlockSpec((1, T, KV_WIDTH), lambda bi, h, i: (bi, 0, 0)),
            pl.BlockSpec((1, 1, VT_ROWS, T), lambda bi, h, i: (bi, h, 0, 0)),
        ],
        out_specs=pl.BlockSpec((1, Q_POS, ATT_GROUP * HEAD_DIM), lambda bi, h, i: (bi, i, h)),
        out_shape=jax.ShapeDtypeStruct((B, T, ATT_WIDTH), BF16),
        compiler_params=_params(("parallel", "parallel", "arbitrary")),
        name="attention",
    )(qt, k, vt)


def _mixout_kernel(x_ref, y_ref, g_ref, bg_ref, att_ref, lnw_ref, lnb_ref, hs_ref, wo_r_ref, wo_a_ref, o_ref):
    y = y_ref[0] + y_ref[1]
    hs = hs_ref[...]
    mu = _dot_exact_rhs(y, hs) * (1.0 / HEAD_DIM)
    dy = y - mu
    var = _dot_exact_rhs(dy * dy, hs) * (1.0 / HEAD_DIM)
    yn = dy * lax.rsqrt(var + LNX_EPS) * lnw_ref[...] + lnb_ref[...]
    yr = yn * g_ref[...] + bg_ref[...]
    o_ref[...] = x_ref[...] + _dot(_bf(yr), wo_r_ref[...]) + _dot(att_ref[...], wo_a_ref[...])


def _mixout(x2, y, g, bg, att, p, tm):
    m = x2.shape[0]
    full = lambda shape: pl.BlockSpec(shape, lambda i: tuple(0 for _ in shape))
    tok = pl.BlockSpec((tm, 512), lambda i: (i, 0))
    return pl.pallas_call(
        _mixout_kernel,
        grid=(m // tm,),
        in_specs=[
            pl.BlockSpec((tm, D_MODEL), lambda i: (i, 0)),
            pl.BlockSpec((2, tm, 512), lambda i: (0, i, 0)),
            tok, tok, tok,
            full((1, 512)), full((1, 512)), full((512, 512)),
            full((512, D_MODEL)), full((512, D_MODEL)),
        ],
        out_specs=pl.BlockSpec((tm, D_MODEL), lambda i: (i, 0)),
        out_shape=jax.ShapeDtypeStruct((m, D_MODEL), F32),
        compiler_params=_params(("parallel",)),
        name="mixout",
    )(x2, y, g, bg, att, p["lnx_w"], p["lnx_b"], p["head_ones"], p["wo_r"], p["wo_a"])


def _ffn_kernel(x_ref, g2_ref, wg_ref, wu_ref, wd_ref, gf_ref, o_ref, h_scr, acc_scr):
    j = pl.program_id(1)

    @pl.when(j == 0)
    def _():
        x = x_ref[...]
        ms = jnp.mean(x * x, axis=-1, keepdims=True)
        h_scr[...] = _bf(x * lax.rsqrt(ms + NORM_EPS) * g2_ref[...])
        acc_scr[...] = x

    h = h_scr[...]
    gate = _dot(h, wg_ref[...])
    up = _dot(h, wu_ref[...])
    act = gate * _sigmoid(gate) * up
    acc_scr[...] += _dot(_bf(act), wd_ref[...])

    @pl.when(j == pl.num_programs(1) - 1)
    def _():
        xo = acc_scr[...]
        ms = jnp.mean(xo * xo, axis=-1, keepdims=True)
        o_ref[...] = xo * lax.rsqrt(ms + NORM_EPS) * gf_ref[...]


def _ffn(x1, p, tm, tf):
    m = x1.shape[0]
    return pl.pallas_call(
        _ffn_kernel,
        grid=(m // tm, D_FF // tf),
        in_specs=[
            pl.BlockSpec((tm, D_MODEL), lambda i, j: (i, 0)),
            pl.BlockSpec((1, D_MODEL), lambda i, j: (0, 0)),
            pl.BlockSpec((D_MODEL, tf), lambda i, j: (0, j)),
            pl.BlockSpec((D_MODEL, tf), lambda i, j: (0, j)),
            pl.BlockSpec((tf, D_MODEL), lambda i, j: (j, 0)),
            pl.BlockSpec((1, D_MODEL), lambda i, j: (0, 0)),
        ],
        out_specs=pl.BlockSpec((tm, D_MODEL), lambda i, j: (i, 0)),
        out_shape=jax.ShapeDtypeStruct((m, D_MODEL), F32),
        scratch_shapes=[pltpu.VMEM((tm, D_MODEL), BF16), pltpu.VMEM((tm, D_MODEL), F32)],
        compiler_params=_params(("parallel", "arbitrary")),
        name="ffn",
    )(x1, p["norm2_g"], p["ffn_gate"], p["ffn_up"], p["ffn_down"], p["norm_f_g"])


def _rope_tables(T):
    n_rows = T // GRID_W
    t = jnp.arange(T, dtype=jnp.int32)
    row = (t // GRID_W).astype(F32)
    col = (t % GRID_W).astype(F32)
    inv = ROPE_THETA ** (-jnp.arange(ROPE_PAIRS, dtype=F32) / ROPE_PAIRS)
    ar = row[:, None] * inv
    ac = col[:, None] * inv
    cos = jnp.concatenate([jnp.cos(ar), jnp.cos(ar), jnp.cos(ac), jnp.cos(ac)], axis=1)
    sin = jnp.concatenate([-jnp.sin(ar), jnp.sin(ar), -jnp.sin(ac), jnp.sin(ac)], axis=1)
    del n_rows
    return jnp.tile(cos, (1, 2)), jnp.tile(sin, (1, 2))


def _block_diag2(a, b):
    za = jnp.zeros_like(a)
    return jnp.concatenate([jnp.concatenate([a, za], axis=1), jnp.concatenate([za, b], axis=1)], axis=0)


def _hi_lo(w):
    hi = w.astype(BF16)
    return hi, (w - hi.astype(F32)).astype(BF16)


def _prepare_params(norm1_g, w_in, mu_prev, mu_next, k_k, k_a, r_k, w0_f, w_lora_f, w0_b, w_lora_b,
                    a0_f, a_lora_f, a0_b, a_lora_b, g_lora, lnx_w, lnx_b, q_gain, k_gain, w_out,
                    norm2_g, ffn_gate, ffn_up, ffn_down, norm_f_g):
    l = 0
    p = {}
    p["norm1_g"] = norm1_g[l][None]
    p["w_r"] = w_in[l][:, :RWKV_COLS].astype(BF16)
    p["w_a"] = w_in[l][:, RWKV_COLS:].astype(BF16)
    p["mu_prev"] = mu_prev[l][None]
    p["mu_next"] = mu_next[l][None]
    p["k_k"] = k_k[l][None]
    p["k_a"] = k_a[l][None]
    p["r_k"] = r_k[l].reshape(1, RWKV_WIDTH)
    p["w0"] = jnp.concatenate([w0_f[l], w0_b[l]])[None]
    p["wl_hi"], p["wl_lo"] = _hi_lo(_block_diag2(w_lora_f[l], w_lora_b[l]))
    p["a0"] = jnp.concatenate([a0_f[l], a0_b[l]])[None]
    p["al_hi"], p["al_lo"] = _hi_lo(_block_diag2(a_lora_f[l], a_lora_b[l]))
    p["gl_hi"], p["gl_lo"] = _hi_lo(g_lora[l])
    p["lnx_w"] = lnx_w[l][None]
    p["lnx_b"] = lnx_b[l][None]
    scale = HEAD_DIM ** -0.5 * float(np.log2(np.e))
    p["qk_gain"] = jnp.concatenate([jnp.tile(q_gain[l] * scale, ATT_Q_HEADS), jnp.tile(k_gain[l], ATT_KV_HEADS)])[None]
    hid = np.arange(640) // HEAD_DIM
    ones = (hid[:, None] == hid[None, :]).astype(np.float32)
    p["head_ones_qk"] = jnp.asarray(ones, BF16)
    p["head_ones"] = jnp.asarray(ones[:512, :512], BF16)
    p["wo_r"] = w_out[l][:RWKV_WIDTH].astype(BF16)
    p["wo_a"] = w_out[l][RWKV_WIDTH:].astype(BF16)
    p["norm2_g"] = norm2_g[l][None]
    p["ffn_gate"] = ffn_gate[l].astype(BF16)
    p["ffn_up"] = ffn_up[l].astype(BF16)
    p["ffn_down"] = ffn_down[l].astype(BF16)
    p["norm_f_g"] = norm_f_g[None]
    return p


def _tile(n, pref):
    t = pref
    while n % t:
        t //= 2
    return t


def _trunk(x, p):
    B, T, D = x.shape
    m = B * T
    x2 = x.reshape(m, D)
    z_r, z_a = _inproj(x2, p["norm1_g"], p["w_r"], p["w_a"], _tile(m, 512))
    r, v, kk, lw, kd, b, g, bg = _rwkv_prep(z_r, T, p, _tile(T, 256))
    y = _rwkv_scan(r, v, kk, lw, kd, b, B, T)
    cos_t, sin_t = _rope_tables(T)
    qt, k, vt = _att_prep(z_a, B, T, p, cos_t, sin_t)
    att = _attention(qt, k, vt, B, T, _tile(T, 512)).reshape(m, ATT_WIDTH)
    x1 = _mixout(x2, y, g, bg, att, p, _tile(m, 512))
    out = _ffn(x1, p, _tile(m, 512), 1408)
    return out.reshape(B, T, D)


def kernel(x_prompt, x_sample, norm1_g, w_in, mu_prev, mu_next, k_k, k_a, r_k, w0_f, w_lora_f, w0_b, w_lora_b, a0_f, a_lora_f, a0_b, a_lora_b, g_lora, lnx_w, lnx_b, q_gain, k_gain, w_out, norm2_g, ffn_gate, ffn_up, ffn_down, norm_f_g):
    p = _prepare_params(norm1_g, w_in, mu_prev, mu_next, k_k, k_a, r_k, w0_f, w_lora_f, w0_b, w_lora_b,
                        a0_f, a_lora_f, a0_b, a_lora_b, g_lora, lnx_w, lnx_b, q_gain, k_gain, w_out,
                        norm2_g, ffn_gate, ffn_up, ffn_down, norm_f_g)
    return (_trunk(x_prompt, p), _trunk(x_sample, p))
```

```python
import functools

import jax
import jax.numpy as jnp
import numpy as np
from jax import lax
from jax.experimental import pallas as pl
from jax.experimental.pallas import tpu as pltpu

F32 = jnp.float32
BF16 = jnp.bfloat16

D_MODEL = 1024
HEAD_DIM = 64
RWKV_WIDTH = 512
RWKV_HEADS = 8
ATT_WIDTH = 512
ATT_Q_HEADS = 8
ATT_KV_HEADS = 2
ATT_GROUP = 4
KV_WIDTH = 128
RWKV_COLS = 1920
ATT_COLS = 768
D_FF = 2816
GRID_W = 64
ROPE_THETA = 10000.0
ROPE_PAIRS = 16
NORM_EPS = 1e-6
LNX_EPS = 64e-5

LANES = 128
SUBLANES = 8
CHUNK = 128
Q_POS = 256
KV_UNROLL = 8
VT_ROWS = 80
DENOM_FLOOR = 2.0 ** -100
VMEM_LIMIT = 56 * 1024 * 1024


def _dot(a, b):
    return jnp.dot(a, b, preferred_element_type=F32)


def _bf(x):
    return x.astype(BF16)


def _split2(x):
    hi = _bf(x)
    lo = _bf(x - hi.astype(F32))
    return hi, lo


def _split3(x):
    hi = _bf(x)
    r1 = x - hi.astype(F32)
    mid = _bf(r1)
    lo = _bf(r1 - mid.astype(F32))
    return hi, mid, lo


def _dot_exact_rhs(x, m_bf):
    hi, mid, lo = _split3(x)
    return _dot(hi, m_bf) + _dot(mid, m_bf) + _dot(lo, m_bf)


def _dot3(a, b_hi, b_lo):
    a_hi, a_lo = _split2(a)
    return _dot(a_hi, b_hi) + _dot(a_lo, b_hi) + _dot(a_hi, b_lo)


def _sigmoid(x):
    return 1.0 / (1.0 + jnp.exp(-x))


def _softplus(x):
    return jnp.maximum(x, 0.0) + jnp.log(1.0 + jnp.exp(-jnp.abs(x)))


def _params(sem):
    return pltpu.CompilerParams(dimension_semantics=sem, vmem_limit_bytes=VMEM_LIMIT)


def _inproj_kernel(x_ref, g_ref, wr_ref, wa_ref, zr_ref, za_ref):
    x = x_ref[...]
    ms = jnp.mean(x * x, axis=-1, keepdims=True)
    h = _bf(x * lax.rsqrt(ms + NORM_EPS) * g_ref[...])
    zr_ref[...] = _dot(h, wr_ref[...])
    za_ref[...] = _dot(h, wa_ref[...])


def _inproj(x2, norm1_g, w_r, w_a, tm):
    m = x2.shape[0]
    return pl.pallas_call(
        _inproj_kernel,
        grid=(m // tm,),
        in_specs=[
            pl.BlockSpec((tm, D_MODEL), lambda i: (i, 0)),
            pl.BlockSpec((1, D_MODEL), lambda i: (0, 0)),
            pl.BlockSpec((D_MODEL, RWKV_COLS), lambda i: (0, 0)),
            pl.BlockSpec((D_MODEL, ATT_COLS), lambda i: (0, 0)),
        ],
        out_specs=[
            pl.BlockSpec((tm, RWKV_COLS), lambda i: (i, 0)),
            pl.BlockSpec((tm, ATT_COLS), lambda i: (i, 0)),
        ],
        out_shape=[
            jax.ShapeDtypeStruct((m, RWKV_COLS), F32),
            jax.ShapeDtypeStruct((m, ATT_COLS), F32),
        ],
        compiler_params=_params(("parallel",)),
        name="inproj",
    )(x2, norm1_g, w_r, w_a)


def _rwkv_prep_kernel(z_ref, zp_ref, zn_ref, mup_ref, mun_ref, kk_ref, ka_ref, rk_ref,
                      w0_ref, wlh_ref, wll_ref, a0_ref, alh_ref, all_ref, glh_ref, gll_ref,
                      hs_ref,
                      r_o, v_o, kk_o, lw_o, kd_o, b_o, g_o, bg_o, *, tm, blocks_per_seq):
    i = pl.program_id(0)
    pos = i % blocks_per_seq
    z = z_ref[...]
    prev_row = jnp.where(pos == 0, 0.0, zp_ref[SUBLANES - 1:SUBLANES, :])
    next_row = jnp.where(pos == blocks_per_seq - 1, 0.0, zn_ref[0:1, :])
    rows = lax.broadcasted_iota(jnp.int32, (tm, 1), 0)
    z_prev = jnp.where(rows == 0, prev_row, pltpu.roll(z, 1, 0))
    z_next = jnp.where(rows == tm - 1, next_row, pltpu.roll(z, tm - 1, 0))
    zf = z + mup_ref[...] * (z_prev - z) + mun_ref[...] * (z_next - z)

    r = zf[:, 0:512]
    k = zf[:, 512:1024]
    v = zf[:, 1024:1536]
    wd = zf[:, 1536:1664]
    ad = zf[:, 1664:1792]
    gd = zf[:, 1792:1920]
    hs = hs_ref[...]

    kk = k * kk_ref[...]
    ss = _dot_exact_rhs(kk * kk, hs)
    kk = kk * lax.rsqrt(jnp.maximum(ss, 1e-12))

    lw_both = w0_ref[...] + _dot3(jnp.tanh(wd), wlh_ref[...], wll_ref[...])
    as_both = a0_ref[...] + _dot3(ad, alh_ref[...], all_ref[...])
    ka = ka_ref[...]
    kb = jnp.zeros_like(k)
    for d in range(2):
        w_log = -_softplus(-lw_both[:, 512 * d:512 * (d + 1)]) - 0.5
        lw_o[d] = -jnp.exp(w_log)
        a = _sigmoid(as_both[:, 512 * d:512 * (d + 1)])
        kd = k * (1.0 + (a - 1.0) * ka)
        kd_o[d] = kd
        b_o[d] = kk * a
        kb = kb + kd
    kb = 0.5 * kb
    coef = _dot_exact_rhs(r * kb * rk_ref[...], hs)
    g = _dot3(_sigmoid(gd), glh_ref[...], gll_ref[...])
    r_o[...] = r
    v_o[...] = v
    kk_o[...] = kk
    g_o[...] = g
    bg_o[...] = coef * v * g


def _rwkv_prep(z_r, T, p, tm):
    m = z_r.shape[0]
    bps = T // tm
    hb = tm // SUBLANES
    nhalo = m // SUBLANES
    full = lambda shape: pl.BlockSpec(shape, lambda i: tuple(0 for _ in shape))
    tok = pl.BlockSpec((tm, RWKV_WIDTH), lambda i: (i, 0))
    tok2 = pl.BlockSpec((2, tm, RWKV_WIDTH), lambda i: (0, i, 0))
    kern = functools.partial(_rwkv_prep_kernel, tm=tm, blocks_per_seq=bps)
    return pl.pallas_call(
        kern,
        grid=(m // tm,),
        in_specs=[
            pl.BlockSpec((tm, RWKV_COLS), lambda i: (i, 0)),
            pl.BlockSpec((SUBLANES, RWKV_COLS), lambda i: (jnp.maximum(i * hb - 1, 0), 0)),
            pl.BlockSpec((SUBLANES, RWKV_COLS), lambda i: (jnp.minimum((i + 1) * hb, nhalo - 1), 0)),
            full((1, RWKV_COLS)), full((1, RWKV_COLS)),
            full((1, 512)), full((1, 512)), full((1, 512)),
            full((1, 1024)), full((128, 1024)), full((128, 1024)),
            full((1, 1024)), full((128, 1024)), full((128, 1024)),
            full((128, 512)), full((128, 512)),
            full((512, 512)),
        ],
        out_specs=[tok, tok, tok, tok2, tok2, tok2, tok, tok],
        out_shape=[
            jax.ShapeDtypeStruct((m, 512), F32),
            jax.ShapeDtypeStruct((m, 512), F32),
            jax.ShapeDtypeStruct((m, 512), F32),
            jax.ShapeDtypeStruct((2, m, 512), F32),
            jax.ShapeDtypeStruct((2, m, 512), F32),
            jax.ShapeDtypeStruct((2, m, 512), F32),
            jax.ShapeDtypeStruct((m, 512), F32),
            jax.ShapeDtypeStruct((m, 512), F32),
        ],
        compiler_params=_params(("parallel",)),
        name="rwkv_prep",
    )(z_r, z_r, z_r, p["mu_prev"], p["mu_next"], p["k_k"], p["k_a"], p["r_k"],
      p["w0"], p["wl_hi"], p["wl_lo"], p["a0"], p["al_hi"], p["al_lo"], p["gl_hi"], p["gl_lo"],
      p["head_ones"])


def _scan_kernel(r_ref, v_ref, kk_ref, lw_ref, kd_ref, b_ref, y_ref, z_scr):
    C = CHUNK
    d = pl.program_id(1)
    c = pl.program_id(2)

    @pl.when(c == 0)
    def _():
        z_scr[...] = jnp.zeros_like(z_scr)

    row = lax.broadcasted_iota(jnp.int32, (C, C), 0)
    col = lax.broadcasted_iota(jnp.int32, (C, C), 1)
    order = (row - col) * (1 - 2 * d)
    strict = order > 0
    incl = order >= 0
    eye = (row == col).astype(F32)
    tri = incl.astype(BF16)
    lane = lax.broadcasted_iota(jnp.int32, (1, LANES), 1)
    m0 = lane < HEAD_DIM
    m1 = lane >= HEAD_DIM
    blockdiag = (row < HEAD_DIM) == (col < HEAD_DIM)

    lw = lw_ref[0]
    hi, mid, lo = _split3(lw)
    cum = _dot(tri, hi) + _dot(tri, mid) + _dot(tri, lo)
    l_end = jnp.sum(lw, axis=0, keepdims=True)
    l_half = 0.5 * l_end
    r = r_ref[...]
    v = v_ref[...]
    kd = kd_ref[0]
    b = b_ref[0]
    a_sh = -kk_ref[...] * jnp.exp(cum - lw - l_half)
    r_sh = r * jnp.exp(cum - l_half)
    e_b = jnp.exp(l_half - cum)
    b_sh = b * e_b
    k_sh = kd * e_b
    e_end = jnp.exp(l_end - cum)
    b_hat = b * e_end
    k_hat = kd * e_end
    r_full = r * jnp.exp(cum)
    e_half = jnp.exp(l_half)
    p_end = jnp.exp(l_end)

    def both_heads(x):
        return jnp.concatenate([jnp.where(m0, x, 0.0), jnp.where(m1, x, 0.0)], axis=0)

    pairs = range(RWKV_HEADS // 2)
    heads = range(RWKV_HEADS)
    lanes = [slice(LANES * p, LANES * (p + 1)) for p in pairs]

    amat = []
    for p in pairs:
        sl = lanes[p]
        ar = jnp.concatenate([a_sh[:, sl], r_sh[:, sl]], axis=0)
        bk = _bf(jnp.concatenate([b_sh[:, sl], k_sh[:, sl]], axis=0))
        amat.append(lax.dot_general(_bf(both_heads(ar)), bk, (((1,), (1,)), ((), ())),
                                    preferred_element_type=F32))
    n_bf, a_ak, a_rb, a_rk, t_inv = [], [], [], [], []
    for hd in heads:
        ah = amat[hd // 2][2 * C * (hd % 2):2 * C * (hd % 2 + 1)]
        n = jnp.where(strict, ah[:C, :C], 0.0)
        n_bf.append(_bf(n))
        t_inv.append(eye + n)
        a_ak.append(jnp.where(strict, ah[:C, C:], 0.0))
        a_rb.append(jnp.where(incl, ah[C:, :C], 0.0))
        a_rk.append(jnp.where(incl, ah[C:, C:], 0.0))
    pw = [_bf(_dot(n_bf[hd], n_bf[hd])) for hd in heads]
    avy = []
    for p in pairs:
        akrk = jnp.concatenate([jnp.concatenate(a_ak[2 * p:2 * p + 2], axis=1),
                                jnp.concatenate(a_rk[2 * p:2 * p + 2], axis=1)], axis=0)
        avy.append(_dot(_bf(akrk), _bf(both_heads(v[:, lanes[p]]))))
    for _ in range(5):
        both = [_dot(pw[hd], jnp.concatenate([pw[hd], _bf(t_inv[hd])], axis=1)) for hd in heads]
        pw = [_bf(both[hd][:, :C]) for hd in heads]
        t_inv = [t_inv[hd] + both[hd][:, C:] for hd in heads]
    last = [_dot(pw[hd], _bf(t_inv[hd])) for hd in heads]
    t_inv = [t_inv[hd] + last[hd] for hd in heads]

    aw = []
    for p in pairs:
        a_p = a_sh[:, lanes[p]]
        av = avy[p][:C]
        rhs = jnp.concatenate([
            jnp.concatenate([jnp.where(m0, a_p, 0.0), jnp.where(m0, av, 0.0)], axis=1),
            jnp.concatenate([jnp.where(m1, a_p, 0.0), jnp.where(m1, av, 0.0)], axis=1)], axis=0)
        aw.append(_dot(_bf(jnp.concatenate(t_inv[2 * p:2 * p + 2], axis=1)), _bf(rhs)))

    zs = [z_scr[p] for p in pairs]
    xs = []
    for p in pairs:
        a_bar = aw[p][:, :LANES] * e_half[:, lanes[p]]
        xs.append(_dot(_bf(jnp.concatenate([a_bar, r_full[:, lanes[p]]], axis=0)), _bf(zs[p])))
    us = [xs[p][:C] + aw[p][:, LANES:] for p in pairs]
    for p in pairs:
        y = xs[p][C:] + avy[p][C:] + _dot(_bf(jnp.concatenate(a_rb[2 * p:2 * p + 2], axis=1)),
                                           _bf(both_heads(us[p])))
        y_ref[0, :, lanes[p]] = y
    for p in pairs:
        sl = lanes[p]
        bkt = jnp.concatenate([b_hat[:, sl].T, k_hat[:, sl].T], axis=1)
        uv = jnp.concatenate([us[p], v[:, sl]], axis=0)
        pend_col = jnp.broadcast_to(p_end[:, sl], (LANES, LANES)).T
        z_new = zs[p] * pend_col + _dot(_bf(bkt), _bf(uv))
        z_scr[p] = jnp.where(blockdiag, z_new, 0.0)


def _rwkv_scan(r, v, kk, lw, kd, b, B, T):
    m = r.shape[0]
    nc = T // CHUNK

    def blk(bi, d, c):
        return bi * nc + c + d * (nc - 1 - 2 * c)

    tok = pl.BlockSpec((CHUNK, RWKV_WIDTH), lambda bi, d, c: (blk(bi, d, c), 0))
    tok2 = pl.BlockSpec((1, CHUNK, RWKV_WIDTH), lambda bi, d, c: (d, blk(bi, d, c), 0))
    return pl.pallas_call(
        _scan_kernel,
        grid=(B, 2, nc),
        in_specs=[tok, tok, tok, tok2, tok2, tok2],
        out_specs=tok2,
        out_shape=jax.ShapeDtypeStruct((2, m, RWKV_WIDTH), F32),
        scratch_shapes=[pltpu.VMEM((RWKV_HEADS // 2, LANES, LANES), F32)],
        compiler_params=_params(("parallel", "parallel", "arbitrary")),
        name="rwkv_scan",
    )(r, v, kk, lw, kd, b)


def _att_prep_kernel(z_ref, gain_ref, cos_ref, sin_ref, hs_ref, qt_o, k_o, vt_o, qn_o, kn_o):
    i = pl.program_id(1)
    z = z_ref[...]
    qk = z[:, :640]
    ss = _dot_exact_rhs(qk * qk, hs_ref[...])
    qk = qk * lax.rsqrt(ss * (1.0 / HEAD_DIM) + NORM_EPS) * gain_ref[...]
    width = qk.shape[1]
    lane = lax.broadcasted_iota(jnp.int32, (1, width), 1)
    first = (lane % (2 * ROPE_PAIRS)) < ROPE_PAIRS
    partner = jnp.where(first, pltpu.roll(qk, width - ROPE_PAIRS, 1), pltpu.roll(qk, ROPE_PAIRS, 1))
    cos = jnp.concatenate([cos_ref[...]] * 5, axis=1)
    sin = jnp.concatenate([sin_ref[...]] * 5, axis=1)
    qk = qk * cos + partner * sin
    for j in range(4):
        st = _bf(qk[:, LANES * j:LANES * (j + 1)].T)
        h = j // 2
        g = (2 * j) % ATT_GROUP
        qt_o[0, h, :, Q_POS * g:Q_POS * (g + 1)] = st[:HEAD_DIM]
        qt_o[0, h, :, Q_POS * (g + 1):Q_POS * (g + 2)] = st[HEAD_DIM:]
        sq = st.astype(F32)
        sq = sq * sq
        for e in range(2):
            nrm = jnp.sum(sq[HEAD_DIM * e:HEAD_DIM * (e + 1)], axis=0, keepdims=True)
            qn_o[0, h, :, Q_POS * (g + e):Q_POS * (g + e + 1)] = jnp.broadcast_to(nrm, (SUBLANES, Q_POS))
    kb = _bf(qk[:, 512:640])
    k_o[0] = kb
    kf = kb.astype(F32)
    kn = jnp.max(_dot_exact_rhs(kf * kf, hs_ref[512:640, 512:640]), axis=0, keepdims=True)
    kn = jnp.broadcast_to(kn, (SUBLANES, KV_WIDTH))

    @pl.when(i == 0)
    def _():
        kn_o[0] = kn

    @pl.when(i > 0)
    def _():
        kn_o[0] = jnp.maximum(kn_o[0], kn)

    vt = _bf(z[:, 640:768].T)
    ones = jnp.ones((VT_ROWS - HEAD_DIM, Q_POS), BF16)
    for h in range(ATT_KV_HEADS):
        vt_o[0, h, :HEAD_DIM, :] = vt[HEAD_DIM * h:HEAD_DIM * (h + 1)]
        vt_o[0, h, HEAD_DIM:, :] = ones


def _att_prep(z_a, B, T, p, cos_t, sin_t):
    nb = T // Q_POS
    full = lambda shape: pl.BlockSpec(shape, lambda bi, i: tuple(0 for _ in shape))
    return pl.pallas_call(
        _att_prep_kernel,
        grid=(B, nb),
        in_specs=[
            pl.BlockSpec((Q_POS, ATT_COLS), lambda bi, i: (bi * nb + i, 0)),
            full((1, 640)),
            pl.BlockSpec((Q_POS, LANES), lambda bi, i: (i, 0)),
            pl.BlockSpec((Q_POS, LANES), lambda bi, i: (i, 0)),
            full((640, 640)),
        ],
        out_specs=[
            pl.BlockSpec((1, ATT_KV_HEADS, HEAD_DIM, ATT_GROUP * Q_POS), lambda bi, i: (bi, 0, 0, i)),
            pl.BlockSpec((1, Q_POS, KV_WIDTH), lambda bi, i: (bi, i, 0)),
            pl.BlockSpec((1, ATT_KV_HEADS, VT_ROWS, Q_POS), lambda bi, i: (bi, 0, 0, i)),
            pl.BlockSpec((1, ATT_KV_HEADS, SUBLANES, ATT_GROUP * Q_POS), lambda bi, i: (bi, 0, 0, i)),
            pl.BlockSpec((1, SUBLANES, KV_WIDTH), lambda bi, i: (bi, 0, 0)),
        ],
        out_shape=[
            jax.ShapeDtypeStruct((B, ATT_KV_HEADS, HEAD_DIM, ATT_GROUP * T), BF16),
            jax.ShapeDtypeStruct((B, T, KV_WIDTH), BF16),
            jax.ShapeDtypeStruct((B, ATT_KV_HEADS, VT_ROWS, T), BF16),
            jax.ShapeDtypeStruct((B, ATT_KV_HEADS, SUBLANES, ATT_GROUP * T), F32),
            jax.ShapeDtypeStruct((B, SUBLANES, KV_WIDTH), F32),
        ],
        compiler_params=_params(("parallel", "arbitrary")),
        name="att_prep",
    )(z_a, p["qk_gain"], cos_t, sin_t, p["head_ones_qk"])


def _attn_kernel(qt_ref, qn_ref, kn_ref, k_ref, vt_ref, o_ref, *, n_kv, tkv, unroll):
    h = pl.program_id(1)
    ncol = ATT_GROUP * Q_POS
    qt = qt_ref[0, 0]
    rowh = lax.broadcasted_iota(jnp.int32, (KV_WIDTH, 1), 0) // HEAD_DIM
    q2 = jnp.where(rowh == h, jnp.concatenate([qt, qt], axis=0), jnp.zeros((), BF16))
    laneh = lax.broadcasted_iota(jnp.int32, (1, KV_WIDTH), 1) // HEAD_DIM
    kn = jnp.max(jnp.where(laneh == h, kn_ref[0, 0:1, :], 0.0), axis=1, keepdims=True)
    shift = jnp.sqrt(qn_ref[0, 0, 0:1, :] * kn)

    def chunk_start(j):
        return j * tkv if isinstance(j, int) else pl.multiple_of(j * tkv, tkv)

    def scores(j):
        kc = k_ref[0, pl.ds(chunk_start(j), tkv), :]
        return _dot(kc, q2)

    def pv(j, pt):
        vc = vt_ref[0, 0, :, pl.ds(chunk_start(j), tkv)]
        return _dot(vc, pt)

    def sweep(step, carry):
        def group(base, carry, s, final):
            for u in range(unroll):
                s_next = None if (final and u == unroll - 1) else scores(base + u + 1)
                carry = step(base + u, carry, s)
                s = s_next
            return carry, s

        s = scores(0)
        n_groups = n_kv // unroll
        if n_groups > 1:
            carry, s = lax.fori_loop(0, n_groups - 1, lambda j, c: group(j * unroll, *c, False), (carry, s))
        carry, _ = group((n_groups - 1) * unroll, carry, s, True)
        return carry

    def emit(acc):
        o = acc[:HEAD_DIM] / acc[HEAD_DIM:HEAD_DIM + 1]
        ot = jnp.concatenate([o, jnp.zeros_like(o)], axis=0).T
        for g in range(ATT_GROUP):
            o_ref[0, :, HEAD_DIM * g:HEAD_DIM * (g + 1)] = _bf(ot[Q_POS * g:Q_POS * (g + 1), :HEAD_DIM])

    def fast_group(base, acc):
        for u in range(unroll):
            acc = acc + pv(base + u, _bf(jnp.exp2(scores(base + u) - shift)))
        return acc

    acc0 = jnp.zeros((VT_ROWS, ncol), F32)
    n_groups = n_kv // unroll
    if n_groups > 1:
        acc = lax.fori_loop(0, n_groups, lambda j, acc: fast_group(j * unroll, acc), acc0)
    else:
        acc = fast_group(0, acc0)
    emit(acc)
    denom_ok = jnp.min(acc[HEAD_DIM:HEAD_DIM + 1]) >= DENOM_FLOOR

    @pl.when(jnp.logical_not(denom_ok))
    def _():
        def step(j, carry, s):
            m, acc = carry
            m_new = jnp.maximum(m, jnp.max(s, axis=0, keepdims=True))
            return m_new, acc * jnp.exp2(m - m_new) + pv(j, _bf(jnp.exp2(s - m_new)))

        _, acc_online = sweep(step, (jnp.full((1, ncol), -jnp.inf, F32), acc0))
        emit(acc_online)


def _attention(qt, qn, kn, k, vt, B, T, tkv):
    nb = T // Q_POS
    n_kv = T // tkv
    kern = functools.partial(_attn_kernel, n_kv=n_kv, tkv=tkv, unroll=_tile(n_kv, KV_UNROLL))
    return pl.pallas_call(
        kern,
        grid=(B, ATT_KV_HEADS, nb),
        in_specs=[
            pl.BlockSpec((1, 1, HEAD_DIM, ATT_GROUP * Q_POS), lambda bi, h, i: (bi, h, 0, i)),
            pl.BlockSpec((1, 1, SUBLANES, ATT_GROUP * Q_POS), lambda bi, h, i: (bi, h, 0, i)),
            pl.BlockSpec((1, SUBLANES, KV_WIDTH), lambda bi, h, i: (bi, 0, 0)),
            pl.BlockSpec((1, T, KV_WIDTH), lambda bi, h, i: (bi, 0, 0)),
            pl.BlockSpec((1, 1, VT_ROWS, T), lambda bi, h, i: (bi, h, 0, 0)),
        ],
        out_specs=pl.BlockSpec((1, Q_POS, ATT_GROUP * HEAD_DIM), lambda bi, h, i: (bi, i, h)),
        out_shape=jax.ShapeDtypeStruct((B, T, ATT_WIDTH), BF16),
        compiler_params=_params(("parallel", "parallel", "arbitrary")),
        name="attention",
    )(qt, qn, kn, k, vt)


def _mixout_kernel(x_ref, y_ref, g_ref, bg_ref, att_ref, lnw_ref, lnb_ref, hs_ref, wo_r_ref, wo_a_ref, o_ref):
    y = y_ref[0] + y_ref[1]
    hs = hs_ref[...]
    mu = _dot_exact_rhs(y, hs) * (1.0 / HEAD_DIM)
    dy = y - mu
    var = _dot_exact_rhs(dy * dy, hs) * (1.0 / HEAD_DIM)
    yn = dy * lax.rsqrt(var + LNX_EPS) * lnw_ref[...] + lnb_ref[...]
    yr = yn * g_ref[...] + bg_ref[...]
    o_ref[...] = x_ref[...] + _dot(_bf(yr), wo_r_ref[...]) + _dot(att_ref[...], wo_a_ref[...])


def _mixout(x2, y, g, bg, att, p, tm):
    m = x2.shape[0]
    full = lambda shape: pl.BlockSpec(shape, lambda i: tuple(0 for _ in shape))
    tok = pl.BlockSpec((tm, 512), lambda i: (i, 0))
    return pl.pallas_call(
        _mixout_kernel,
        grid=(m // tm,),
        in_specs=[
            pl.BlockSpec((tm, D_MODEL), lambda i: (i, 0)),
            pl.BlockSpec((2, tm, 512), lambda i: (0, i, 0)),
            tok, tok, tok,
            full((1, 512)), full((1, 512)), full((512, 512)),
            full((512, D_MODEL)), full((512, D_MODEL)),
        ],
        out_specs=pl.BlockSpec((tm, D_MODEL), lambda i: (i, 0)),
        out_shape=jax.ShapeDtypeStruct((m, D_MODEL), F32),
        compiler_params=_params(("parallel",)),
        name="mixout",
    )(x2, y, g, bg, att, p["lnx_w"], p["lnx_b"], p["head_ones"], p["wo_r"], p["wo_a"])


def _ffn_kernel(x_ref, g2_ref, wg_ref, wu_ref, wd_ref, gf_ref, o_ref, h_scr, acc_scr):
    j = pl.program_id(1)

    @pl.when(j == 0)
    def _():
        x = x_ref[...]
        ms = jnp.mean(x * x, axis=-1, keepdims=True)
        h_scr[...] = _bf(x * lax.rsqrt(ms + NORM_EPS) * g2_ref[...])
        acc_scr[...] = x

    h = h_scr[...]
    gate = _dot(h, wg_ref[...])
    up = _dot(h, wu_ref[...])
    act = gate * _sigmoid(gate) * up
    acc_scr[...] += _dot(_bf(act), wd_ref[...])

    @pl.when(j == pl.num_programs(1) - 1)
    def _():
        xo = acc_scr[...]
        ms = jnp.mean(xo * xo, axis=-1, keepdims=True)
        o_ref[...] = xo * lax.rsqrt(ms + NORM_EPS) * gf_ref[...]


def _ffn(x1, p, tm, tf):
    m = x1.shape[0]
    return pl.pallas_call(
        _ffn_kernel,
        grid=(m // tm, D_FF // tf),
        in_specs=[
            pl.BlockSpec((tm, D_MODEL), lambda i, j: (i, 0)),
            pl.BlockSpec((1, D_MODEL), lambda i, j: (0, 0)),
            pl.BlockSpec((D_MODEL, tf), lambda i, j: (0, j)),
            pl.BlockSpec((D_MODEL, tf), lambda i, j: (0, j)),
            pl.BlockSpec((tf, D_MODEL), lambda i, j: (j, 0)),
            pl.BlockSpec((1, D_MODEL), lambda i, j: (0, 0)),
        ],
        out_specs=pl.BlockSpec((tm, D_MODEL), lambda i, j: (i, 0)),
        out_shape=jax.ShapeDtypeStruct((m, D_MODEL), F32),
        scratch_shapes=[pltpu.VMEM((tm, D_MODEL), BF16), pltpu.VMEM((tm, D_MODEL), F32)],
        compiler_params=_params(("parallel", "arbitrary")),
        name="ffn",
    )(x1, p["norm2_g"], p["ffn_gate"], p["ffn_up"], p["ffn_down"], p["norm_f_g"])


def _rope_tables(T):
    n_rows = T // GRID_W
    t = jnp.arange(T, dtype=jnp.int32)
    row = (t // GRID_W).astype(F32)
    col = (t % GRID_W).astype(F32)
    inv = ROPE_THETA ** (-jnp.arange(ROPE_PAIRS, dtype=F32) / ROPE_PAIRS)
    ar = row[:, None] * inv
    ac = col[:, None] * inv
    cos = jnp.concatenate([jnp.cos(ar), jnp.cos(ar), jnp.cos(ac), jnp.cos(ac)], axis=1)
    sin = jnp.concatenate([-jnp.sin(ar), jnp.sin(ar), -jnp.sin(ac), jnp.sin(ac)], axis=1)
    del n_rows
    return jnp.tile(cos, (1, 2)), jnp.tile(sin, (1, 2))


def _block_diag2(a, b):
    za = jnp.zeros_like(a)
    return jnp.concatenate([jnp.concatenate([a, za], axis=1), jnp.concatenate([za, b], axis=1)], axis=0)


def _hi_lo(w):
    hi = w.astype(BF16)
    return hi, (w - hi.astype(F32)).astype(BF16)


def _prepare_params(norm1_g, w_in, mu_prev, mu_next, k_k, k_a, r_k, w0_f, w_lora_f, w0_b, w_lora_b,
                    a0_f, a_lora_f, a0_b, a_lora_b, g_lora, lnx_w, lnx_b, q_gain, k_gain, w_out,
                    norm2_g, ffn_gate, ffn_up, ffn_down, norm_f_g):
    l = 0
    p = {}
    p["norm1_g"] = norm1_g[l][None]
    p["w_r"] = w_in[l][:, :RWKV_COLS].astype(BF16)
    p["w_a"] = w_in[l][:, RWKV_COLS:].astype(BF16)
    p["mu_prev"] = mu_prev[l][None]
    p["mu_next"] = mu_next[l][None]
    p["k_k"] = k_k[l][None]
    p["k_a"] = k_a[l][None]
    p["r_k"] = r_k[l].reshape(1, RWKV_WIDTH)
    p["w0"] = jnp.concatenate([w0_f[l], w0_b[l]])[None]
    p["wl_hi"], p["wl_lo"] = _hi_lo(_block_diag2(w_lora_f[l], w_lora_b[l]))
    p["a0"] = jnp.concatenate([a0_f[l], a0_b[l]])[None]
    p["al_hi"], p["al_lo"] = _hi_lo(_block_diag2(a_lora_f[l], a_lora_b[l]))
    p["gl_hi"], p["gl_lo"] = _hi_lo(g_lora[l])
    p["lnx_w"] = lnx_w[l][None]
    p["lnx_b"] = lnx_b[l][None]
    scale = HEAD_DIM ** -0.5 * float(np.log2(np.e))
    p["qk_gain"] = jnp.concatenate([jnp.tile(q_gain[l] * scale, ATT_Q_HEADS), jnp.tile(k_gain[l], ATT_KV_HEADS)])[None]
    hid = np.arange(640) // HEAD_DIM
    ones = (hid[:, None] == hid[None, :]).astype(np.float32)
    p["head_ones_qk"] = jnp.asarray(ones, BF16)
    p["head_ones"] = jnp.asarray(ones[:512, :512], BF16)
    p["wo_r"] = w_out[l][:RWKV_WIDTH].astype(BF16)
    p["wo_a"] = w_out[l][RWKV_WIDTH:].astype(BF16)
    p["norm2_g"] = norm2_g[l][None]
    p["ffn_gate"] = ffn_gate[l].astype(BF16)
    p["ffn_up"] = ffn_up[l].astype(BF16)
    p["ffn_down"] = ffn_down[l].astype(BF16)
    p["norm_f_g"] = norm_f_g[None]
    return p


def _tile(n, pref):
    t = pref
    while n % t:
        t //= 2
    return t


def _trunk(x, p):
    B, T, D = x.shape
    m = B * T
    x2 = x.reshape(m, D)
    z_r, z_a = _inproj(x2, p["norm1_g"], p["w_r"], p["w_a"], _tile(m, 512))
    r, v, kk, lw, kd, b, g, bg = _rwkv_prep(z_r, T, p, _tile(T, 256))
    y = _rwkv_scan(r, v, kk, lw, kd, b, B, T)
    cos_t, sin_t = _rope_tables(T)
    qt, k, vt, qn, kn = _att_prep(z_a, B, T, p, cos_t, sin_t)
    att = _attention(qt, qn, kn, k, vt, B, T, _tile(T, 512)).reshape(m, ATT_WIDTH)
    x1 = _mixout(x2, y, g, bg, att, p, _tile(m, 512))
    out = _ffn(x1, p, _tile(m, 512), 1408)
    return out.reshape(B, T, D)


def kernel(x_prompt, x_sample, norm1_g, w_in, mu_prev, mu_next, k_k, k_a, r_k, w0_f, w_lora_f, w0_b, w_lora_b, a0_f, a_lora_f, a0_b, a_lora_b, g_lora, lnx_w, lnx_b, q_gain, k_gain, w_out, norm2_g, ffn_gate, ffn_up, ffn_down, norm_f_g):
    p = _prepare_params(norm1_g, w_in, mu_prev, mu_next, k_k, k_a, r_k, w0_f, w_lora_f, w0_b, w_lora_b,
                        a0_f, a_lora_f, a0_b, a_lora_b, g_lora, lnx_w, lnx_b, q_gain, k_gain, w_out,
                        norm2_g, ffn_gate, ffn_up, ffn_down, norm_f_g)
    return (_trunk(x_prompt, p), _trunk(x_sample, p))
```

```python
import functools

import jax
import jax.numpy as jnp
import numpy as np
from jax import lax
from jax.experimental import pallas as pl
from jax.experimental.pallas import tpu as pltpu

F32 = jnp.float32
BF16 = jnp.bfloat16

D_MODEL = 1024
HEAD_DIM = 64
RWKV_WIDTH = 512
RWKV_HEADS = 8
ATT_WIDTH = 512
ATT_Q_HEADS = 8
ATT_KV_HEADS = 2
ATT_GROUP = 4
KV_WIDTH = 128
RWKV_COLS = 1920
ATT_COLS = 768
D_FF = 2816
GRID_W = 64
ROPE_THETA = 10000.0
ROPE_PAIRS = 16
NORM_EPS = 1e-6
LNX_EPS = 64e-5

LANES = 128
SUBLANES = 8
CHUNK = 128
Q_POS = 256
KV_UNROLL = 8
VT_ROWS = 80
DECAY_SCALE = float(np.exp(-0.5))
DENOM_FLOOR = 2.0 ** -100
VMEM_LIMIT = 56 * 1024 * 1024


def _dot(a, b):
    return jnp.dot(a, b, preferred_element_type=F32)


def _bf(x):
    return x.astype(BF16)


def _split2(x):
    hi = _bf(x)
    lo = _bf(x - hi.astype(F32))
    return hi, lo


def _split3(x):
    hi = _bf(x)
    r1 = x - hi.astype(F32)
    mid = _bf(r1)
    lo = _bf(r1 - mid.astype(F32))
    return hi, mid, lo


def _dot_exact_rhs(x, m_bf):
    hi, lo = _split2(x)
    return _dot(hi, m_bf) + _dot(lo, m_bf)


def _dot3(a, b_hi, b_lo):
    a_hi, a_lo = _split2(a)
    return _dot(a_hi, b_hi) + _dot(a_lo, b_hi) + _dot(a_hi, b_lo)


def _sigmoid(x):
    return 0.5 * jnp.tanh(0.5 * x) + 0.5


def _softplus(x):
    return jnp.maximum(x, 0.0) + jnp.log(1.0 + jnp.exp(-jnp.abs(x)))


def _params(sem):
    return pltpu.CompilerParams(dimension_semantics=sem, vmem_limit_bytes=VMEM_LIMIT)


def _inproj_kernel(x_ref, g_ref, wr_ref, wa_ref, zr_ref, za_ref):
    x = x_ref[...]
    ms = jnp.mean(x * x, axis=-1, keepdims=True)
    h = _bf(x * lax.rsqrt(ms + NORM_EPS) * g_ref[...])
    zr_ref[...] = _dot(h, wr_ref[...])
    za_ref[...] = _dot(h, wa_ref[...])


def _inproj(x2, norm1_g, w_r, w_a, tm):
    m = x2.shape[0]
    return pl.pallas_call(
        _inproj_kernel,
        grid=(m // tm,),
        in_specs=[
            pl.BlockSpec((tm, D_MODEL), lambda i: (i, 0)),
            pl.BlockSpec((1, D_MODEL), lambda i: (0, 0)),
            pl.BlockSpec((D_MODEL, RWKV_COLS), lambda i: (0, 0)),
            pl.BlockSpec((D_MODEL, ATT_COLS), lambda i: (0, 0)),
        ],
        out_specs=[
            pl.BlockSpec((tm, RWKV_COLS), lambda i: (i, 0)),
            pl.BlockSpec((tm, ATT_COLS), lambda i: (i, 0)),
        ],
        out_shape=[
            jax.ShapeDtypeStruct((m, RWKV_COLS), F32),
            jax.ShapeDtypeStruct((m, ATT_COLS), F32),
        ],
        compiler_params=_params(("parallel",)),
        name="inproj",
    )(x2, norm1_g, w_r, w_a)


def _rwkv_prep_kernel(z_ref, zp_ref, zn_ref, mup_ref, mun_ref, kk_ref, ka_ref, rk_ref,
                      w0_ref, wlh_ref, wll_ref, a0_ref, alh_ref, all_ref, glh_ref, gll_ref,
                      hs_ref,
                      v_o, ash_o, rsh_o, bsh_o, ksh_o, rfull_o, bkt_o, lend_o, g_o, bg_o,
                      *, tm, blocks_per_seq):
    i = pl.program_id(0)
    pos = i % blocks_per_seq
    z = z_ref[...]
    prev_row = jnp.where(pos == 0, 0.0, zp_ref[SUBLANES - 1:SUBLANES, :])
    next_row = jnp.where(pos == blocks_per_seq - 1, 0.0, zn_ref[0:1, :])
    rows = lax.broadcasted_iota(jnp.int32, (SUBLANES, 1), 0)
    z_prev = pltpu.roll(z, 1, 0)
    z_prev = jnp.concatenate([jnp.where(rows == 0, prev_row, z_prev[:SUBLANES]), z_prev[SUBLANES:]], axis=0)
    z_next = pltpu.roll(z, tm - 1, 0)
    z_next = jnp.concatenate([z_next[:tm - SUBLANES],
                              jnp.where(rows == SUBLANES - 1, next_row, z_next[tm - SUBLANES:])], axis=0)
    zf = z + mup_ref[...] * (z_prev - z) + mun_ref[...] * (z_next - z)

    r = zf[:, 0:512]
    k = zf[:, 512:1024]
    v = zf[:, 1024:1536]
    wd = zf[:, 1536:1664]
    ad = zf[:, 1664:1792]
    gd = zf[:, 1792:1920]
    hs = hs_ref[...]

    kk = k * kk_ref[...]
    ss = _dot_exact_rhs(kk * kk, hs)
    kk = kk * lax.rsqrt(jnp.maximum(ss, 1e-12))

    lw_both = w0_ref[...] + _dot3(jnp.tanh(wd), wlh_ref[...], wll_ref[...])
    as_both = a0_ref[...] + _dot3(ad, alh_ref[...], all_ref[...])
    ka = ka_ref[...]
    kb = jnp.zeros_like(k)
    ri = lax.broadcasted_iota(jnp.int32, (tm, tm), 0)
    ci = lax.broadcasted_iota(jnp.int32, (tm, tm), 1)
    same_chunk = (ri // CHUNK) == (ci // CHUNK)
    chunk_ones = same_chunk.astype(BF16)
    for d in range(2):
        lw = -DECAY_SCALE * _sigmoid(lw_both[:, 512 * d:512 * (d + 1)])
        a = _sigmoid(as_both[:, 512 * d:512 * (d + 1)])
        kd = k * (1.0 + (a - 1.0) * ka)
        b = kk * a
        kb = kb + kd
        processed = (ci <= ri) if d == 0 else (ci >= ri)
        tri = (same_chunk & processed).astype(BF16)
        hi, mid, lo = _split3(lw)
        cum = _dot(tri, hi) + _dot(tri, mid) + _dot(tri, lo)
        l_end = _dot(chunk_ones, hi) + _dot(chunk_ones, mid) + _dot(chunk_ones, lo)
        l_half = 0.5 * l_end
        e_half = jnp.exp(l_half)
        r_sh = r * jnp.exp(cum - l_half)
        ash_o[d] = _bf(-kk * jnp.exp(cum - lw - l_half))
        rsh_o[d] = _bf(r_sh)
        rfull_o[d] = _bf(r_sh * e_half)
        e_b = jnp.exp(l_half - cum)
        b_sh = b * e_b
        k_sh = kd * e_b
        bsh_o[d] = _bf(b_sh)
        ksh_o[d] = _bf(k_sh)
        b_hat = b_sh * e_half
        k_hat = k_sh * e_half
        for cc in range(tm // CHUNK):
            rs = slice(CHUNK * cc, CHUNK * (cc + 1))
            lend_o[d, cc] = l_end[CHUNK * cc:CHUNK * cc + SUBLANES]
            for p in range(RWKV_HEADS // 2):
                ls = slice(LANES * p, LANES * (p + 1))
                bkt_o[d, cc, ls, 0:CHUNK] = _bf(b_hat[rs, ls].T)
                bkt_o[d, cc, ls, CHUNK:2 * CHUNK] = _bf(k_hat[rs, ls].T)
    kb = 0.5 * kb
    coef = _dot_exact_rhs(r * kb * rk_ref[...], hs)
    g = _dot3(_sigmoid(gd), glh_ref[...], gll_ref[...])
    v_o[...] = _bf(v)
    g_o[...] = g
    bg_o[...] = coef * v * g


def _rwkv_prep(z_r, T, p, tm):
    m = z_r.shape[0]
    bps = T // tm
    hb = tm // SUBLANES
    nhalo = m // SUBLANES
    full = lambda shape: pl.BlockSpec(shape, lambda i: tuple(0 for _ in shape))
    tok = pl.BlockSpec((tm, RWKV_WIDTH), lambda i: (i, 0))
    tok2 = pl.BlockSpec((2, tm, RWKV_WIDTH), lambda i: (0, i, 0))
    cpb = tm // CHUNK
    nchunk = m // CHUNK
    tok2_shape = jax.ShapeDtypeStruct((2, m, RWKV_WIDTH), BF16)
    kern = functools.partial(_rwkv_prep_kernel, tm=tm, blocks_per_seq=bps)
    return pl.pallas_call(
        kern,
        grid=(m // tm,),
        in_specs=[
            pl.BlockSpec((tm, RWKV_COLS), lambda i: (i, 0)),
            pl.BlockSpec((SUBLANES, RWKV_COLS), lambda i: (jnp.maximum(i * hb - 1, 0), 0)),
            pl.BlockSpec((SUBLANES, RWKV_COLS), lambda i: (jnp.minimum((i + 1) * hb, nhalo - 1), 0)),
            full((1, RWKV_COLS)), full((1, RWKV_COLS)),
            full((1, 512)), full((1, 512)), full((1, 512)),
            full((1, 1024)), full((128, 1024)), full((128, 1024)),
            full((1, 1024)), full((128, 1024)), full((128, 1024)),
            full((128, 512)), full((128, 512)),
            full((512, 512)),
        ],
        out_specs=[
            tok, tok2, tok2, tok2, tok2, tok2,
            pl.BlockSpec((2, cpb, RWKV_WIDTH, 2 * CHUNK), lambda i: (0, i, 0, 0)),
            pl.BlockSpec((2, cpb, SUBLANES, RWKV_WIDTH), lambda i: (0, i, 0, 0)),
            tok, tok,
        ],
        out_shape=[
            jax.ShapeDtypeStruct((m, RWKV_WIDTH), BF16),
            tok2_shape, tok2_shape, tok2_shape, tok2_shape, tok2_shape,
            jax.ShapeDtypeStruct((2, nchunk, RWKV_WIDTH, 2 * CHUNK), BF16),
            jax.ShapeDtypeStruct((2, nchunk, SUBLANES, RWKV_WIDTH), F32),
            jax.ShapeDtypeStruct((m, RWKV_WIDTH), F32),
            jax.ShapeDtypeStruct((m, RWKV_WIDTH), F32),
        ],
        compiler_params=_params(("parallel",)),
        name="rwkv_prep",
    )(z_r, z_r, z_r, p["mu_prev"], p["mu_next"], p["k_k"], p["k_a"], p["r_k"],
      p["w0"], p["wl_hi"], p["wl_lo"], p["a0"], p["al_hi"], p["al_lo"], p["gl_hi"], p["gl_lo"],
      p["head_ones"])


def _scan_kernel(v_ref, ash_ref, rsh_ref, bsh_ref, ksh_ref, rfull_ref, bkt_ref, lend_ref, y_ref, z_scr):
    C = CHUNK
    d = pl.program_id(1)
    c = pl.program_id(2)

    @pl.when(c == 0)
    def _():
        z_scr[...] = jnp.zeros_like(z_scr)

    row = lax.broadcasted_iota(jnp.int32, (C, C), 0)
    col = lax.broadcasted_iota(jnp.int32, (C, C), 1)
    order = (row - col) * (1 - 2 * d)
    strict = order > 0
    incl = order >= 0
    eye = (row == col).astype(F32)
    lane = lax.broadcasted_iota(jnp.int32, (1, LANES), 1)
    m0 = lane < HEAD_DIM
    m1 = lane >= HEAD_DIM
    blockdiag = (row < HEAD_DIM) == (col < HEAD_DIM)

    v = v_ref[...]
    a_sh = ash_ref[0]
    r_sh = rsh_ref[0]
    b_sh = bsh_ref[0]
    k_sh = ksh_ref[0]
    r_full = rfull_ref[0]
    l_end = lend_ref[0, 0, 0:1, :]
    e_half = jnp.exp(0.5 * l_end)
    p_end = jnp.exp(l_end)

    def both_heads(x):
        zero = jnp.zeros((), x.dtype)
        return jnp.concatenate([jnp.where(m0, x, zero), jnp.where(m1, x, zero)], axis=0)

    pairs = range(RWKV_HEADS // 2)
    heads = range(RWKV_HEADS)
    lanes = [slice(LANES * p, LANES * (p + 1)) for p in pairs]

    amat = []
    for p in pairs:
        sl = lanes[p]
        ar = jnp.concatenate([a_sh[:, sl], r_sh[:, sl]], axis=0)
        bk = jnp.concatenate([b_sh[:, sl], k_sh[:, sl]], axis=0)
        amat.append(lax.dot_general(both_heads(ar), bk, (((1,), (1,)), ((), ())),
                                    preferred_element_type=F32))
    n_bf, a_ak, a_rb, a_rk, t_inv = [], [], [], [], []
    for hd in heads:
        ah = amat[hd // 2][2 * C * (hd % 2):2 * C * (hd % 2 + 1)]
        n = jnp.where(strict, ah[:C, :C], 0.0)
        n_bf.append(_bf(n))
        t_inv.append(eye + n)
        a_ak.append(jnp.where(strict, ah[:C, C:], 0.0))
        a_rb.append(jnp.where(incl, ah[C:, :C], 0.0))
        a_rk.append(jnp.where(incl, ah[C:, C:], 0.0))
    pw = [_bf(_dot(n_bf[hd], n_bf[hd])) for hd in heads]
    avy = []
    for p in pairs:
        akrk = jnp.concatenate([jnp.concatenate(a_ak[2 * p:2 * p + 2], axis=1),
                                jnp.concatenate(a_rk[2 * p:2 * p + 2], axis=1)], axis=0)
        avy.append(_dot(_bf(akrk), both_heads(v[:, lanes[p]])))
    for _ in range(5):
        both = [_dot(pw[hd], jnp.concatenate([pw[hd], _bf(t_inv[hd])], axis=1)) for hd in heads]
        pw = [_bf(both[hd][:, :C]) for hd in heads]
        t_inv = [t_inv[hd] + both[hd][:, C:] for hd in heads]
    last = [_dot(pw[hd], _bf(t_inv[hd])) for hd in heads]
    t_inv = [t_inv[hd] + last[hd] for hd in heads]

    aw = []
    for p in pairs:
        rhs = jnp.concatenate([both_heads(a_sh[:, lanes[p]]), both_heads(_bf(avy[p][:C]))], axis=1)
        aw.append(_dot(_bf(jnp.concatenate(t_inv[2 * p:2 * p + 2], axis=1)), rhs))

    zs = [z_scr[p] for p in pairs]
    xs = []
    for p in pairs:
        a_bar = _bf(aw[p][:, :LANES] * e_half[:, lanes[p]])
        xs.append(_dot(jnp.concatenate([a_bar, r_full[:, lanes[p]]], axis=0), _bf(zs[p])))
    us = [xs[p][:C] + aw[p][:, LANES:] for p in pairs]
    for p in pairs:
        y = xs[p][C:] + avy[p][C:] + _dot(_bf(jnp.concatenate(a_rb[2 * p:2 * p + 2], axis=1)),
                                           _bf(both_heads(us[p])))
        y_ref[0, :, lanes[p]] = y
    for p in pairs:
        sl = lanes[p]
        bkt = bkt_ref[0, 0, sl, :]
        uv = jnp.concatenate([_bf(us[p]), v[:, sl]], axis=0)
        pend_col = jnp.broadcast_to(p_end[:, sl], (LANES, LANES)).T
        z_new = zs[p] * pend_col + _dot(bkt, uv)
        z_scr[p] = jnp.where(blockdiag, z_new, 0.0)


def _rwkv_scan(v, ash, rsh, bsh, ksh, rfull, bkt, lend, B, T):
    m = v.shape[0]
    nc = T // CHUNK

    def blk(bi, d, c):
        return bi * nc + c + d * (nc - 1 - 2 * c)

    tok = pl.BlockSpec((CHUNK, RWKV_WIDTH), lambda bi, d, c: (blk(bi, d, c), 0))
    tok2 = pl.BlockSpec((1, CHUNK, RWKV_WIDTH), lambda bi, d, c: (d, blk(bi, d, c), 0))
    return pl.pallas_call(
        _scan_kernel,
        grid=(B, 2, nc),
        in_specs=[
            tok, tok2, tok2, tok2, tok2, tok2,
            pl.BlockSpec((1, 1, RWKV_WIDTH, 2 * CHUNK), lambda bi, d, c: (d, blk(bi, d, c), 0, 0)),
            pl.BlockSpec((1, 1, SUBLANES, RWKV_WIDTH), lambda bi, d, c: (d, blk(bi, d, c), 0, 0)),
        ],
        out_specs=tok2,
        out_shape=jax.ShapeDtypeStruct((2, m, RWKV_WIDTH), F32),
        scratch_shapes=[pltpu.VMEM((RWKV_HEADS // 2, LANES, LANES), F32)],
        compiler_params=_params(("parallel", "parallel", "arbitrary")),
        name="rwkv_scan",
    )(v, ash, rsh, bsh, ksh, rfull, bkt, lend)


def _att_prep_kernel(z_ref, gain_ref, cos_ref, sin_ref, hs_ref, qt_o, k_o, vt_o, qn_o, kn_o):
    i = pl.program_id(1)
    z = z_ref[...]
    qk = z[:, :640]
    ss = _dot_exact_rhs(qk * qk, hs_ref[...])
    qk = qk * lax.rsqrt(ss * (1.0 / HEAD_DIM) + NORM_EPS) * gain_ref[...]
    width = qk.shape[1]
    lane = lax.broadcasted_iota(jnp.int32, (1, width), 1)
    first = (lane % (2 * ROPE_PAIRS)) < ROPE_PAIRS
    partner = jnp.where(first, pltpu.roll(qk, width - ROPE_PAIRS, 1), pltpu.roll(qk, ROPE_PAIRS, 1))
    cos = jnp.concatenate([cos_ref[...]] * 5, axis=1)
    sin = jnp.concatenate([sin_ref[...]] * 5, axis=1)
    qk = qk * cos + partner * sin
    for j in range(4):
        st = _bf(qk[:, LANES * j:LANES * (j + 1)].T)
        h = j // 2
        g = (2 * j) % ATT_GROUP
        qt_o[0, h, :, Q_POS * g:Q_POS * (g + 1)] = st[:HEAD_DIM]
        qt_o[0, h, :, Q_POS * (g + 1):Q_POS * (g + 2)] = st[HEAD_DIM:]
        sq = st.astype(F32)
        sq = sq * sq
        for e in range(2):
            nrm = jnp.sum(sq[HEAD_DIM * e:HEAD_DIM * (e + 1)], axis=0, keepdims=True)
            qn_o[0, h, :, Q_POS * (g + e):Q_POS * (g + e + 1)] = jnp.broadcast_to(nrm, (SUBLANES, Q_POS))
    kb = _bf(qk[:, 512:640])
    k_o[0] = kb
    kf = kb.astype(F32)
    kn = jnp.max(_dot_exact_rhs(kf * kf, hs_ref[512:640, 512:640]), axis=0, keepdims=True)
    kn = jnp.broadcast_to(kn, (SUBLANES, KV_WIDTH))

    @pl.when(i == 0)
    def _():
        kn_o[0] = kn

    @pl.when(i > 0)
    def _():
        kn_o[0] = jnp.maximum(kn_o[0], kn)

    vt = _bf(z[:, 640:768].T)
    ones = jnp.ones((VT_ROWS - HEAD_DIM, Q_POS), BF16)
    for h in range(ATT_KV_HEADS):
        vt_o[0, h, :HEAD_DIM, :] = vt[HEAD_DIM * h:HEAD_DIM * (h + 1)]
        vt_o[0, h, HEAD_DIM:, :] = ones


def _att_prep(z_a, B, T, p, cos_t, sin_t):
    nb = T // Q_POS
    full = lambda shape: pl.BlockSpec(shape, lambda bi, i: tuple(0 for _ in shape))
    return pl.pallas_call(
        _att_prep_kernel,
        grid=(B, nb),
        in_specs=[
            pl.BlockSpec((Q_POS, ATT_COLS), lambda bi, i: (bi * nb + i, 0)),
            full((1, 640)),
            pl.BlockSpec((Q_POS, LANES), lambda bi, i: (i, 0)),
            pl.BlockSpec((Q_POS, LANES), lambda bi, i: (i, 0)),
            full((640, 640)),
        ],
        out_specs=[
            pl.BlockSpec((1, ATT_KV_HEADS, HEAD_DIM, ATT_GROUP * Q_POS), lambda bi, i: (bi, 0, 0, i)),
            pl.BlockSpec((1, Q_POS, KV_WIDTH), lambda bi, i: (bi, i, 0)),
            pl.BlockSpec((1, ATT_KV_HEADS, VT_ROWS, Q_POS), lambda bi, i: (bi, 0, 0, i)),
            pl.BlockSpec((1, ATT_KV_HEADS, SUBLANES, ATT_GROUP * Q_POS), lambda bi, i: (bi, 0, 0, i)),
            pl.BlockSpec((1, SUBLANES, KV_WIDTH), lambda bi, i: (bi, 0, 0)),
        ],
        out_shape=[
            jax.ShapeDtypeStruct((B, ATT_KV_HEADS, HEAD_DIM, ATT_GROUP * T), BF16),
            jax.ShapeDtypeStruct((B, T, KV_WIDTH), BF16),
            jax.ShapeDtypeStruct((B, ATT_KV_HEADS, VT_ROWS, T), BF16),
            jax.ShapeDtypeStruct((B, ATT_KV_HEADS, SUBLANES, ATT_GROUP * T), F32),
            jax.ShapeDtypeStruct((B, SUBLANES, KV_WIDTH), F32),
        ],
        compiler_params=_params(("parallel", "arbitrary")),
        name="att_prep",
    )(z_a, p["qk_gain"], cos_t, sin_t, p["head_ones_qk"])


def _attn_kernel(qt_ref, qn_ref, kn_ref, k_ref, vt_ref, o_ref, *, n_kv, tkv, unroll):
    h = pl.program_id(1)
    ncol = ATT_GROUP * Q_POS
    qt = qt_ref[0, 0]
    rowh = lax.broadcasted_iota(jnp.int32, (KV_WIDTH, 1), 0) // HEAD_DIM
    q2 = jnp.where(rowh == h, jnp.concatenate([qt, qt], axis=0), jnp.zeros((), BF16))
    laneh = lax.broadcasted_iota(jnp.int32, (1, KV_WIDTH), 1) // HEAD_DIM
    kn = jnp.max(jnp.where(laneh == h, kn_ref[0, 0:1, :], 0.0), axis=1, keepdims=True)
    shift = jnp.sqrt(qn_ref[0, 0, 0:1, :] * kn)

    def chunk_start(j):
        return j * tkv if isinstance(j, int) else pl.multiple_of(j * tkv, tkv)

    def scores(j):
        kc = k_ref[0, pl.ds(chunk_start(j), tkv), :]
        return _dot(kc, q2)

    def pv(j, pt):
        vc = vt_ref[0, 0, :, pl.ds(chunk_start(j), tkv)]
        return _dot(vc, pt)

    def sweep(step, carry):
        def group(base, carry, s, final):
            for u in range(unroll):
                s_next = None if (final and u == unroll - 1) else scores(base + u + 1)
                carry = step(base + u, carry, s)
                s = s_next
            return carry, s

        s = scores(0)
        n_groups = n_kv // unroll
        if n_groups > 1:
            carry, s = lax.fori_loop(0, n_groups - 1, lambda j, c: group(j * unroll, *c, False), (carry, s))
        carry, _ = group((n_groups - 1) * unroll, carry, s, True)
        return carry

    def emit(acc):
        o = acc[:HEAD_DIM] / acc[HEAD_DIM:HEAD_DIM + 1]
        ot = jnp.concatenate([o, jnp.zeros_like(o)], axis=0).T
        for g in range(ATT_GROUP):
            o_ref[0, :, HEAD_DIM * g:HEAD_DIM * (g + 1)] = _bf(ot[Q_POS * g:Q_POS * (g + 1), :HEAD_DIM])

    def fast_group(base, acc):
        for u in range(unroll):
            acc = acc + pv(base + u, _bf(jnp.exp2(scores(base + u) - shift)))
        return acc

    acc0 = jnp.zeros((VT_ROWS, ncol), F32)
    n_groups = n_kv // unroll
    if n_groups > 1:
        acc = lax.fori_loop(0, n_groups, lambda j, acc: fast_group(j * unroll, acc), acc0)
    else:
        acc = fast_group(0, acc0)
    emit(acc)
    denom_ok = jnp.min(acc[HEAD_DIM:HEAD_DIM + 1]) >= DENOM_FLOOR

    @pl.when(jnp.logical_not(denom_ok))
    def _():
        def step(j, carry, s):
            m, acc = carry
            m_new = jnp.maximum(m, jnp.max(s, axis=0, keepdims=True))
            return m_new, acc * jnp.exp2(m - m_new) + pv(j, _bf(jnp.exp2(s - m_new)))

        _, acc_online = sweep(step, (jnp.full((1, ncol), -jnp.inf, F32), acc0))
        emit(acc_online)


def _attention(qt, qn, kn, k, vt, B, T, tkv):
    nb = T // Q_POS
    n_kv = T // tkv
    kern = functools.partial(_attn_kernel, n_kv=n_kv, tkv=tkv, unroll=_tile(n_kv, KV_UNROLL))
    return pl.pallas_call(
        kern,
        grid=(B, ATT_KV_HEADS, nb),
        in_specs=[
            pl.BlockSpec((1, 1, HEAD_DIM, ATT_GROUP * Q_POS), lambda bi, h, i: (bi, h, 0, i)),
            pl.BlockSpec((1, 1, SUBLANES, ATT_GROUP * Q_POS), lambda bi, h, i: (bi, h, 0, i)),
            pl.BlockSpec((1, SUBLANES, KV_WIDTH), lambda bi, h, i: (bi, 0, 0)),
            pl.BlockSpec((1, T, KV_WIDTH), lambda bi, h, i: (bi, 0, 0)),
            pl.BlockSpec((1, 1, VT_ROWS, T), lambda bi, h, i: (bi, h, 0, 0)),
        ],
        out_specs=pl.BlockSpec((1, Q_POS, ATT_GROUP * HEAD_DIM), lambda bi, h, i: (bi, i, h)),
        out_shape=jax.ShapeDtypeStruct((B, T, ATT_WIDTH), BF16),
        compiler_params=_params(("parallel", "parallel", "arbitrary")),
        name="attention",
    )(qt, qn, kn, k, vt)


def _mixout_kernel(x_ref, y_ref, g_ref, bg_ref, att_ref, lnw_ref, lnb_ref, hs_ref, wo_r_ref, wo_a_ref, o_ref):
    y = y_ref[0] + y_ref[1]
    hs = hs_ref[...]
    mu = _dot_exact_rhs(y, hs) * (1.0 / HEAD_DIM)
    dy = y - mu
    var = _dot_exact_rhs(dy * dy, hs) * (1.0 / HEAD_DIM)
    yn = dy * lax.rsqrt(var + LNX_EPS) * lnw_ref[...] + lnb_ref[...]
    yr = yn * g_ref[...] + bg_ref[...]
    o_ref[...] = x_ref[...] + _dot(_bf(yr), wo_r_ref[...]) + _dot(att_ref[...], wo_a_ref[...])


def _mixout(x2, y, g, bg, att, p, tm):
    m = x2.shape[0]
    full = lambda shape: pl.BlockSpec(shape, lambda i: tuple(0 for _ in shape))
    tok = pl.BlockSpec((tm, 512), lambda i: (i, 0))
    return pl.pallas_call(
        _mixout_kernel,
        grid=(m // tm,),
        in_specs=[
            pl.BlockSpec((tm, D_MODEL), lambda i: (i, 0)),
            pl.BlockSpec((2, tm, 512), lambda i: (0, i, 0)),
            tok, tok, tok,
            full((1, 512)), full((1, 512)), full((512, 512)),
            full((512, D_MODEL)), full((512, D_MODEL)),
        ],
        out_specs=pl.BlockSpec((tm, D_MODEL), lambda i: (i, 0)),
        out_shape=jax.ShapeDtypeStruct((m, D_MODEL), F32),
        compiler_params=_params(("parallel",)),
        name="mixout",
    )(x2, y, g, bg, att, p["lnx_w"], p["lnx_b"], p["head_ones"], p["wo_r"], p["wo_a"])


def _ffn_kernel(x_ref, g2_ref, wg_ref, wu_ref, wd_ref, gf_ref, o_ref, h_scr, acc_scr):
    j = pl.program_id(1)

    @pl.when(j == 0)
    def _():
        x = x_ref[...]
        ms = jnp.mean(x * x, axis=-1, keepdims=True)
        h_scr[...] = _bf(x * lax.rsqrt(ms + NORM_EPS) * g2_ref[...])
        acc_scr[...] = x

    h = h_scr[...]
    gate = _dot(h, wg_ref[...])
    up = _dot(h, wu_ref[...])
    act = gate * _sigmoid(gate) * up
    acc_scr[...] += _dot(_bf(act), wd_ref[...])

    @pl.when(j == pl.num_programs(1) - 1)
    def _():
        xo = acc_scr[...]
        ms = jnp.mean(xo * xo, axis=-1, keepdims=True)
        o_ref[...] = xo * lax.rsqrt(ms + NORM_EPS) * gf_ref[...]


def _ffn(x1, p, tm, tf):
    m = x1.shape[0]
    return pl.pallas_call(
        _ffn_kernel,
        grid=(m // tm, D_FF // tf),
        in_specs=[
            pl.BlockSpec((tm, D_MODEL), lambda i, j: (i, 0)),
            pl.BlockSpec((1, D_MODEL), lambda i, j: (0, 0)),
            pl.BlockSpec((D_MODEL, tf), lambda i, j: (0, j)),
            pl.BlockSpec((D_MODEL, tf), lambda i, j: (0, j)),
            pl.BlockSpec((tf, D_MODEL), lambda i, j: (j, 0)),
            pl.BlockSpec((1, D_MODEL), lambda i, j: (0, 0)),
        ],
        out_specs=pl.BlockSpec((tm, D_MODEL), lambda i, j: (i, 0)),
        out_shape=jax.ShapeDtypeStruct((m, D_MODEL), F32),
        scratch_shapes=[pltpu.VMEM((tm, D_MODEL), BF16), pltpu.VMEM((tm, D_MODEL), F32)],
        compiler_params=_params(("parallel", "arbitrary")),
        name="ffn",
    )(x1, p["norm2_g"], p["ffn_gate"], p["ffn_up"], p["ffn_down"], p["norm_f_g"])


def _rope_tables(T):
    n_rows = T // GRID_W
    t = jnp.arange(T, dtype=jnp.int32)
    row = (t // GRID_W).astype(F32)
    col = (t % GRID_W).astype(F32)
    inv = ROPE_THETA ** (-jnp.arange(ROPE_PAIRS, dtype=F32) / ROPE_PAIRS)
    ar = row[:, None] * inv
    ac = col[:, None] * inv
    cos = jnp.concatenate([jnp.cos(ar), jnp.cos(ar), jnp.cos(ac), jnp.cos(ac)], axis=1)
    sin = jnp.concatenate([-jnp.sin(ar), jnp.sin(ar), -jnp.sin(ac), jnp.sin(ac)], axis=1)
    del n_rows
    return jnp.tile(cos, (1, 2)), jnp.tile(sin, (1, 2))


def _block_diag2(a, b):
    za = jnp.zeros_like(a)
    return jnp.concatenate([jnp.concatenate([a, za], axis=1), jnp.concatenate([za, b], axis=1)], axis=0)


def _hi_lo(w):
    hi = w.astype(BF16)
    return hi, (w - hi.astype(F32)).astype(BF16)


def _prepare_params(norm1_g, w_in, mu_prev, mu_next, k_k, k_a, r_k, w0_f, w_lora_f, w0_b, w_lora_b,
                    a0_f, a_lora_f, a0_b, a_lora_b, g_lora, lnx_w, lnx_b, q_gain, k_gain, w_out,
                    norm2_g, ffn_gate, ffn_up, ffn_down, norm_f_g):
    l = 0
    p = {}
    p["norm1_g"] = norm1_g[l][None]
    p["w_r"] = w_in[l][:, :RWKV_COLS].astype(BF16)
    p["w_a"] = w_in[l][:, RWKV_COLS:].astype(BF16)
    p["mu_prev"] = mu_prev[l][None]
    p["mu_next"] = mu_next[l][None]
    p["k_k"] = k_k[l][None]
    p["k_a"] = k_a[l][None]
    p["r_k"] = r_k[l].reshape(1, RWKV_WIDTH)
    p["w0"] = jnp.concatenate([w0_f[l], w0_b[l]])[None]
    p["wl_hi"], p["wl_lo"] = _hi_lo(_block_diag2(w_lora_f[l], w_lora_b[l]))
    p["a0"] = jnp.concatenate([a0_f[l], a0_b[l]])[None]
    p["al_hi"], p["al_lo"] = _hi_lo(_block_diag2(a_lora_f[l], a_lora_b[l]))
    p["gl_hi"], p["gl_lo"] = _hi_lo(g_lora[l])
    p["lnx_w"] = lnx_w[l][None]
    p["lnx_b"] = lnx_b[l][None]
    scale = HEAD_DIM ** -0.5 * float(np.log2(np.e))
    p["qk_gain"] = jnp.concatenate([jnp.tile(q_gain[l] * scale, ATT_Q_HEADS), jnp.tile(k_gain[l], ATT_KV_HEADS)])[None]
    hid = np.arange(640) // HEAD_DIM
    ones = (hid[:, None] == hid[None, :]).astype(np.float32)
    p["head_ones_qk"] = jnp.asarray(ones, BF16)
    p["head_ones"] = jnp.asarray(ones[:512, :512], BF16)
    p["wo_r"] = w_out[l][:RWKV_WIDTH].astype(BF16)
    p["wo_a"] = w_out[l][RWKV_WIDTH:].astype(BF16)
    p["norm2_g"] = norm2_g[l][None]
    p["ffn_gate"] = ffn_gate[l].astype(BF16)
    p["ffn_up"] = ffn_up[l].astype(BF16)
    p["ffn_down"] = ffn_down[l].astype(BF16)
    p["norm_f_g"] = norm_f_g[None]
    return p


def _tile(n, pref):
    t = pref
    while n % t:
        t //= 2
    return t


def _trunk(x, p):
    B, T, D = x.shape
    m = B * T
    x2 = x.reshape(m, D)
    z_r, z_a = _inproj(x2, p["norm1_g"], p["w_r"], p["w_a"], _tile(m, 512))
    v, ash, rsh, bsh, ksh, rfull, bkt, lend, g, bg = _rwkv_prep(z_r, T, p, _tile(T, 256))
    y = _rwkv_scan(v, ash, rsh, bsh, ksh, rfull, bkt, lend, B, T)
    cos_t, sin_t = _rope_tables(T)
    qt, k, vt, qn, kn = _att_prep(z_a, B, T, p, cos_t, sin_t)
    att = _attention(qt, qn, kn, k, vt, B, T, _tile(T, 512)).reshape(m, ATT_WIDTH)
    x1 = _mixout(x2, y, g, bg, att, p, _tile(m, 512))
    out = _ffn(x1, p, _tile(m, 512), 1408)
    return out.reshape(B, T, D)


def kernel(x_prompt, x_sample, norm1_g, w_in, mu_prev, mu_next, k_k, k_a, r_k, w0_f, w_lora_f, w0_b, w_lora_b, a0_f, a_lora_f, a0_b, a_lora_b, g_lora, lnx_w, lnx_b, q_gain, k_gain, w_out, norm2_g, ffn_gate, ffn_up, ffn_down, norm_f_g):
    p = _prepare_params(norm1_g, w_in, mu_prev, mu_next, k_k, k_a, r_k, w0_f, w_lora_f, w0_b, w_lora_b,
                        a0_f, a_lora_f, a0_b, a_lora_b, g_lora, lnx_w, lnx_b, q_gain, k_gain, w_out,
                        norm2_g, ffn_gate, ffn_up, ffn_down, norm_f_g)
    return (_trunk(x_prompt, p), _trunk(x_sample, p))
```

```python
import functools

import jax
import jax.numpy as jnp
import numpy as np
from jax import lax
from jax.experimental import pallas as pl
from jax.experimental.pallas import tpu as pltpu

F32 = jnp.float32
BF16 = jnp.bfloat16

D_MODEL = 1024
HEAD_DIM = 64
RWKV_WIDTH = 512
RWKV_HEADS = 8
ATT_WIDTH = 512
ATT_Q_HEADS = 8
ATT_KV_HEADS = 2
ATT_GROUP = 4
KV_WIDTH = 128
RWKV_COLS = 1920
ATT_COLS = 768
D_FF = 2816
GRID_W = 64
ROPE_THETA = 10000.0
ROPE_PAIRS = 16
NORM_EPS = 1e-6
LNX_EPS = 64e-5

LANES = 128
SUBLANES = 8
CHUNK = 128
Q_POS = 256
KV_UNROLL = 8
VT_ROWS = 80
DECAY_SCALE = float(np.exp(-0.5))
DENOM_FLOOR = 2.0 ** -100
VMEM_LIMIT = 56 * 1024 * 1024


def _dot(a, b):
    return jnp.dot(a, b, preferred_element_type=F32)


def _bf(x):
    return x.astype(BF16)


def _split2(x):
    hi = _bf(x)
    lo = _bf(x - hi.astype(F32))
    return hi, lo


def _split3(x):
    hi = _bf(x)
    r1 = x - hi.astype(F32)
    mid = _bf(r1)
    lo = _bf(r1 - mid.astype(F32))
    return hi, mid, lo


def _dot_exact_rhs(x, m_bf):
    hi, lo = _split2(x)
    return _dot(hi, m_bf) + _dot(lo, m_bf)


def _dot3(a, b_hi, b_lo):
    a_hi, a_lo = _split2(a)
    return _dot(a_hi, b_hi) + _dot(a_lo, b_hi) + _dot(a_hi, b_lo)


def _sigmoid(x):
    return 0.5 * jnp.tanh(0.5 * x) + 0.5


def _softplus(x):
    return jnp.maximum(x, 0.0) + jnp.log(1.0 + jnp.exp(-jnp.abs(x)))


def _params(sem):
    return pltpu.CompilerParams(dimension_semantics=sem, vmem_limit_bytes=VMEM_LIMIT)


def _inproj_kernel(x_ref, g_ref, wr_ref, wa_ref, zr_ref, za_ref):
    x = x_ref[...]
    ms = jnp.mean(x * x, axis=-1, keepdims=True)
    h = _bf(x * lax.rsqrt(ms + NORM_EPS) * g_ref[...])
    zr_ref[...] = _dot(h, wr_ref[...])
    za_ref[...] = _dot(h, wa_ref[...])


def _inproj(x2, norm1_g, w_r, w_a, tm):
    m = x2.shape[0]
    return pl.pallas_call(
        _inproj_kernel,
        grid=(m // tm,),
        in_specs=[
            pl.BlockSpec((tm, D_MODEL), lambda i: (i, 0)),
            pl.BlockSpec((1, D_MODEL), lambda i: (0, 0)),
            pl.BlockSpec((D_MODEL, RWKV_COLS), lambda i: (0, 0)),
            pl.BlockSpec((D_MODEL, ATT_COLS), lambda i: (0, 0)),
        ],
        out_specs=[
            pl.BlockSpec((tm, RWKV_COLS), lambda i: (i, 0)),
            pl.BlockSpec((tm, ATT_COLS), lambda i: (i, 0)),
        ],
        out_shape=[
            jax.ShapeDtypeStruct((m, RWKV_COLS), F32),
            jax.ShapeDtypeStruct((m, ATT_COLS), F32),
        ],
        compiler_params=_params(("parallel",)),
        name="inproj",
    )(x2, norm1_g, w_r, w_a)


def _rwkv_prep_kernel(z_ref, zp_ref, zn_ref, mup_ref, mun_ref, kk_ref, ka_ref, rk_ref,
                      w0_ref, wlh_ref, wll_ref, a0_ref, alh_ref, all_ref, glh_ref, gll_ref,
                      hs_ref,
                      v_o, ash_o, rsh_o, bsh_o, ksh_o, rfull_o, bkt_o, lend_o, g_o, bg_o,
                      *, tm, blocks_per_seq):
    i = pl.program_id(0)
    pos = i % blocks_per_seq
    z = z_ref[...]
    prev_row = jnp.where(pos == 0, 0.0, zp_ref[SUBLANES - 1:SUBLANES, :])
    next_row = jnp.where(pos == blocks_per_seq - 1, 0.0, zn_ref[0:1, :])
    rows = lax.broadcasted_iota(jnp.int32, (SUBLANES, 1), 0)
    z_prev = pltpu.roll(z, 1, 0)
    z_prev = jnp.concatenate([jnp.where(rows == 0, prev_row, z_prev[:SUBLANES]), z_prev[SUBLANES:]], axis=0)
    z_next = pltpu.roll(z, tm - 1, 0)
    z_next = jnp.concatenate([z_next[:tm - SUBLANES],
                              jnp.where(rows == SUBLANES - 1, next_row, z_next[tm - SUBLANES:])], axis=0)
    zf = z + mup_ref[...] * (z_prev - z) + mun_ref[...] * (z_next - z)

    r = zf[:, 0:512]
    k = zf[:, 512:1024]
    v = zf[:, 1024:1536]
    wd = zf[:, 1536:1664]
    ad = zf[:, 1664:1792]
    gd = zf[:, 1792:1920]
    hs = hs_ref[...]

    kk = k * kk_ref[...]
    ss = _dot_exact_rhs(kk * kk, hs)
    kk = kk * lax.rsqrt(jnp.maximum(ss, 1e-12))

    lw_both = w0_ref[...] + _dot3(jnp.tanh(wd), wlh_ref[...], wll_ref[...])
    as_both = a0_ref[...] + _dot3(ad, alh_ref[...], all_ref[...])
    ka = ka_ref[...]
    kb = jnp.zeros_like(k)
    ri = lax.broadcasted_iota(jnp.int32, (tm, tm), 0)
    ci = lax.broadcasted_iota(jnp.int32, (tm, tm), 1)
    same_chunk = (ri // CHUNK) == (ci // CHUNK)
    chunk_ones = same_chunk.astype(BF16)
    for d in range(2):
        lw = -DECAY_SCALE * _sigmoid(lw_both[:, 512 * d:512 * (d + 1)])
        a = _sigmoid(as_both[:, 512 * d:512 * (d + 1)])
        kd = k * (1.0 + (a - 1.0) * ka)
        b = kk * a
        kb = kb + kd
        processed = (ci <= ri) if d == 0 else (ci >= ri)
        tri = (same_chunk & processed).astype(BF16)
        hi, mid, lo = _split3(lw)
        cum = _dot(tri, hi) + _dot(tri, mid) + _dot(tri, lo)
        l_end = _dot(chunk_ones, hi) + _dot(chunk_ones, mid) + _dot(chunk_ones, lo)
        l_half = 0.5 * l_end
        e_half = jnp.exp(l_half)
        r_sh = r * jnp.exp(cum - l_half)
        ash_o[d] = _bf(-kk * jnp.exp(cum - lw - l_half))
        rsh_o[d] = _bf(r_sh)
        rfull_o[d] = _bf(r_sh * e_half)
        e_b = jnp.exp(l_half - cum)
        b_sh = b * e_b
        k_sh = kd * e_b
        bsh_o[d] = _bf(b_sh)
        ksh_o[d] = _bf(k_sh)
        b_hat = b_sh * e_half
        k_hat = k_sh * e_half
        for cc in range(tm // CHUNK):
            rs = slice(CHUNK * cc, CHUNK * (cc + 1))
            lend_o[d, cc] = l_end[CHUNK * cc:CHUNK * cc + SUBLANES]
            for p in range(RWKV_HEADS // 2):
                ls = slice(LANES * p, LANES * (p + 1))
                bkt_o[d, cc, ls, 0:CHUNK] = _bf(b_hat[rs, ls].T)
                bkt_o[d, cc, ls, CHUNK:2 * CHUNK] = _bf(k_hat[rs, ls].T)
    kb = 0.5 * kb
    coef = _dot_exact_rhs(r * kb * rk_ref[...], hs)
    g = _dot3(_sigmoid(gd), glh_ref[...], gll_ref[...])
    v_o[...] = _bf(v)
    g_o[...] = g
    bg_o[...] = coef * v * g


def _rwkv_prep(z_r, T, p, tm):
    m = z_r.shape[0]
    bps = T // tm
    hb = tm // SUBLANES
    nhalo = m // SUBLANES
    full = lambda shape: pl.BlockSpec(shape, lambda i: tuple(0 for _ in shape))
    tok = pl.BlockSpec((tm, RWKV_WIDTH), lambda i: (i, 0))
    tok2 = pl.BlockSpec((2, tm, RWKV_WIDTH), lambda i: (0, i, 0))
    cpb = tm // CHUNK
    nchunk = m // CHUNK
    tok2_shape = jax.ShapeDtypeStruct((2, m, RWKV_WIDTH), BF16)
    kern = functools.partial(_rwkv_prep_kernel, tm=tm, blocks_per_seq=bps)
    return pl.pallas_call(
        kern,
        grid=(m // tm,),
        in_specs=[
            pl.BlockSpec((tm, RWKV_COLS), lambda i: (i, 0)),
            pl.BlockSpec((SUBLANES, RWKV_COLS), lambda i: (jnp.maximum(i * hb - 1, 0), 0)),
            pl.BlockSpec((SUBLANES, RWKV_COLS), lambda i: (jnp.minimum((i + 1) * hb, nhalo - 1), 0)),
            full((1, RWKV_COLS)), full((1, RWKV_COLS)),
            full((1, 512)), full((1, 512)), full((1, 512)),
            full((1, 1024)), full((128, 1024)), full((128, 1024)),
            full((1, 1024)), full((128, 1024)), full((128, 1024)),
            full((128, 512)), full((128, 512)),
            full((512, 512)),
        ],
        out_specs=[
            tok, tok2, tok2, tok2, tok2, tok2,
            pl.BlockSpec((2, cpb, RWKV_WIDTH, 2 * CHUNK), lambda i: (0, i, 0, 0)),
            pl.BlockSpec((2, cpb, SUBLANES, RWKV_WIDTH), lambda i: (0, i, 0, 0)),
            tok, tok,
        ],
        out_shape=[
            jax.ShapeDtypeStruct((m, RWKV_WIDTH), BF16),
            tok2_shape, tok2_shape, tok2_shape, tok2_shape, tok2_shape,
            jax.ShapeDtypeStruct((2, nchunk, RWKV_WIDTH, 2 * CHUNK), BF16),
            jax.ShapeDtypeStruct((2, nchunk, SUBLANES, RWKV_WIDTH), F32),
            jax.ShapeDtypeStruct((m, RWKV_WIDTH), F32),
            jax.ShapeDtypeStruct((m, RWKV_WIDTH), F32),
        ],
        compiler_params=_params(("parallel",)),
        name="rwkv_prep",
    )(z_r, z_r, z_r, p["mu_prev"], p["mu_next"], p["k_k"], p["k_a"], p["r_k"],
      p["w0"], p["wl_hi"], p["wl_lo"], p["a0"], p["al_hi"], p["al_lo"], p["gl_hi"], p["gl_lo"],
      p["head_ones"])


def _scan_kernel(*refs):
    C = CHUNK
    c = pl.program_id(1)
    z_scr = refs[-1]

    @pl.when(c == 0)
    def _():
        z_scr[...] = jnp.zeros_like(z_scr)

    row = lax.broadcasted_iota(jnp.int32, (C, C), 0)
    col = lax.broadcasted_iota(jnp.int32, (C, C), 1)
    strict_d = [col < row, col > row]
    incl_d = [col <= row, col >= row]
    eye = (row == col).astype(F32)
    lane = lax.broadcasted_iota(jnp.int32, (1, LANES), 1)
    m0 = lane < HEAD_DIM
    m1 = lane >= HEAD_DIM
    blockdiag = (row < HEAD_DIM) == (col < HEAD_DIM)

    def both_heads(x):
        zero = jnp.zeros((), x.dtype)
        return jnp.concatenate([jnp.where(m0, x, zero), jnp.where(m1, x, zero)], axis=0)

    ppd = RWKV_HEADS // 2
    pairs = range(2 * ppd)
    heads = range(2 * RWKV_HEADS)
    n_in = (len(refs) - 3) // 2
    v, a_sh, r_sh, b_sh, k_sh, r_full, bkt_refs, e_half, p_end, strict, incl = ([] for _ in range(11))
    for pp in pairs:
        d, sl = pp // ppd, slice(LANES * (pp % ppd), LANES * (pp % ppd + 1))
        v_ref, ash_ref, rsh_ref, bsh_ref, ksh_ref, rfull_ref, bkt_ref, lend_ref = refs[n_in * d:n_in * (d + 1)]
        v.append(v_ref[:, sl])
        a_sh.append(ash_ref[0, :, sl])
        r_sh.append(rsh_ref[0, :, sl])
        b_sh.append(bsh_ref[0, :, sl])
        k_sh.append(ksh_ref[0, :, sl])
        r_full.append(rfull_ref[0, :, sl])
        bkt_refs.append((bkt_ref, sl))
        l_end = lend_ref[0, 0, 0:1, sl]
        e_half.append(jnp.exp(0.5 * l_end))
        p_end.append(jnp.exp(l_end))
        strict.append(strict_d[d])
        incl.append(incl_d[d])
    y_refs = refs[2 * n_in:2 * n_in + 2]

    amat = []
    for p in pairs:
        ar = jnp.concatenate([a_sh[p], r_sh[p]], axis=0)
        bk = jnp.concatenate([b_sh[p], k_sh[p]], axis=0)
        amat.append(lax.dot_general(both_heads(ar), bk, (((1,), (1,)), ((), ())),
                                    preferred_element_type=F32))
    n_bf, a_ak, a_rb, a_rk, t_inv = [], [], [], [], []
    for hd in heads:
        ah = amat[hd // 2][2 * C * (hd % 2):2 * C * (hd % 2 + 1)]
        n = jnp.where(strict[hd // 2], ah[:C, :C], 0.0)
        n_bf.append(_bf(n))
        t_inv.append(eye + n)
        a_ak.append(jnp.where(strict[hd // 2], ah[:C, C:], 0.0))
        a_rb.append(jnp.where(incl[hd // 2], ah[C:, :C], 0.0))
        a_rk.append(jnp.where(incl[hd // 2], ah[C:, C:], 0.0))
    def pair_products(lhs, rhs):
        out = []
        for p in pairs:
            a, b = rhs[2 * p], rhs[2 * p + 1]
            zero = jnp.zeros_like(a)
            diag = jnp.concatenate([jnp.concatenate([a, zero], axis=1),
                                    jnp.concatenate([zero, b], axis=1)], axis=0)
            prod = _dot(jnp.concatenate(lhs[2 * p:2 * p + 2], axis=1), diag)
            out += [prod[:, :C], prod[:, C:]]
        return out

    pw = [_bf(x) for x in pair_products(n_bf, n_bf)]
    avy = []
    for p in pairs:
        akrk = jnp.concatenate([jnp.concatenate(a_ak[2 * p:2 * p + 2], axis=1),
                                jnp.concatenate(a_rk[2 * p:2 * p + 2], axis=1)], axis=0)
        avy.append(_dot(_bf(akrk), both_heads(v[p])))
    for _ in range(5):
        both = [_dot(pw[hd], jnp.concatenate([pw[hd], _bf(t_inv[hd])], axis=1)) for hd in heads]
        pw = [_bf(both[hd][:, :C]) for hd in heads]
        t_inv = [t_inv[hd] + both[hd][:, C:] for hd in heads]
    last = pair_products(pw, [_bf(t) for t in t_inv])
    t_inv = [t_inv[hd] + last[hd] for hd in heads]

    aw = []
    for p in pairs:
        rhs = jnp.concatenate([both_heads(a_sh[p]), both_heads(_bf(avy[p][:C]))], axis=1)
        aw.append(_dot(_bf(jnp.concatenate(t_inv[2 * p:2 * p + 2], axis=1)), rhs))

    zs = [z_scr[p] for p in pairs]
    xs = []
    for q in range(0, 2 * ppd, 2):
        a_bar = jnp.concatenate([aw[q][:, :LANES] * e_half[q], aw[q + 1][:, :LANES] * e_half[q + 1]], axis=1)
        zero = jnp.zeros((LANES, LANES), BF16)
        z_diag = jnp.concatenate([jnp.concatenate([_bf(zs[q]), zero], axis=1),
                                  jnp.concatenate([zero, _bf(zs[q + 1])], axis=1)], axis=0)
        r_wide = jnp.concatenate([r_full[q], r_full[q + 1]], axis=1)
        x2 = _dot(jnp.concatenate([_bf(a_bar), r_wide], axis=0), z_diag)
        xs += [x2[:, :LANES], x2[:, LANES:]]
    us = [xs[p][:C] + aw[p][:, LANES:] for p in pairs]
    for p in pairs:
        y = xs[p][C:] + avy[p][C:] + _dot(_bf(jnp.concatenate(a_rb[2 * p:2 * p + 2], axis=1)),
                                           _bf(both_heads(us[p])))
        bkt_ref, sl = bkt_refs[p]
        y_refs[p // ppd][:, sl] = y
    for p in pairs:
        bkt_ref, sl = bkt_refs[p]
        bkt = bkt_ref[0, 0, sl, :]
        uv = jnp.concatenate([_bf(us[p]), v[p]], axis=0)
        pend_col = jnp.broadcast_to(p_end[p], (LANES, LANES)).T
        z_new = zs[p] * pend_col + _dot(bkt, uv)
        z_scr[p] = jnp.where(blockdiag, z_new, 0.0)


def _rwkv_scan(v, ash, rsh, bsh, ksh, rfull, bkt, lend, B, T):
    m = v.shape[0]
    nc = T // CHUNK

    def specs(d):
        def blk(bi, c):
            return bi * nc + c + d * (nc - 1 - 2 * c)

        tok = pl.BlockSpec((CHUNK, RWKV_WIDTH), lambda bi, c: (blk(bi, c), 0))
        tok2 = pl.BlockSpec((1, CHUNK, RWKV_WIDTH), lambda bi, c: (d, blk(bi, c), 0))
        return tok, [
            tok, tok2, tok2, tok2, tok2, tok2,
            pl.BlockSpec((1, 1, RWKV_WIDTH, 2 * CHUNK), lambda bi, c: (d, blk(bi, c), 0, 0)),
            pl.BlockSpec((1, 1, SUBLANES, RWKV_WIDTH), lambda bi, c: (d, blk(bi, c), 0, 0)),
        ]

    (out_f, in_f), (out_b, in_b) = specs(0), specs(1)
    operands = (v, ash, rsh, bsh, ksh, rfull, bkt, lend)
    y_shape = jax.ShapeDtypeStruct((m, RWKV_WIDTH), F32)
    return pl.pallas_call(
        _scan_kernel,
        grid=(B, nc),
        in_specs=in_f + in_b,
        out_specs=[out_f, out_b],
        out_shape=[y_shape, y_shape],
        scratch_shapes=[pltpu.VMEM((RWKV_HEADS, LANES, LANES), F32)],
        compiler_params=_params(("parallel", "arbitrary")),
        name="rwkv_scan",
    )(*operands, *operands)


def _att_prep_kernel(z_ref, gain_ref, cos_ref, sin_ref, hs_ref, qt_o, k_o, vt_o, qn_o, kn_o):
    i = pl.program_id(1)
    z = z_ref[...]
    qk = z[:, :640]
    ss = _dot_exact_rhs(qk * qk, hs_ref[...])
    qk = qk * lax.rsqrt(ss * (1.0 / HEAD_DIM) + NORM_EPS) * gain_ref[...]
    width = qk.shape[1]
    lane = lax.broadcasted_iota(jnp.int32, (1, width), 1)
    first = (lane % (2 * ROPE_PAIRS)) < ROPE_PAIRS
    partner = jnp.where(first, pltpu.roll(qk, width - ROPE_PAIRS, 1), pltpu.roll(qk, ROPE_PAIRS, 1))
    cos = jnp.concatenate([cos_ref[...]] * 5, axis=1)
    sin = jnp.concatenate([sin_ref[...]] * 5, axis=1)
    qk = qk * cos + partner * sin
    for j in range(4):
        st = _bf(qk[:, LANES * j:LANES * (j + 1)].T)
        h = j // 2
        g = (2 * j) % ATT_GROUP
        qt_o[0, h, :, Q_POS * g:Q_POS * (g + 1)] = st[:HEAD_DIM]
        qt_o[0, h, :, Q_POS * (g + 1):Q_POS * (g + 2)] = st[HEAD_DIM:]
        sq = st.astype(F32)
        sq = sq * sq
        for e in range(2):
            nrm = jnp.sum(sq[HEAD_DIM * e:HEAD_DIM * (e + 1)], axis=0, keepdims=True)
            qn_o[0, h, :, Q_POS * (g + e):Q_POS * (g + e + 1)] = jnp.broadcast_to(nrm, (SUBLANES, Q_POS))
    kb = _bf(qk[:, 512:640])
    k_o[0] = kb
    kf = kb.astype(F32)
    kn = jnp.max(_dot_exact_rhs(kf * kf, hs_ref[512:640, 512:640]), axis=0, keepdims=True)
    kn = jnp.broadcast_to(kn, (SUBLANES, KV_WIDTH))

    @pl.when(i == 0)
    def _():
        kn_o[0] = kn

    @pl.when(i > 0)
    def _():
        kn_o[0] = jnp.maximum(kn_o[0], kn)

    vt = _bf(z[:, 640:768].T)
    ones = jnp.ones((VT_ROWS - HEAD_DIM, Q_POS), BF16)
    for h in range(ATT_KV_HEADS):
        vt_o[0, h, :HEAD_DIM, :] = vt[HEAD_DIM * h:HEAD_DIM * (h + 1)]
        vt_o[0, h, HEAD_DIM:, :] = ones


def _att_prep(z_a, B, T, p, cos_t, sin_t):
    nb = T // Q_POS
    full = lambda shape: pl.BlockSpec(shape, lambda bi, i: tuple(0 for _ in shape))
    return pl.pallas_call(
        _att_prep_kernel,
        grid=(B, nb),
        in_specs=[
            pl.BlockSpec((Q_POS, ATT_COLS), lambda bi, i: (bi * nb + i, 0)),
            full((1, 640)),
            pl.BlockSpec((Q_POS, LANES), lambda bi, i: (i, 0)),
            pl.BlockSpec((Q_POS, LANES), lambda bi, i: (i, 0)),
            full((640, 640)),
        ],
        out_specs=[
            pl.BlockSpec((1, ATT_KV_HEADS, HEAD_DIM, ATT_GROUP * Q_POS), lambda bi, i: (bi, 0, 0, i)),
            pl.BlockSpec((1, Q_POS, KV_WIDTH), lambda bi, i: (bi, i, 0)),
            pl.BlockSpec((1, ATT_KV_HEADS, VT_ROWS, Q_POS), lambda bi, i: (bi, 0, 0, i)),
            pl.BlockSpec((1, ATT_KV_HEADS, SUBLANES, ATT_GROUP * Q_POS), lambda bi, i: (bi, 0, 0, i)),
            pl.BlockSpec((1, SUBLANES, KV_WIDTH), lambda bi, i: (bi, 0, 0)),
        ],
        out_shape=[
            jax.ShapeDtypeStruct((B, ATT_KV_HEADS, HEAD_DIM, ATT_GROUP * T), BF16),
            jax.ShapeDtypeStruct((B, T, KV_WIDTH), BF16),
            jax.ShapeDtypeStruct((B, ATT_KV_HEADS, VT_ROWS, T), BF16),
            jax.ShapeDtypeStruct((B, ATT_KV_HEADS, SUBLANES, ATT_GROUP * T), F32),
            jax.ShapeDtypeStruct((B, SUBLANES, KV_WIDTH), F32),
        ],
        compiler_params=_params(("parallel", "arbitrary")),
        name="att_prep",
    )(z_a, p["qk_gain"], cos_t, sin_t, p["head_ones_qk"])


def _attn_kernel(qt_ref, qn_ref, kn_ref, k_ref, vt_ref, o_ref, *, n_kv, tkv, unroll):
    h = pl.program_id(1)
    ncol = ATT_GROUP * Q_POS
    qt = qt_ref[0, 0]
    rowh = lax.broadcasted_iota(jnp.int32, (KV_WIDTH, 1), 0) // HEAD_DIM
    q2 = jnp.where(rowh == h, jnp.concatenate([qt, qt], axis=0), jnp.zeros((), BF16))
    laneh = lax.broadcasted_iota(jnp.int32, (1, KV_WIDTH), 1) // HEAD_DIM
    kn = jnp.max(jnp.where(laneh == h, kn_ref[0, 0:1, :], 0.0), axis=1, keepdims=True)
    shift = jnp.sqrt(qn_ref[0, 0, 0:1, :] * kn)

    def chunk_start(j):
        return j * tkv if isinstance(j, int) else pl.multiple_of(j * tkv, tkv)

    def scores(j):
        kc = k_ref[0, pl.ds(chunk_start(j), tkv), :]
        return _dot(kc, q2)

    def pv(j, pt):
        vc = vt_ref[0, 0, :, pl.ds(chunk_start(j), tkv)]
        return _dot(vc, pt)

    def sweep(step, carry):
        def group(base, carry, s, final):
            for u in range(unroll):
                s_next = None if (final and u == unroll - 1) else scores(base + u + 1)
                carry = step(base + u, carry, s)
                s = s_next
            return carry, s

        s = scores(0)
        n_groups = n_kv // unroll
        if n_groups > 1:
            carry, s = lax.fori_loop(0, n_groups - 1, lambda j, c: group(j * unroll, *c, False), (carry, s))
        carry, _ = group((n_groups - 1) * unroll, carry, s, True)
        return carry

    def emit(acc):
        o = acc[:HEAD_DIM] / acc[HEAD_DIM:HEAD_DIM + 1]
        ot = jnp.concatenate([o, jnp.zeros_like(o)], axis=0).T
        for g in range(ATT_GROUP):
            o_ref[0, :, HEAD_DIM * g:HEAD_DIM * (g + 1)] = _bf(ot[Q_POS * g:Q_POS * (g + 1), :HEAD_DIM])

    def fast_group(base, acc):
        for u in range(unroll):
            acc = acc + pv(base + u, _bf(jnp.exp2(scores(base + u) - shift)))
        return acc

    acc0 = jnp.zeros((VT_ROWS, ncol), F32)
    n_groups = n_kv // unroll
    if n_groups > 1:
        acc = lax.fori_loop(0, n_groups, lambda j, acc: fast_group(j * unroll, acc), acc0)
    else:
        acc = fast_group(0, acc0)
    emit(acc)
    denom_ok = jnp.min(acc[HEAD_DIM:HEAD_DIM + 1]) >= DENOM_FLOOR

    @pl.when(jnp.logical_not(denom_ok))
    def _():
        def step(j, carry, s):
            m, acc = carry
            m_new = jnp.maximum(m, jnp.max(s, axis=0, keepdims=True))
            return m_new, acc * jnp.exp2(m - m_new) + pv(j, _bf(jnp.exp2(s - m_new)))

        _, acc_online = sweep(step, (jnp.full((1, ncol), -jnp.inf, F32), acc0))
        emit(acc_online)


def _attention(qt, qn, kn, k, vt, B, T, tkv):
    nb = T // Q_POS
    n_kv = T // tkv
    kern = functools.partial(_attn_kernel, n_kv=n_kv, tkv=tkv, unroll=_tile(n_kv, KV_UNROLL))
    return pl.pallas_call(
        kern,
        grid=(B, ATT_KV_HEADS, nb),
        in_specs=[
            pl.BlockSpec((1, 1, HEAD_DIM, ATT_GROUP * Q_POS), lambda bi, h, i: (bi, h, 0, i)),
            pl.BlockSpec((1, 1, SUBLANES, ATT_GROUP * Q_POS), lambda bi, h, i: (bi, h, 0, i)),
            pl.BlockSpec((1, SUBLANES, KV_WIDTH), lambda bi, h, i: (bi, 0, 0)),
            pl.BlockSpec((1, T, KV_WIDTH), lambda bi, h, i: (bi, 0, 0)),
            pl.BlockSpec((1, 1, VT_ROWS, T), lambda bi, h, i: (bi, h, 0, 0)),
        ],
        out_specs=pl.BlockSpec((1, Q_POS, ATT_GROUP * HEAD_DIM), lambda bi, h, i: (bi, i, h)),
        out_shape=jax.ShapeDtypeStruct((B, T, ATT_WIDTH), BF16),
        compiler_params=_params(("parallel", "parallel", "arbitrary")),
        name="attention",
    )(qt, qn, kn, k, vt)


def _mixout_kernel(x_ref, yf_ref, yb_ref, g_ref, bg_ref, att_ref, lnw_ref, lnb_ref, hs_ref, wo_r_ref, wo_a_ref,
                   o_ref):
    y = yf_ref[...] + yb_ref[...]
    hs = hs_ref[...]
    mu = _dot_exact_rhs(y, hs) * (1.0 / HEAD_DIM)
    dy = y - mu
    var = _dot_exact_rhs(dy * dy, hs) * (1.0 / HEAD_DIM)
    yn = dy * lax.rsqrt(var + LNX_EPS) * lnw_ref[...] + lnb_ref[...]
    yr = yn * g_ref[...] + bg_ref[...]
    o_ref[...] = x_ref[...] + _dot(_bf(yr), wo_r_ref[...]) + _dot(att_ref[...], wo_a_ref[...])


def _mixout(x2, y_f, y_b, g, bg, att, p, tm):
    m = x2.shape[0]
    full = lambda shape: pl.BlockSpec(shape, lambda i: tuple(0 for _ in shape))
    tok = pl.BlockSpec((tm, 512), lambda i: (i, 0))
    return pl.pallas_call(
        _mixout_kernel,
        grid=(m // tm,),
        in_specs=[
            pl.BlockSpec((tm, D_MODEL), lambda i: (i, 0)),
            tok, tok, tok, tok, tok,
            full((1, 512)), full((1, 512)), full((512, 512)),
            full((512, D_MODEL)), full((512, D_MODEL)),
        ],
        out_specs=pl.BlockSpec((tm, D_MODEL), lambda i: (i, 0)),
        out_shape=jax.ShapeDtypeStruct((m, D_MODEL), F32),
        compiler_params=_params(("parallel",)),
        name="mixout",
    )(x2, y_f, y_b, g, bg, att, p["lnx_w"], p["lnx_b"], p["head_ones"], p["wo_r"], p["wo_a"])


def _ffn_kernel(x_ref, g2_ref, wg_ref, wu_ref, wd_ref, gf_ref, o_ref, h_scr, acc_scr):
    j = pl.program_id(1)

    @pl.when(j == 0)
    def _():
        x = x_ref[...]
        ms = jnp.mean(x * x, axis=-1, keepdims=True)
        h_scr[...] = _bf(x * lax.rsqrt(ms + NORM_EPS) * g2_ref[...])
        acc_scr[...] = x

    h = h_scr[...]
    gate = _dot(h, wg_ref[...])
    up = _dot(h, wu_ref[...])
    act = gate * _sigmoid(gate) * up
    acc_scr[...] += _dot(_bf(act), wd_ref[...])

    @pl.when(j == pl.num_programs(1) - 1)
    def _():
        xo = acc_scr[...]
        ms = jnp.mean(xo * xo, axis=-1, keepdims=True)
        o_ref[...] = xo * lax.rsqrt(ms + NORM_EPS) * gf_ref[...]


def _ffn(x1, p, tm, tf):
    m = x1.shape[0]
    return pl.pallas_call(
        _ffn_kernel,
        grid=(m // tm, D_FF // tf),
        in_specs=[
            pl.BlockSpec((tm, D_MODEL), lambda i, j: (i, 0)),
            pl.BlockSpec((1, D_MODEL), lambda i, j: (0, 0)),
            pl.BlockSpec((D_MODEL, tf), lambda i, j: (0, j)),
            pl.BlockSpec((D_MODEL, tf), lambda i, j: (0, j)),
            pl.BlockSpec((tf, D_MODEL), lambda i, j: (j, 0)),
            pl.BlockSpec((1, D_MODEL), lambda i, j: (0, 0)),
        ],
        out_specs=pl.BlockSpec((tm, D_MODEL), lambda i, j: (i, 0)),
        out_shape=jax.ShapeDtypeStruct((m, D_MODEL), F32),
        scratch_shapes=[pltpu.VMEM((tm, D_MODEL), BF16), pltpu.VMEM((tm, D_MODEL), F32)],
        compiler_params=_params(("parallel", "arbitrary")),
        name="ffn",
    )(x1, p["norm2_g"], p["ffn_gate"], p["ffn_up"], p["ffn_down"], p["norm_f_g"])


def _rope_tables(T):
    n_rows = T // GRID_W
    t = jnp.arange(T, dtype=jnp.int32)
    row = (t // GRID_W).astype(F32)
    col = (t % GRID_W).astype(F32)
    inv = ROPE_THETA ** (-jnp.arange(ROPE_PAIRS, dtype=F32) / ROPE_PAIRS)
    ar = row[:, None] * inv
    ac = col[:, None] * inv
    cos = jnp.concatenate([jnp.cos(ar), jnp.cos(ar), jnp.cos(ac), jnp.cos(ac)], axis=1)
    sin = jnp.concatenate([-jnp.sin(ar), jnp.sin(ar), -jnp.sin(ac), jnp.sin(ac)], axis=1)
    del n_rows
    return jnp.tile(cos, (1, 2)), jnp.tile(sin, (1, 2))


def _block_diag2(a, b):
    za = jnp.zeros_like(a)
    return jnp.concatenate([jnp.concatenate([a, za], axis=1), jnp.concatenate([za, b], axis=1)], axis=0)


def _hi_lo(w):
    hi = w.astype(BF16)
    return hi, (w - hi.astype(F32)).astype(BF16)


def _prepare_params(norm1_g, w_in, mu_prev, mu_next, k_k, k_a, r_k, w0_f, w_lora_f, w0_b, w_lora_b,
                    a0_f, a_lora_f, a0_b, a_lora_b, g_lora, lnx_w, lnx_b, q_gain, k_gain, w_out,
                    norm2_g, ffn_gate, ffn_up, ffn_down, norm_f_g):
    l = 0
    p = {}
    p["norm1_g"] = norm1_g[l][None]
    p["w_r"] = w_in[l][:, :RWKV_COLS].astype(BF16)
    p["w_a"] = w_in[l][:, RWKV_COLS:].astype(BF16)
    p["mu_prev"] = mu_prev[l][None]
    p["mu_next"] = mu_next[l][None]
    p["k_k"] = k_k[l][None]
    p["k_a"] = k_a[l][None]
    p["r_k"] = r_k[l].reshape(1, RWKV_WIDTH)
    p["w0"] = jnp.concatenate([w0_f[l], w0_b[l]])[None]
    p["wl_hi"], p["wl_lo"] = _hi_lo(_block_diag2(w_lora_f[l], w_lora_b[l]))
    p["a0"] = jnp.concatenate([a0_f[l], a0_b[l]])[None]
    p["al_hi"], p["al_lo"] = _hi_lo(_block_diag2(a_lora_f[l], a_lora_b[l]))
    p["gl_hi"], p["gl_lo"] = _hi_lo(g_lora[l])
    p["lnx_w"] = lnx_w[l][None]
    p["lnx_b"] = lnx_b[l][None]
    scale = HEAD_DIM ** -0.5 * float(np.log2(np.e))
    p["qk_gain"] = jnp.concatenate([jnp.tile(q_gain[l] * scale, ATT_Q_HEADS), jnp.tile(k_gain[l], ATT_KV_HEADS)])[None]
    hid = np.arange(640) // HEAD_DIM
    ones = (hid[:, None] == hid[None, :]).astype(np.float32)
    p["head_ones_qk"] = jnp.asarray(ones, BF16)
    p["head_ones"] = jnp.asarray(ones[:512, :512], BF16)
    p["wo_r"] = w_out[l][:RWKV_WIDTH].astype(BF16)
    p["wo_a"] = w_out[l][RWKV_WIDTH:].astype(BF16)
    p["norm2_g"] = norm2_g[l][None]
    p["ffn_gate"] = ffn_gate[l].astype(BF16)
    p["ffn_up"] = ffn_up[l].astype(BF16)
    p["ffn_down"] = ffn_down[l].astype(BF16)
    p["norm_f_g"] = norm_f_g[None]
    return p


def _tile(n, pref):
    t = pref
    while n % t:
        t //= 2
    return t


def _trunk(x, p):
    B, T, D = x.shape
    m = B * T
    x2 = x.reshape(m, D)
    z_r, z_a = _inproj(x2, p["norm1_g"], p["w_r"], p["w_a"], _tile(m, 512))
    v, ash, rsh, bsh, ksh, rfull, bkt, lend, g, bg = _rwkv_prep(z_r, T, p, _tile(T, 256))
    y_f, y_b = _rwkv_scan(v, ash, rsh, bsh, ksh, rfull, bkt, lend, B, T)
    cos_t, sin_t = _rope_tables(T)
    qt, k, vt, qn, kn = _att_prep(z_a, B, T, p, cos_t, sin_t)
    att = _attention(qt, qn, kn, k, vt, B, T, _tile(T, 512)).reshape(m, ATT_WIDTH)
    x1 = _mixout(x2, y_f, y_b, g, bg, att, p, _tile(m, 512))
    out = _ffn(x1, p, _tile(m, 512), 1408)
    return out.reshape(B, T, D)


def kernel(x_prompt, x_sample, norm1_g, w_in, mu_prev, mu_next, k_k, k_a, r_k, w0_f, w_lora_f, w0_b, w_lora_b, a0_f, a_lora_f, a0_b, a_lora_b, g_lora, lnx_w, lnx_b, q_gain, k_gain, w_out, norm2_g, ffn_gate, ffn_up, ffn_down, norm_f_g):
    p = _prepare_params(norm1_g, w_in, mu_prev, mu_next, k_k, k_a, r_k, w0_f, w_lora_f, w0_b, w_lora_b,
                        a0_f, a_lora_f, a0_b, a_lora_b, g_lora, lnx_w, lnx_b, q_gain, k_gain, w_out,
                        norm2_g, ffn_gate, ffn_up, ffn_down, norm_f_g)
    return (_trunk(x_prompt, p), _trunk(x_sample, p))
```

```python
import functools

import jax
import jax.numpy as jnp
import numpy as np
from jax import lax
from jax.experimental import pallas as pl
from jax.experimental.pallas import tpu as pltpu

F32 = jnp.float32
BF16 = jnp.bfloat16

D_MODEL = 1024
HEAD_DIM = 64
RWKV_WIDTH = 512
RWKV_HEADS = 8
ATT_WIDTH = 512
ATT_Q_HEADS = 8
ATT_KV_HEADS = 2
ATT_GROUP = 4
KV_WIDTH = 128
RWKV_COLS = 1920
ATT_COLS = 768
D_FF = 2816
GRID_W = 64
ROPE_THETA = 10000.0
ROPE_PAIRS = 16
NORM_EPS = 1e-6
LNX_EPS = 64e-5

LANES = 128
SUBLANES = 8
CHUNK = 128
Q_POS = 256
KV_UNROLL = 8
VT_ROWS = 80
DECAY_SCALE = float(np.exp(-0.5))
DENOM_FLOOR = 2.0 ** -100
VMEM_LIMIT = 56 * 1024 * 1024


def _dot(a, b):
    return jnp.dot(a, b, preferred_element_type=F32)


def _bf(x):
    return x.astype(BF16)


def _split2(x):
    hi = _bf(x)
    lo = _bf(x - hi.astype(F32))
    return hi, lo


def _split3(x):
    hi = _bf(x)
    r1 = x - hi.astype(F32)
    mid = _bf(r1)
    lo = _bf(r1 - mid.astype(F32))
    return hi, mid, lo


def _dot_exact_rhs(x, m_bf):
    hi, lo = _split2(x)
    return _dot(hi, m_bf) + _dot(lo, m_bf)


def _dot3(a, b_hi, b_lo):
    a_hi, a_lo = _split2(a)
    return _dot(a_hi, b_hi) + _dot(a_lo, b_hi) + _dot(a_hi, b_lo)


def _sigmoid(x):
    return 0.5 * jnp.tanh(0.5 * x) + 0.5


def _softplus(x):
    return jnp.maximum(x, 0.0) + jnp.log(1.0 + jnp.exp(-jnp.abs(x)))


def _params(sem):
    return pltpu.CompilerParams(dimension_semantics=sem, vmem_limit_bytes=VMEM_LIMIT)


def _inproj_kernel(x_ref, g_ref, wr_ref, wa_ref, zr_ref, za_ref):
    x = x_ref[...]
    ms = jnp.mean(x * x, axis=-1, keepdims=True)
    h = _bf(x * lax.rsqrt(ms + NORM_EPS) * g_ref[...])
    zr_ref[...] = _dot(h, wr_ref[...])
    za_ref[...] = _dot(h, wa_ref[...])


def _inproj(x2, norm1_g, w_r, w_a, tm):
    m = x2.shape[0]
    return pl.pallas_call(
        _inproj_kernel,
        grid=(m // tm,),
        in_specs=[
            pl.BlockSpec((tm, D_MODEL), lambda i: (i, 0)),
            pl.BlockSpec((1, D_MODEL), lambda i: (0, 0)),
            pl.BlockSpec((D_MODEL, RWKV_COLS), lambda i: (0, 0)),
            pl.BlockSpec((D_MODEL, ATT_COLS), lambda i: (0, 0)),
        ],
        out_specs=[
            pl.BlockSpec((tm, RWKV_COLS), lambda i: (i, 0)),
            pl.BlockSpec((tm, ATT_COLS), lambda i: (i, 0)),
        ],
        out_shape=[
            jax.ShapeDtypeStruct((m, RWKV_COLS), F32),
            jax.ShapeDtypeStruct((m, ATT_COLS), F32),
        ],
        compiler_params=_params(("parallel",)),
        name="inproj",
    )(x2, norm1_g, w_r, w_a)


def _rwkv_prep_kernel(z_ref, zp_ref, zn_ref, mup_ref, mun_ref, kk_ref, ka_ref, rk_ref,
                      w0_ref, wlh_ref, wll_ref, a0_ref, alh_ref, all_ref, glh_ref, gll_ref,
                      hs_ref,
                      v_o, ash_o, rsh_o, bsh_o, ksh_o, rfull_o, bkt_o, lend_o, g_o, bg_o,
                      *, tm, blocks_per_seq):
    i = pl.program_id(0)
    pos = i % blocks_per_seq
    z = z_ref[...]
    prev_row = jnp.where(pos == 0, 0.0, zp_ref[SUBLANES - 1:SUBLANES, :])
    next_row = jnp.where(pos == blocks_per_seq - 1, 0.0, zn_ref[0:1, :])
    rows = lax.broadcasted_iota(jnp.int32, (SUBLANES, 1), 0)
    z_prev = pltpu.roll(z, 1, 0)
    z_prev = jnp.concatenate([jnp.where(rows == 0, prev_row, z_prev[:SUBLANES]), z_prev[SUBLANES:]], axis=0)
    z_next = pltpu.roll(z, tm - 1, 0)
    z_next = jnp.concatenate([z_next[:tm - SUBLANES],
                              jnp.where(rows == SUBLANES - 1, next_row, z_next[tm - SUBLANES:])], axis=0)
    zf = z + mup_ref[...] * (z_prev - z) + mun_ref[...] * (z_next - z)

    r = zf[:, 0:512]
    k = zf[:, 512:1024]
    v = zf[:, 1024:1536]
    wd = zf[:, 1536:1664]
    ad = zf[:, 1664:1792]
    gd = zf[:, 1792:1920]
    hs = hs_ref[...]

    kk = k * kk_ref[...]
    ss = _dot_exact_rhs(kk * kk, hs)
    kk = kk * lax.rsqrt(jnp.maximum(ss, 1e-12))

    lw_both = w0_ref[...] + _dot3(jnp.tanh(wd), wlh_ref[...], wll_ref[...])
    as_both = a0_ref[...] + _dot3(ad, alh_ref[...], all_ref[...])
    ka = ka_ref[...]
    kb = jnp.zeros_like(k)
    ri = lax.broadcasted_iota(jnp.int32, (tm, tm), 0)
    ci = lax.broadcasted_iota(jnp.int32, (tm, tm), 1)
    same_chunk = (ri // CHUNK) == (ci // CHUNK)
    chunk_ones = same_chunk.astype(BF16)
    for d in range(2):
        lw = -DECAY_SCALE * _sigmoid(lw_both[:, 512 * d:512 * (d + 1)])
        a = _sigmoid(as_both[:, 512 * d:512 * (d + 1)])
        kd = k * (1.0 + (a - 1.0) * ka)
        b = kk * a
        kb = kb + kd
        processed = (ci <= ri) if d == 0 else (ci >= ri)
        tri = (same_chunk & processed).astype(BF16)
        hi, mid, lo = _split3(lw)
        cum = _dot(tri, hi) + _dot(tri, mid) + _dot(tri, lo)
        l_end = _dot(chunk_ones, hi) + _dot(chunk_ones, mid) + _dot(chunk_ones, lo)
        l_half = 0.5 * l_end
        e_half = jnp.exp(l_half)
        r_sh = r * jnp.exp(cum - l_half)
        ash_o[d] = _bf(-kk * jnp.exp(cum - lw - l_half))
        rsh_o[d] = _bf(r_sh)
        rfull_o[d] = _bf(r_sh * e_half)
        e_b = jnp.exp(l_half - cum)
        b_sh = b * e_b
        k_sh = kd * e_b
        bsh_o[d] = _bf(b_sh)
        ksh_o[d] = _bf(k_sh)
        b_hat = b_sh * e_half
        k_hat = k_sh * e_half
        for cc in range(tm // CHUNK):
            rs = slice(CHUNK * cc, CHUNK * (cc + 1))
            lend_o[d, cc] = l_end[CHUNK * cc:CHUNK * cc + SUBLANES]
            for p in range(RWKV_HEADS // 2):
                ls = slice(LANES * p, LANES * (p + 1))
                bkt_o[d, cc, ls, 0:CHUNK] = _bf(b_hat[rs, ls].T)
                bkt_o[d, cc, ls, CHUNK:2 * CHUNK] = _bf(k_hat[rs, ls].T)
    kb = 0.5 * kb
    coef = _dot_exact_rhs(r * kb * rk_ref[...], hs)
    g = _dot3(_sigmoid(gd), glh_ref[...], gll_ref[...])
    v_o[...] = _bf(v)
    g_o[...] = _bf(g)
    bg_o[...] = _bf(coef * v * g)


def _rwkv_prep(z_r, T, p, tm):
    m = z_r.shape[0]
    bps = T // tm
    hb = tm // SUBLANES
    nhalo = m // SUBLANES
    full = lambda shape: pl.BlockSpec(shape, lambda i: tuple(0 for _ in shape))
    tok = pl.BlockSpec((tm, RWKV_WIDTH), lambda i: (i, 0))
    tok2 = pl.BlockSpec((2, tm, RWKV_WIDTH), lambda i: (0, i, 0))
    cpb = tm // CHUNK
    nchunk = m // CHUNK
    tok2_shape = jax.ShapeDtypeStruct((2, m, RWKV_WIDTH), BF16)
    kern = functools.partial(_rwkv_prep_kernel, tm=tm, blocks_per_seq=bps)
    return pl.pallas_call(
        kern,
        grid=(m // tm,),
        in_specs=[
            pl.BlockSpec((tm, RWKV_COLS), lambda i: (i, 0)),
            pl.BlockSpec((SUBLANES, RWKV_COLS), lambda i: (jnp.maximum(i * hb - 1, 0), 0)),
            pl.BlockSpec((SUBLANES, RWKV_COLS), lambda i: (jnp.minimum((i + 1) * hb, nhalo - 1), 0)),
            full((1, RWKV_COLS)), full((1, RWKV_COLS)),
            full((1, 512)), full((1, 512)), full((1, 512)),
            full((1, 1024)), full((128, 1024)), full((128, 1024)),
            full((1, 1024)), full((128, 1024)), full((128, 1024)),
            full((128, 512)), full((128, 512)),
            full((512, 512)),
        ],
        out_specs=[
            tok, tok2, tok2, tok2, tok2, tok2,
            pl.BlockSpec((2, cpb, RWKV_WIDTH, 2 * CHUNK), lambda i: (0, i, 0, 0)),
            pl.BlockSpec((2, cpb, SUBLANES, RWKV_WIDTH), lambda i: (0, i, 0, 0)),
            tok, tok,
        ],
        out_shape=[
            jax.ShapeDtypeStruct((m, RWKV_WIDTH), BF16),
            tok2_shape, tok2_shape, tok2_shape, tok2_shape, tok2_shape,
            jax.ShapeDtypeStruct((2, nchunk, RWKV_WIDTH, 2 * CHUNK), BF16),
            jax.ShapeDtypeStruct((2, nchunk, SUBLANES, RWKV_WIDTH), F32),
            jax.ShapeDtypeStruct((m, RWKV_WIDTH), BF16),
            jax.ShapeDtypeStruct((m, RWKV_WIDTH), BF16),
        ],
        compiler_params=_params(("parallel",)),
        name="rwkv_prep",
    )(z_r, z_r, z_r, p["mu_prev"], p["mu_next"], p["k_k"], p["k_a"], p["r_k"],
      p["w0"], p["wl_hi"], p["wl_lo"], p["a0"], p["al_hi"], p["al_lo"], p["gl_hi"], p["gl_lo"],
      p["head_ones"])


def _scan_kernel(*refs):
    C = CHUNK
    c = pl.program_id(1)
    z_scr = refs[-1]

    @pl.when(c == 0)
    def _():
        z_scr[...] = jnp.zeros_like(z_scr)

    row = lax.broadcasted_iota(jnp.int32, (C, C), 0)
    col = lax.broadcasted_iota(jnp.int32, (C, C), 1)
    strict_d = [col < row, col > row]
    incl_d = [col <= row, col >= row]
    eye = (row == col).astype(F32)
    lane = lax.broadcasted_iota(jnp.int32, (1, LANES), 1)
    m0 = lane < HEAD_DIM
    m1 = lane >= HEAD_DIM
    blockdiag = (row < HEAD_DIM) == (col < HEAD_DIM)

    def both_heads(x):
        zero = jnp.zeros((), x.dtype)
        return jnp.concatenate([jnp.where(m0, x, zero), jnp.where(m1, x, zero)], axis=0)

    ppd = RWKV_HEADS // 2
    pairs = range(2 * ppd)
    heads = range(2 * RWKV_HEADS)
    n_in = (len(refs) - 3) // 2
    v, a_sh, r_sh, b_sh, k_sh, r_full, bkt_refs, e_half, p_end, strict, incl = ([] for _ in range(11))
    for pp in pairs:
        d, sl = pp // ppd, slice(LANES * (pp % ppd), LANES * (pp % ppd + 1))
        v_ref, ash_ref, rsh_ref, bsh_ref, ksh_ref, rfull_ref, bkt_ref, lend_ref = refs[n_in * d:n_in * (d + 1)]
        v.append(v_ref[:, sl])
        a_sh.append(ash_ref[0, :, sl])
        r_sh.append(rsh_ref[0, :, sl])
        b_sh.append(bsh_ref[0, :, sl])
        k_sh.append(ksh_ref[0, :, sl])
        r_full.append(rfull_ref[0, :, sl])
        bkt_refs.append((bkt_ref, sl))
        l_end = lend_ref[0, 0, 0:1, sl]
        e_half.append(jnp.exp(0.5 * l_end))
        p_end.append(jnp.exp(l_end))
        strict.append(strict_d[d])
        incl.append(incl_d[d])
    y_refs = refs[2 * n_in:2 * n_in + 2]

    amat = []
    for p in pairs:
        ar = jnp.concatenate([a_sh[p], r_sh[p]], axis=0)
        bk = jnp.concatenate([b_sh[p], k_sh[p]], axis=0)
        amat.append(lax.dot_general(both_heads(ar), bk, (((1,), (1,)), ((), ())),
                                    preferred_element_type=F32))
    n_bf, a_ak, a_rb, a_rk, t_inv = [], [], [], [], []
    for hd in heads:
        ah = amat[hd // 2][2 * C * (hd % 2):2 * C * (hd % 2 + 1)]
        n = jnp.where(strict[hd // 2], ah[:C, :C], 0.0)
        n_bf.append(_bf(n))
        t_inv.append(eye + n)
        a_ak.append(jnp.where(strict[hd // 2], ah[:C, C:], 0.0))
        a_rb.append(jnp.where(incl[hd // 2], ah[C:, :C], 0.0))
        a_rk.append(jnp.where(incl[hd // 2], ah[C:, C:], 0.0))
    def pair_products(lhs, rhs):
        out = []
        for p in pairs:
            a, b = rhs[2 * p], rhs[2 * p + 1]
            zero = jnp.zeros_like(a)
            diag = jnp.concatenate([jnp.concatenate([a, zero], axis=1),
                                    jnp.concatenate([zero, b], axis=1)], axis=0)
            prod = _dot(jnp.concatenate(lhs[2 * p:2 * p + 2], axis=1), diag)
            out += [prod[:, :C], prod[:, C:]]
        return out

    pw = [_bf(x) for x in pair_products(n_bf, n_bf)]
    avy = []
    for p in pairs:
        akrk = jnp.concatenate([jnp.concatenate(a_ak[2 * p:2 * p + 2], axis=1),
                                jnp.concatenate(a_rk[2 * p:2 * p + 2], axis=1)], axis=0)
        avy.append(_dot(_bf(akrk), both_heads(v[p])))
    for _ in range(5):
        both = [_dot(pw[hd], jnp.concatenate([pw[hd], _bf(t_inv[hd])], axis=1)) for hd in heads]
        pw = [_bf(both[hd][:, :C]) for hd in heads]
        t_inv = [t_inv[hd] + both[hd][:, C:] for hd in heads]
    last = pair_products(pw, [_bf(t) for t in t_inv])
    t_inv = [t_inv[hd] + last[hd] for hd in heads]

    aw = []
    for p in pairs:
        rhs = jnp.concatenate([both_heads(a_sh[p]), both_heads(_bf(avy[p][:C]))], axis=1)
        aw.append(_dot(_bf(jnp.concatenate(t_inv[2 * p:2 * p + 2], axis=1)), rhs))

    zs = [z_scr[p] for p in pairs]
    xs = []
    for q in range(0, 2 * ppd, 2):
        a_bar = jnp.concatenate([aw[q][:, :LANES] * e_half[q], aw[q + 1][:, :LANES] * e_half[q + 1]], axis=1)
        zero = jnp.zeros((LANES, LANES), BF16)
        z_diag = jnp.concatenate([jnp.concatenate([_bf(zs[q]), zero], axis=1),
                                  jnp.concatenate([zero, _bf(zs[q + 1])], axis=1)], axis=0)
        r_wide = jnp.concatenate([r_full[q], r_full[q + 1]], axis=1)
        x2 = _dot(jnp.concatenate([_bf(a_bar), r_wide], axis=0), z_diag)
        xs += [x2[:, :LANES], x2[:, LANES:]]
    us = [xs[p][:C] + aw[p][:, LANES:] for p in pairs]
    for p in pairs:
        y = xs[p][C:] + avy[p][C:] + _dot(_bf(jnp.concatenate(a_rb[2 * p:2 * p + 2], axis=1)),
                                           _bf(both_heads(us[p])))
        bkt_ref, sl = bkt_refs[p]
        y_refs[p // ppd][:, sl] = y
    for p in pairs:
        bkt_ref, sl = bkt_refs[p]
        bkt = bkt_ref[0, 0, sl, :]
        uv = jnp.concatenate([_bf(us[p]), v[p]], axis=0)
        pend_col = jnp.broadcast_to(p_end[p], (LANES, LANES)).T
        z_new = zs[p] * pend_col + _dot(bkt, uv)
        z_scr[p] = jnp.where(blockdiag, z_new, 0.0)


def _rwkv_scan(v, ash, rsh, bsh, ksh, rfull, bkt, lend, B, T):
    m = v.shape[0]
    nc = T // CHUNK

    def specs(d):
        def blk(bi, c):
            return bi * nc + c + d * (nc - 1 - 2 * c)

        tok = pl.BlockSpec((CHUNK, RWKV_WIDTH), lambda bi, c: (blk(bi, c), 0))
        tok2 = pl.BlockSpec((1, CHUNK, RWKV_WIDTH), lambda bi, c: (d, blk(bi, c), 0))
        return tok, [
            tok, tok2, tok2, tok2, tok2, tok2,
            pl.BlockSpec((1, 1, RWKV_WIDTH, 2 * CHUNK), lambda bi, c: (d, blk(bi, c), 0, 0)),
            pl.BlockSpec((1, 1, SUBLANES, RWKV_WIDTH), lambda bi, c: (d, blk(bi, c), 0, 0)),
        ]

    (out_f, in_f), (out_b, in_b) = specs(0), specs(1)
    operands = (v, ash, rsh, bsh, ksh, rfull, bkt, lend)
    y_shape = jax.ShapeDtypeStruct((m, RWKV_WIDTH), F32)
    return pl.pallas_call(
        _scan_kernel,
        grid=(B, nc),
        in_specs=in_f + in_b,
        out_specs=[out_f, out_b],
        out_shape=[y_shape, y_shape],
        scratch_shapes=[pltpu.VMEM((RWKV_HEADS, LANES, LANES), F32)],
        compiler_params=_params(("parallel", "arbitrary")),
        name="rwkv_scan",
    )(*operands, *operands)


def _att_prep_kernel(z_ref, gain_ref, cos_ref, sin_ref, hs_ref, qt_o, k_o, vt_o, qn_o, kn_o):
    i = pl.program_id(1)
    z = z_ref[...]
    qk = z[:, :640]
    ss = _dot_exact_rhs(qk * qk, hs_ref[...])
    qk = qk * lax.rsqrt(ss * (1.0 / HEAD_DIM) + NORM_EPS) * gain_ref[...]
    width = qk.shape[1]
    lane = lax.broadcasted_iota(jnp.int32, (1, width), 1)
    first = (lane % (2 * ROPE_PAIRS)) < ROPE_PAIRS
    partner = jnp.where(first, pltpu.roll(qk, width - ROPE_PAIRS, 1), pltpu.roll(qk, ROPE_PAIRS, 1))
    cos = jnp.concatenate([cos_ref[...]] * 5, axis=1)
    sin = jnp.concatenate([sin_ref[...]] * 5, axis=1)
    qk = qk * cos + partner * sin
    for j in range(4):
        st = _bf(qk[:, LANES * j:LANES * (j + 1)].T)
        h = j // 2
        g = (2 * j) % ATT_GROUP
        qt_o[0, h, :, Q_POS * g:Q_POS * (g + 1)] = st[:HEAD_DIM]
        qt_o[0, h, :, Q_POS * (g + 1):Q_POS * (g + 2)] = st[HEAD_DIM:]
        sq = st.astype(F32)
        sq = sq * sq
        for e in range(2):
            nrm = jnp.sum(sq[HEAD_DIM * e:HEAD_DIM * (e + 1)], axis=0, keepdims=True)
            qn_o[0, h, :, Q_POS * (g + e):Q_POS * (g + e + 1)] = jnp.broadcast_to(nrm, (SUBLANES, Q_POS))
    kb = _bf(qk[:, 512:640])
    k_o[0] = kb
    kf = kb.astype(F32)
    kn = jnp.max(_dot_exact_rhs(kf * kf, hs_ref[512:640, 512:640]), axis=0, keepdims=True)
    kn = jnp.broadcast_to(kn, (SUBLANES, KV_WIDTH))

    @pl.when(i == 0)
    def _():
        kn_o[0] = kn

    @pl.when(i > 0)
    def _():
        kn_o[0] = jnp.maximum(kn_o[0], kn)

    vt = _bf(z[:, 640:768].T)
    ones = jnp.ones((VT_ROWS - HEAD_DIM, Q_POS), BF16)
    for h in range(ATT_KV_HEADS):
        vt_o[0, h, :HEAD_DIM, :] = vt[HEAD_DIM * h:HEAD_DIM * (h + 1)]
        vt_o[0, h, HEAD_DIM:, :] = ones


def _att_prep(z_a, B, T, p, cos_t, sin_t):
    nb = T // Q_POS
    full = lambda shape: pl.BlockSpec(shape, lambda bi, i: tuple(0 for _ in shape))
    return pl.pallas_call(
        _att_prep_kernel,
        grid=(B, nb),
        in_specs=[
            pl.BlockSpec((Q_POS, ATT_COLS), lambda bi, i: (bi * nb + i, 0)),
            full((1, 640)),
            pl.BlockSpec((Q_POS, LANES), lambda bi, i: (i, 0)),
            pl.BlockSpec((Q_POS, LANES), lambda bi, i: (i, 0)),
            full((640, 640)),
        ],
        out_specs=[
            pl.BlockSpec((1, ATT_KV_HEADS, HEAD_DIM, ATT_GROUP * Q_POS), lambda bi, i: (bi, 0, 0, i)),
            pl.BlockSpec((1, Q_POS, KV_WIDTH), lambda bi, i: (bi, i, 0)),
            pl.BlockSpec((1, ATT_KV_HEADS, VT_ROWS, Q_POS), lambda bi, i: (bi, 0, 0, i)),
            pl.BlockSpec((1, ATT_KV_HEADS, SUBLANES, ATT_GROUP * Q_POS), lambda bi, i: (bi, 0, 0, i)),
            pl.BlockSpec((1, SUBLANES, KV_WIDTH), lambda bi, i: (bi, 0, 0)),
        ],
        out_shape=[
            jax.ShapeDtypeStruct((B, ATT_KV_HEADS, HEAD_DIM, ATT_GROUP * T), BF16),
            jax.ShapeDtypeStruct((B, T, KV_WIDTH), BF16),
            jax.ShapeDtypeStruct((B, ATT_KV_HEADS, VT_ROWS, T), BF16),
            jax.ShapeDtypeStruct((B, ATT_KV_HEADS, SUBLANES, ATT_GROUP * T), F32),
            jax.ShapeDtypeStruct((B, SUBLANES, KV_WIDTH), F32),
        ],
        compiler_params=_params(("parallel", "arbitrary")),
        name="att_prep",
    )(z_a, p["qk_gain"], cos_t, sin_t, p["head_ones_qk"])


def _attn_kernel(qt_ref, qn_ref, kn_ref, k_ref, vt_ref, o_ref, *, n_kv, tkv, unroll):
    h = pl.program_id(1)
    ncol = ATT_GROUP * Q_POS
    qt = qt_ref[0, 0]
    rowh = lax.broadcasted_iota(jnp.int32, (KV_WIDTH, 1), 0) // HEAD_DIM
    q2 = jnp.where(rowh == h, jnp.concatenate([qt, qt], axis=0), jnp.zeros((), BF16))
    laneh = lax.broadcasted_iota(jnp.int32, (1, KV_WIDTH), 1) // HEAD_DIM
    kn = jnp.max(jnp.where(laneh == h, kn_ref[0, 0:1, :], 0.0), axis=1, keepdims=True)
    shift = jnp.sqrt(qn_ref[0, 0, 0:1, :] * kn)

    def chunk_start(j):
        return j * tkv if isinstance(j, int) else pl.multiple_of(j * tkv, tkv)

    def scores(j):
        kc = k_ref[0, pl.ds(chunk_start(j), tkv), :]
        return _dot(kc, q2)

    def pv(j, pt):
        vc = vt_ref[0, 0, :, pl.ds(chunk_start(j), tkv)]
        return _dot(vc, pt)

    def sweep(step, carry):
        def group(base, carry, s, final):
            for u in range(unroll):
                s_next = None if (final and u == unroll - 1) else scores(base + u + 1)
                carry = step(base + u, carry, s)
                s = s_next
            return carry, s

        s = scores(0)
        n_groups = n_kv // unroll
        if n_groups > 1:
            carry, s = lax.fori_loop(0, n_groups - 1, lambda j, c: group(j * unroll, *c, False), (carry, s))
        carry, _ = group((n_groups - 1) * unroll, carry, s, True)
        return carry

    def emit(acc):
        o = acc[:HEAD_DIM] / acc[HEAD_DIM:HEAD_DIM + 1]
        ot = jnp.concatenate([o, jnp.zeros_like(o)], axis=0).T
        for g in range(ATT_GROUP):
            o_ref[0, :, HEAD_DIM * g:HEAD_DIM * (g + 1)] = _bf(ot[Q_POS * g:Q_POS * (g + 1), :HEAD_DIM])

    def fast_group(base, acc):
        for u in range(unroll):
            acc = acc + pv(base + u, _bf(jnp.exp2(scores(base + u) - shift)))
        return acc

    acc0 = jnp.zeros((VT_ROWS, ncol), F32)
    n_groups = n_kv // unroll
    if n_groups > 1:
        acc = lax.fori_loop(0, n_groups, lambda j, acc: fast_group(j * unroll, acc), acc0)
    else:
        acc = fast_group(0, acc0)
    emit(acc)
    denom_ok = jnp.min(acc[HEAD_DIM:HEAD_DIM + 1]) >= DENOM_FLOOR

    @pl.when(jnp.logical_not(denom_ok))
    def _():
        def step(j, carry, s):
            m, acc = carry
            m_new = jnp.maximum(m, jnp.max(s, axis=0, keepdims=True))
            return m_new, acc * jnp.exp2(m - m_new) + pv(j, _bf(jnp.exp2(s - m_new)))

        _, acc_online = sweep(step, (jnp.full((1, ncol), -jnp.inf, F32), acc0))
        emit(acc_online)


def _attention(qt, qn, kn, k, vt, B, T, tkv):
    nb = T // Q_POS
    n_kv = T // tkv
    kern = functools.partial(_attn_kernel, n_kv=n_kv, tkv=tkv, unroll=_tile(n_kv, KV_UNROLL))
    return pl.pallas_call(
        kern,
        grid=(B, ATT_KV_HEADS, nb),
        in_specs=[
            pl.BlockSpec((1, 1, HEAD_DIM, ATT_GROUP * Q_POS), lambda bi, h, i: (bi, h, 0, i)),
            pl.BlockSpec((1, 1, SUBLANES, ATT_GROUP * Q_POS), lambda bi, h, i: (bi, h, 0, i)),
            pl.BlockSpec((1, SUBLANES, KV_WIDTH), lambda bi, h, i: (bi, 0, 0)),
            pl.BlockSpec((1, T, KV_WIDTH), lambda bi, h, i: (bi, 0, 0)),
            pl.BlockSpec((1, 1, VT_ROWS, T), lambda bi, h, i: (bi, h, 0, 0)),
        ],
        out_specs=pl.BlockSpec((1, Q_POS, ATT_GROUP * HEAD_DIM), lambda bi, h, i: (bi, i, h)),
        out_shape=jax.ShapeDtypeStruct((B, T, ATT_WIDTH), BF16),
        compiler_params=_params(("parallel", "parallel", "arbitrary")),
        name="attention",
    )(qt, qn, kn, k, vt)


def _mix_ffn_kernel(x_ref, yf_ref, yb_ref, g_ref, bg_ref, att_ref, lnw_ref, lnb_ref, hs_ref, wo_r_ref, wo_a_ref,
                    g2_ref, wg_ref, wu_ref, wd_ref, gf_ref, o_ref, h_scr, acc_scr):
    j = pl.program_id(1)

    @pl.when(j == 0)
    def _():
        y = yf_ref[...] + yb_ref[...]
        hs = hs_ref[...]
        mu = _dot_exact_rhs(y, hs) * (1.0 / HEAD_DIM)
        dy = y - mu
        var = _dot_exact_rhs(dy * dy, hs) * (1.0 / HEAD_DIM)
        yn = dy * lax.rsqrt(var + LNX_EPS) * lnw_ref[...] + lnb_ref[...]
        yr = yn * g_ref[...].astype(F32) + bg_ref[...].astype(F32)
        x = x_ref[...] + _dot(_bf(yr), wo_r_ref[...]) + _dot(att_ref[...], wo_a_ref[...])
        ms = jnp.mean(x * x, axis=-1, keepdims=True)
        h_scr[...] = _bf(x * lax.rsqrt(ms + NORM_EPS) * g2_ref[...])
        acc_scr[...] = x

    h = h_scr[...]
    gate = _dot(h, wg_ref[...])
    up = _dot(h, wu_ref[...])
    act = gate * _sigmoid(gate) * up
    acc_scr[...] += _dot(_bf(act), wd_ref[...])

    @pl.when(j == pl.num_programs(1) - 1)
    def _():
        xo = acc_scr[...]
        ms = jnp.mean(xo * xo, axis=-1, keepdims=True)
        o_ref[...] = xo * lax.rsqrt(ms + NORM_EPS) * gf_ref[...]


def _mix_ffn(x2, y_f, y_b, g, bg, att, p, tm, tf):
    m = x2.shape[0]
    full = lambda shape: pl.BlockSpec(shape, lambda i, j: tuple(0 for _ in shape))
    tok = pl.BlockSpec((tm, RWKV_WIDTH), lambda i, j: (i, 0))
    wide = pl.BlockSpec((tm, D_MODEL), lambda i, j: (i, 0))
    return pl.pallas_call(
        _mix_ffn_kernel,
        grid=(m // tm, D_FF // tf),
        in_specs=[
            wide, tok, tok, tok, tok, tok,
            full((1, RWKV_WIDTH)), full((1, RWKV_WIDTH)), full((RWKV_WIDTH, RWKV_WIDTH)),
            full((RWKV_WIDTH, D_MODEL)), full((ATT_WIDTH, D_MODEL)),
            full((1, D_MODEL)),
            pl.BlockSpec((D_MODEL, tf), lambda i, j: (0, j)),
            pl.BlockSpec((D_MODEL, tf), lambda i, j: (0, j)),
            pl.BlockSpec((tf, D_MODEL), lambda i, j: (j, 0)),
            full((1, D_MODEL)),
        ],
        out_specs=wide,
        out_shape=jax.ShapeDtypeStruct((m, D_MODEL), F32),
        scratch_shapes=[pltpu.VMEM((tm, D_MODEL), BF16), pltpu.VMEM((tm, D_MODEL), F32)],
        compiler_params=_params(("parallel", "arbitrary")),
        name="mix_ffn",
    )(x2, y_f, y_b, g, bg, att, p["lnx_w"], p["lnx_b"], p["head_ones"], p["wo_r"], p["wo_a"],
      p["norm2_g"], p["ffn_gate"], p["ffn_up"], p["ffn_down"], p["norm_f_g"])


def _rope_tables(T):
    n_rows = T // GRID_W
    t = jnp.arange(T, dtype=jnp.int32)
    row = (t // GRID_W).astype(F32)
    col = (t % GRID_W).astype(F32)
    inv = ROPE_THETA ** (-jnp.arange(ROPE_PAIRS, dtype=F32) / ROPE_PAIRS)
    ar = row[:, None] * inv
    ac = col[:, None] * inv
    cos = jnp.concatenate([jnp.cos(ar), jnp.cos(ar), jnp.cos(ac), jnp.cos(ac)], axis=1)
    sin = jnp.concatenate([-jnp.sin(ar), jnp.sin(ar), -jnp.sin(ac), jnp.sin(ac)], axis=1)
    del n_rows
    return jnp.tile(cos, (1, 2)), jnp.tile(sin, (1, 2))


def _block_diag2(a, b):
    za = jnp.zeros_like(a)
    return jnp.concatenate([jnp.concatenate([a, za], axis=1), jnp.concatenate([za, b], axis=1)], axis=0)


def _hi_lo(w):
    hi = w.astype(BF16)
    return hi, (w - hi.astype(F32)).astype(BF16)


def _prepare_params(norm1_g, w_in, mu_prev, mu_next, k_k, k_a, r_k, w0_f, w_lora_f, w0_b, w_lora_b,
                    a0_f, a_lora_f, a0_b, a_lora_b, g_lora, lnx_w, lnx_b, q_gain, k_gain, w_out,
                    norm2_g, ffn_gate, ffn_up, ffn_down, norm_f_g):
    l = 0
    p = {}
    p["norm1_g"] = norm1_g[l][None]
    p["w_r"] = w_in[l][:, :RWKV_COLS].astype(BF16)
    p["w_a"] = w_in[l][:, RWKV_COLS:].astype(BF16)
    p["mu_prev"] = mu_prev[l][None]
    p["mu_next"] = mu_next[l][None]
    p["k_k"] = k_k[l][None]
    p["k_a"] = k_a[l][None]
    p["r_k"] = r_k[l].reshape(1, RWKV_WIDTH)
    p["w0"] = jnp.concatenate([w0_f[l], w0_b[l]])[None]
    p["wl_hi"], p["wl_lo"] = _hi_lo(_block_diag2(w_lora_f[l], w_lora_b[l]))
    p["a0"] = jnp.concatenate([a0_f[l], a0_b[l]])[None]
    p["al_hi"], p["al_lo"] = _hi_lo(_block_diag2(a_lora_f[l], a_lora_b[l]))
    p["gl_hi"], p["gl_lo"] = _hi_lo(g_lora[l])
    p["lnx_w"] = lnx_w[l][None]
    p["lnx_b"] = lnx_b[l][None]
    scale = HEAD_DIM ** -0.5 * float(np.log2(np.e))
    p["qk_gain"] = jnp.concatenate([jnp.tile(q_gain[l] * scale, ATT_Q_HEADS), jnp.tile(k_gain[l], ATT_KV_HEADS)])[None]
    hid = np.arange(640) // HEAD_DIM
    ones = (hid[:, None] == hid[None, :]).astype(np.float32)
    p["head_ones_qk"] = jnp.asarray(ones, BF16)
    p["head_ones"] = jnp.asarray(ones[:512, :512], BF16)
    p["wo_r"] = w_out[l][:RWKV_WIDTH].astype(BF16)
    p["wo_a"] = w_out[l][RWKV_WIDTH:].astype(BF16)
    p["norm2_g"] = norm2_g[l][None]
    p["ffn_gate"] = ffn_gate[l].astype(BF16)
    p["ffn_up"] = ffn_up[l].astype(BF16)
    p["ffn_down"] = ffn_down[l].astype(BF16)
    p["norm_f_g"] = norm_f_g[None]
    return p


def _tile(n, pref):
    t = pref
    while n % t:
        t //= 2
    return t


def _trunk(x, p):
    B, T, D = x.shape
    m = B * T
    x2 = x.reshape(m, D)
    z_r, z_a = _inproj(x2, p["norm1_g"], p["w_r"], p["w_a"], _tile(m, 512))
    v, ash, rsh, bsh, ksh, rfull, bkt, lend, g, bg = _rwkv_prep(z_r, T, p, _tile(T, 256))
    y_f, y_b = _rwkv_scan(v, ash, rsh, bsh, ksh, rfull, bkt, lend, B, T)
    cos_t, sin_t = _rope_tables(T)
    qt, k, vt, qn, kn = _att_prep(z_a, B, T, p, cos_t, sin_t)
    att = _attention(qt, qn, kn, k, vt, B, T, _tile(T, 512)).reshape(m, ATT_WIDTH)
    out = _mix_ffn(x2, y_f, y_b, g, bg, att, p, _tile(m, 512), 1408)
    return out.reshape(B, T, D)


def kernel(x_prompt, x_sample, norm1_g, w_in, mu_prev, mu_next, k_k, k_a, r_k, w0_f, w_lora_f, w0_b, w_lora_b, a0_f, a_lora_f, a0_b, a_lora_b, g_lora, lnx_w, lnx_b, q_gain, k_gain, w_out, norm2_g, ffn_gate, ffn_up, ffn_down, norm_f_g):
    p = _prepare_params(norm1_g, w_in, mu_prev, mu_next, k_k, k_a, r_k, w0_f, w_lora_f, w0_b, w_lora_b,
                        a0_f, a_lora_f, a0_b, a_lora_b, g_lora, lnx_w, lnx_b, q_gain, k_gain, w_out,
                        norm2_g, ffn_gate, ffn_up, ffn_down, norm_f_g)
    return (_trunk(x_prompt, p), _trunk(x_sample, p))
```

```python
import functools

import jax
import jax.numpy as jnp
import numpy as np
from jax import lax
from jax.experimental import pallas as pl
from jax.experimental.pallas import tpu as pltpu

F32 = jnp.float32
BF16 = jnp.bfloat16

D_MODEL = 1024
HEAD_DIM = 64
RWKV_WIDTH = 512
RWKV_HEADS = 8
ATT_WIDTH = 512
ATT_Q_HEADS = 8
ATT_KV_HEADS = 2
ATT_GROUP = 4
KV_WIDTH = 128
RWKV_COLS = 1920
ATT_COLS = 768
D_FF = 2816
GRID_W = 64
ROPE_THETA = 10000.0
ROPE_PAIRS = 16
NORM_EPS = 1e-6
LNX_EPS = 64e-5

LANES = 128
SUBLANES = 8
CHUNK = 128
Q_POS = 256
KV_CHUNK = 2048
KV_UNROLL = 2
VT_ROWS = 80
DECAY_SCALE = float(np.exp(-0.5))
DENOM_FLOOR = 2.0 ** -100
VMEM_LIMIT = 56 * 1024 * 1024


def _dot(a, b):
    return jnp.dot(a, b, preferred_element_type=F32)


def _bf(x):
    return x.astype(BF16)


def _split3(x):
    hi = _bf(x)
    r1 = x - hi.astype(F32)
    mid = _bf(r1)
    lo = _bf(r1 - mid.astype(F32))
    return hi, mid, lo


def _head_sum(x, ones_bf):
    return _dot(_bf(x), ones_bf)


def _sigmoid(x):
    return 0.5 * jnp.tanh(0.5 * x) + 0.5


def _softplus(x):
    return jnp.maximum(x, 0.0) + jnp.log(1.0 + jnp.exp(-jnp.abs(x)))


def _params(sem):
    return pltpu.CompilerParams(dimension_semantics=sem, vmem_limit_bytes=VMEM_LIMIT)


def _inproj_kernel(x_ref, g_ref, wr_ref, wa_ref, zr_ref, za_ref):
    x = x_ref[...]
    ms = jnp.mean(x * x, axis=-1, keepdims=True)
    h = _bf(x * lax.rsqrt(ms + NORM_EPS) * g_ref[...])
    zr_ref[...] = _dot(h, wr_ref[...])
    za_ref[...] = _dot(h, wa_ref[...])


def _inproj(x2, norm1_g, w_r, w_a, tm):
    m = x2.shape[0]
    return pl.pallas_call(
        _inproj_kernel,
        grid=(m // tm,),
        in_specs=[
            pl.BlockSpec((tm, D_MODEL), lambda i: (i, 0)),
            pl.BlockSpec((1, D_MODEL), lambda i: (0, 0)),
            pl.BlockSpec((D_MODEL, RWKV_COLS), lambda i: (0, 0)),
            pl.BlockSpec((D_MODEL, ATT_COLS), lambda i: (0, 0)),
        ],
        out_specs=[
            pl.BlockSpec((tm, RWKV_COLS), lambda i: (i, 0)),
            pl.BlockSpec((tm, ATT_COLS), lambda i: (i, 0)),
        ],
        out_shape=[
            jax.ShapeDtypeStruct((m, RWKV_COLS), F32),
            jax.ShapeDtypeStruct((m, ATT_COLS), F32),
        ],
        compiler_params=_params(("parallel",)),
        name="inproj",
    )(x2, norm1_g, w_r, w_a)


def _rwkv_prep_kernel(z_ref, zp_ref, zn_ref, mup_ref, mun_ref, kk_ref, ka_ref, rk_ref,
                      w0_ref, wl_ref, a0_ref, al_ref, gl_ref,
                      hs_ref,
                      v_o, ash_o, rsh_o, bsh_o, ksh_o, rfull_o, bkt_o, lend_o, g_o, bg_o,
                      *, tm, blocks_per_seq):
    i = pl.program_id(0)
    pos = i % blocks_per_seq
    z = z_ref[...]
    prev_row = jnp.where(pos == 0, 0.0, zp_ref[SUBLANES - 1:SUBLANES, :])
    next_row = jnp.where(pos == blocks_per_seq - 1, 0.0, zn_ref[0:1, :])
    rows = lax.broadcasted_iota(jnp.int32, (SUBLANES, 1), 0)
    z_prev = pltpu.roll(z, 1, 0)
    z_prev = jnp.concatenate([jnp.where(rows == 0, prev_row, z_prev[:SUBLANES]), z_prev[SUBLANES:]], axis=0)
    z_next = pltpu.roll(z, tm - 1, 0)
    z_next = jnp.concatenate([z_next[:tm - SUBLANES],
                              jnp.where(rows == SUBLANES - 1, next_row, z_next[tm - SUBLANES:])], axis=0)
    zf = z + mup_ref[...] * (z_prev - z) + mun_ref[...] * (z_next - z)

    r = zf[:, 0:512]
    k = zf[:, 512:1024]
    v = zf[:, 1024:1536]
    wd = zf[:, 1536:1664]
    ad = zf[:, 1664:1792]
    gd = zf[:, 1792:1920]
    hs = hs_ref[...]

    kk = k * kk_ref[...]
    ss = _head_sum(kk * kk, hs)
    kk = kk * lax.rsqrt(jnp.maximum(ss, 1e-12))

    lw_both = w0_ref[...] + _dot(_bf(jnp.tanh(wd)), wl_ref[...])
    as_both = a0_ref[...] + _dot(_bf(ad), al_ref[...])
    ka = ka_ref[...]
    kb = jnp.zeros_like(k)
    ri = lax.broadcasted_iota(jnp.int32, (tm, tm), 0)
    ci = lax.broadcasted_iota(jnp.int32, (tm, tm), 1)
    same_chunk = (ri // CHUNK) == (ci // CHUNK)
    chunk_ones = same_chunk.astype(BF16)
    for d in range(2):
        lw = -DECAY_SCALE * _sigmoid(lw_both[:, 512 * d:512 * (d + 1)])
        a = _sigmoid(as_both[:, 512 * d:512 * (d + 1)])
        kd = k * (1.0 + (a - 1.0) * ka)
        b = kk * a
        kb = kb + kd
        processed = (ci <= ri) if d == 0 else (ci >= ri)
        tri = (same_chunk & processed).astype(BF16)
        hi, mid, lo = _split3(lw)
        cum = _dot(tri, hi) + _dot(tri, mid) + _dot(tri, lo)
        l_end = _dot(chunk_ones, hi) + _dot(chunk_ones, mid) + _dot(chunk_ones, lo)
        l_half = 0.5 * l_end
        e_half = jnp.exp(l_half)
        r_sh = r * jnp.exp(cum - l_half)
        ash_o[d] = _bf(-kk * jnp.exp(cum - lw - l_half))
        rsh_o[d] = _bf(r_sh)
        rfull_o[d] = _bf(r_sh * e_half)
        e_b = jnp.exp(l_half - cum)
        b_sh = b * e_b
        k_sh = kd * e_b
        bsh_o[d] = _bf(b_sh)
        ksh_o[d] = _bf(k_sh)
        b_hat = b_sh * e_half
        k_hat = k_sh * e_half
        for cc in range(tm // CHUNK):
            rs = slice(CHUNK * cc, CHUNK * (cc + 1))
            lend_o[d, cc] = l_end[CHUNK * cc:CHUNK * cc + SUBLANES]
            for p in range(RWKV_HEADS // 2):
                ls = slice(LANES * p, LANES * (p + 1))
                bkt_o[d, cc, ls, 0:CHUNK] = _bf(b_hat[rs, ls].T)
                bkt_o[d, cc, ls, CHUNK:2 * CHUNK] = _bf(k_hat[rs, ls].T)
    kb = 0.5 * kb
    coef = _head_sum(r * kb * rk_ref[...], hs)
    g = _dot(_bf(_sigmoid(gd)), gl_ref[...])
    v_o[...] = _bf(v)
    g_o[...] = _bf(g)
    bg_o[...] = _bf(coef * v * g)


def _rwkv_prep(z_r, T, p, tm):
    m = z_r.shape[0]
    bps = T // tm
    hb = tm // SUBLANES
    nhalo = m // SUBLANES
    full = lambda shape: pl.BlockSpec(shape, lambda i: tuple(0 for _ in shape))
    tok = pl.BlockSpec((tm, RWKV_WIDTH), lambda i: (i, 0))
    tok2 = pl.BlockSpec((2, tm, RWKV_WIDTH), lambda i: (0, i, 0))
    cpb = tm // CHUNK
    nchunk = m // CHUNK
    tok2_shape = jax.ShapeDtypeStruct((2, m, RWKV_WIDTH), BF16)
    kern = functools.partial(_rwkv_prep_kernel, tm=tm, blocks_per_seq=bps)
    return pl.pallas_call(
        kern,
        grid=(m // tm,),
        in_specs=[
            pl.BlockSpec((tm, RWKV_COLS), lambda i: (i, 0)),
            pl.BlockSpec((SUBLANES, RWKV_COLS), lambda i: (jnp.maximum(i * hb - 1, 0), 0)),
            pl.BlockSpec((SUBLANES, RWKV_COLS), lambda i: (jnp.minimum((i + 1) * hb, nhalo - 1), 0)),
            full((1, RWKV_COLS)), full((1, RWKV_COLS)),
            full((1, 512)), full((1, 512)), full((1, 512)),
            full((1, 1024)), full((128, 1024)),
            full((1, 1024)), full((128, 1024)),
            full((128, 512)),
            full((512, 512)),
        ],
        out_specs=[
            tok, tok2, tok2, tok2, tok2, tok2,
            pl.BlockSpec((2, cpb, RWKV_WIDTH, 2 * CHUNK), lambda i: (0, i, 0, 0)),
            pl.BlockSpec((2, cpb, SUBLANES, RWKV_WIDTH), lambda i: (0, i, 0, 0)),
            tok, tok,
        ],
        out_shape=[
            jax.ShapeDtypeStruct((m, RWKV_WIDTH), BF16),
            tok2_shape, tok2_shape, tok2_shape, tok2_shape, tok2_shape,
            jax.ShapeDtypeStruct((2, nchunk, RWKV_WIDTH, 2 * CHUNK), BF16),
            jax.ShapeDtypeStruct((2, nchunk, SUBLANES, RWKV_WIDTH), F32),
            jax.ShapeDtypeStruct((m, RWKV_WIDTH), BF16),
            jax.ShapeDtypeStruct((m, RWKV_WIDTH), BF16),
        ],
        compiler_params=_params(("parallel",)),
        name="rwkv_prep",
    )(z_r, z_r, z_r, p["mu_prev"], p["mu_next"], p["k_k"], p["k_a"], p["r_k"],
      p["w0"], p["w_lora"], p["a0"], p["a_lora"], p["g_lora"],
      p["head_ones"])


def _scan_kernel(*refs):
    C = CHUNK
    c = pl.program_id(1)
    z_scr = refs[-1]

    @pl.when(c == 0)
    def _():
        z_scr[...] = jnp.zeros_like(z_scr)

    row = lax.broadcasted_iota(jnp.int32, (C, C), 0)
    col = lax.broadcasted_iota(jnp.int32, (C, C), 1)
    strict_d = [col < row, col > row]
    incl_d = [col <= row, col >= row]
    eye = (row == col).astype(F32)
    lane = lax.broadcasted_iota(jnp.int32, (1, LANES), 1)
    m0 = lane < HEAD_DIM
    m1 = lane >= HEAD_DIM
    blockdiag = (row < HEAD_DIM) == (col < HEAD_DIM)

    def both_heads(x):
        zero = jnp.zeros((), x.dtype)
        return jnp.concatenate([jnp.where(m0, x, zero), jnp.where(m1, x, zero)], axis=0)

    ppd = RWKV_HEADS // 2
    pairs = range(2 * ppd)
    heads = range(2 * RWKV_HEADS)
    n_in = (len(refs) - 3) // 2
    v, a_sh, r_sh, b_sh, k_sh, r_full, bkt_refs, e_half, p_end, strict, incl = ([] for _ in range(11))
    for pp in pairs:
        d, sl = pp // ppd, slice(LANES * (pp % ppd), LANES * (pp % ppd + 1))
        v_ref, ash_ref, rsh_ref, bsh_ref, ksh_ref, rfull_ref, bkt_ref, lend_ref = refs[n_in * d:n_in * (d + 1)]
        v.append(v_ref[:, sl])
        a_sh.append(ash_ref[0, :, sl])
        r_sh.append(rsh_ref[0, :, sl])
        b_sh.append(bsh_ref[0, :, sl])
        k_sh.append(ksh_ref[0, :, sl])
        r_full.append(rfull_ref[0, :, sl])
        bkt_refs.append((bkt_ref, sl))
        l_end = lend_ref[0, 0, 0:1, sl]
        e_half.append(jnp.exp(0.5 * l_end))
        p_end.append(jnp.exp(l_end))
        strict.append(strict_d[d])
        incl.append(incl_d[d])
    y_refs = refs[2 * n_in:2 * n_in + 2]

    amat = []
    for p in pairs:
        ar = jnp.concatenate([a_sh[p], r_sh[p]], axis=0)
        bk = jnp.concatenate([b_sh[p], k_sh[p]], axis=0)
        amat.append(lax.dot_general(both_heads(ar), bk, (((1,), (1,)), ((), ())),
                                    preferred_element_type=F32))
    n_bf, a_ak, a_rb, a_rk, t_inv = [], [], [], [], []
    for hd in heads:
        ah = amat[hd // 2][2 * C * (hd % 2):2 * C * (hd % 2 + 1)]
        n = jnp.where(strict[hd // 2], ah[:C, :C], 0.0)
        n_bf.append(_bf(n))
        t_inv.append(eye + n)
        a_ak.append(jnp.where(strict[hd // 2], ah[:C, C:], 0.0))
        a_rb.append(jnp.where(incl[hd // 2], ah[C:, :C], 0.0))
        a_rk.append(jnp.where(incl[hd // 2], ah[C:, C:], 0.0))
    def pair_products(lhs, rhs):
        out = []
        for p in pairs:
            a, b = rhs[2 * p], rhs[2 * p + 1]
            zero = jnp.zeros_like(a)
            diag = jnp.concatenate([jnp.concatenate([a, zero], axis=1),
                                    jnp.concatenate([zero, b], axis=1)], axis=0)
            prod = _dot(jnp.concatenate(lhs[2 * p:2 * p + 2], axis=1), diag)
            out += [prod[:, :C], prod[:, C:]]
        return out

    pw = [_bf(x) for x in pair_products(n_bf, n_bf)]
    avy = []
    for p in pairs:
        akrk = jnp.concatenate([jnp.concatenate(a_ak[2 * p:2 * p + 2], axis=1),
                                jnp.concatenate(a_rk[2 * p:2 * p + 2], axis=1)], axis=0)
        avy.append(_dot(_bf(akrk), both_heads(v[p])))
    for _ in range(5):
        both = [_dot(pw[hd], jnp.concatenate([pw[hd], _bf(t_inv[hd])], axis=1)) for hd in heads]
        pw = [_bf(both[hd][:, :C]) for hd in heads]
        t_inv = [t_inv[hd] + both[hd][:, C:] for hd in heads]
    last = pair_products(pw, [_bf(t) for t in t_inv])
    t_inv = [t_inv[hd] + last[hd] for hd in heads]

    aw = []
    for p in pairs:
        rhs = jnp.concatenate([both_heads(a_sh[p]), both_heads(_bf(avy[p][:C]))], axis=1)
        aw.append(_dot(_bf(jnp.concatenate(t_inv[2 * p:2 * p + 2], axis=1)), rhs))

    zs = [z_scr[p] for p in pairs]
    xs = []
    for q in range(0, 2 * ppd, 2):
        a_bar = jnp.concatenate([aw[q][:, :LANES] * e_half[q], aw[q + 1][:, :LANES] * e_half[q + 1]], axis=1)
        zero = jnp.zeros((LANES, LANES), BF16)
        z_diag = jnp.concatenate([jnp.concatenate([_bf(zs[q]), zero], axis=1),
                                  jnp.concatenate([zero, _bf(zs[q + 1])], axis=1)], axis=0)
        r_wide = jnp.concatenate([r_full[q], r_full[q + 1]], axis=1)
        x2 = _dot(jnp.concatenate([_bf(a_bar), r_wide], axis=0), z_diag)
        xs += [x2[:, :LANES], x2[:, LANES:]]
    us = [xs[p][:C] + aw[p][:, LANES:] for p in pairs]
    for p in pairs:
        y = xs[p][C:] + avy[p][C:] + _dot(_bf(jnp.concatenate(a_rb[2 * p:2 * p + 2], axis=1)),
                                           _bf(both_heads(us[p])))
        bkt_ref, sl = bkt_refs[p]
        y_refs[p // ppd][:, sl] = y
    for p in pairs:
        bkt_ref, sl = bkt_refs[p]
        bkt = bkt_ref[0, 0, sl, :]
        uv = jnp.concatenate([_bf(us[p]), v[p]], axis=0)
        pend_col = jnp.broadcast_to(p_end[p], (LANES, LANES)).T
        z_new = zs[p] * pend_col + _dot(bkt, uv)
        z_scr[p] = jnp.where(blockdiag, z_new, 0.0)


def _rwkv_scan(v, ash, rsh, bsh, ksh, rfull, bkt, lend, B, T):
    m = v.shape[0]
    nc = T // CHUNK

    def specs(d):
        def blk(bi, c):
            return bi * nc + c + d * (nc - 1 - 2 * c)

        tok = pl.BlockSpec((CHUNK, RWKV_WIDTH), lambda bi, c: (blk(bi, c), 0))
        tok2 = pl.BlockSpec((1, CHUNK, RWKV_WIDTH), lambda bi, c: (d, blk(bi, c), 0))
        return tok, [
            tok, tok2, tok2, tok2, tok2, tok2,
            pl.BlockSpec((1, 1, RWKV_WIDTH, 2 * CHUNK), lambda bi, c: (d, blk(bi, c), 0, 0)),
            pl.BlockSpec((1, 1, SUBLANES, RWKV_WIDTH), lambda bi, c: (d, blk(bi, c), 0, 0)),
        ]

    (out_f, in_f), (out_b, in_b) = specs(0), specs(1)
    operands = (v, ash, rsh, bsh, ksh, rfull, bkt, lend)
    y_shape = jax.ShapeDtypeStruct((m, RWKV_WIDTH), F32)
    return pl.pallas_call(
        _scan_kernel,
        grid=(B, nc),
        in_specs=in_f + in_b,
        out_specs=[out_f, out_b],
        out_shape=[y_shape, y_shape],
        scratch_shapes=[pltpu.VMEM((RWKV_HEADS, LANES, LANES), F32)],
        compiler_params=_params(("parallel", "arbitrary")),
        name="rwkv_scan",
    )(*operands, *operands)


def _att_prep_kernel(z_ref, gain_ref, cos_ref, sin_ref, hs_ref, qt_o, k_o, vt_o, qn_o, kn_o):
    i = pl.program_id(1)
    z = z_ref[...]
    qk = z[:, :640]
    ss = _head_sum(qk * qk, hs_ref[...])
    qk = qk * lax.rsqrt(ss * (1.0 / HEAD_DIM) + NORM_EPS) * gain_ref[...]
    width = qk.shape[1]
    lane = lax.broadcasted_iota(jnp.int32, (1, width), 1)
    first = (lane % (2 * ROPE_PAIRS)) < ROPE_PAIRS
    partner = jnp.where(first, pltpu.roll(qk, width - ROPE_PAIRS, 1), pltpu.roll(qk, ROPE_PAIRS, 1))
    cos = jnp.concatenate([cos_ref[...]] * 5, axis=1)
    sin = jnp.concatenate([sin_ref[...]] * 5, axis=1)
    qk = qk * cos + partner * sin
    for j in range(4):
        st = _bf(qk[:, LANES * j:LANES * (j + 1)].T)
        h = j // 2
        g = (2 * j) % ATT_GROUP
        qt_o[0, h, :, Q_POS * g:Q_POS * (g + 1)] = st[:HEAD_DIM]
        qt_o[0, h, :, Q_POS * (g + 1):Q_POS * (g + 2)] = st[HEAD_DIM:]
        sq = st.astype(F32)
        sq = sq * sq
        for e in range(2):
            nrm = jnp.sum(sq[HEAD_DIM * e:HEAD_DIM * (e + 1)], axis=0, keepdims=True)
            qn_o[0, h, :, Q_POS * (g + e):Q_POS * (g + e + 1)] = jnp.broadcast_to(nrm, (SUBLANES, Q_POS))
    kb = _bf(qk[:, 512:640])
    k_o[0] = kb
    kf = kb.astype(F32)
    kn = jnp.max(_head_sum(kf * kf, hs_ref[512:640, 512:640]), axis=0, keepdims=True)
    kn = jnp.broadcast_to(kn, (SUBLANES, KV_WIDTH))

    @pl.when(i == 0)
    def _():
        kn_o[0] = kn

    @pl.when(i > 0)
    def _():
        kn_o[0] = jnp.maximum(kn_o[0], kn)

    vt = _bf(z[:, 640:768].T)
    ones = jnp.ones((VT_ROWS - HEAD_DIM, Q_POS), BF16)
    for h in range(ATT_KV_HEADS):
        vt_o[0, h, :HEAD_DIM, :] = vt[HEAD_DIM * h:HEAD_DIM * (h + 1)]
        vt_o[0, h, HEAD_DIM:, :] = ones


def _att_prep(z_a, B, T, p, cos_t, sin_t):
    nb = T // Q_POS
    full = lambda shape: pl.BlockSpec(shape, lambda bi, i: tuple(0 for _ in shape))
    return pl.pallas_call(
        _att_prep_kernel,
        grid=(B, nb),
        in_specs=[
            pl.BlockSpec((Q_POS, ATT_COLS), lambda bi, i: (bi * nb + i, 0)),
            full((1, 640)),
            pl.BlockSpec((Q_POS, LANES), lambda bi, i: (i, 0)),
            pl.BlockSpec((Q_POS, LANES), lambda bi, i: (i, 0)),
            full((640, 640)),
        ],
        out_specs=[
            pl.BlockSpec((1, ATT_KV_HEADS, HEAD_DIM, ATT_GROUP * Q_POS), lambda bi, i: (bi, 0, 0, i)),
            pl.BlockSpec((1, Q_POS, KV_WIDTH), lambda bi, i: (bi, i, 0)),
            pl.BlockSpec((1, ATT_KV_HEADS, VT_ROWS, Q_POS), lambda bi, i: (bi, 0, 0, i)),
            pl.BlockSpec((1, ATT_KV_HEADS, SUBLANES, ATT_GROUP * Q_POS), lambda bi, i: (bi, 0, 0, i)),
            pl.BlockSpec((1, SUBLANES, KV_WIDTH), lambda bi, i: (bi, 0, 0)),
        ],
        out_shape=[
            jax.ShapeDtypeStruct((B, ATT_KV_HEADS, HEAD_DIM, ATT_GROUP * T), BF16),
            jax.ShapeDtypeStruct((B, T, KV_WIDTH), BF16),
            jax.ShapeDtypeStruct((B, ATT_KV_HEADS, VT_ROWS, T), BF16),
            jax.ShapeDtypeStruct((B, ATT_KV_HEADS, SUBLANES, ATT_GROUP * T), F32),
            jax.ShapeDtypeStruct((B, SUBLANES, KV_WIDTH), F32),
        ],
        compiler_params=_params(("parallel", "arbitrary")),
        name="att_prep",
    )(z_a, p["qk_gain"], cos_t, sin_t, p["head_ones_qk"])


def _attn_kernel(qt_ref, qn_ref, kn_ref, k_ref, vt_ref, o_ref, *, n_kv, tkv, unroll):
    h = pl.program_id(1)
    ncol = ATT_GROUP * Q_POS
    qt = qt_ref[0, 0]
    rowh = lax.broadcasted_iota(jnp.int32, (KV_WIDTH, 1), 0) // HEAD_DIM
    q2 = jnp.where(rowh == h, jnp.concatenate([qt, qt], axis=0), jnp.zeros((), BF16))
    laneh = lax.broadcasted_iota(jnp.int32, (1, KV_WIDTH), 1) // HEAD_DIM
    kn = jnp.max(jnp.where(laneh == h, kn_ref[0, 0:1, :], 0.0), axis=1, keepdims=True)
    shift = jnp.sqrt(qn_ref[0, 0, 0:1, :] * kn)

    def chunk_start(j):
        return j * tkv if isinstance(j, int) else pl.multiple_of(j * tkv, tkv)

    def scores(j):
        kc = k_ref[0, pl.ds(chunk_start(j), tkv), :]
        return _dot(kc, q2)

    def pv(j, pt):
        vc = vt_ref[0, 0, :, pl.ds(chunk_start(j), tkv)]
        return _dot(vc, pt)

    def sweep(step, carry):
        def group(base, carry, s, final):
            for u in range(unroll):
                s_next = None if (final and u == unroll - 1) else scores(base + u + 1)
                carry = step(base + u, carry, s)
                s = s_next
            return carry, s

        s = scores(0)
        n_groups = n_kv // unroll
        if n_groups > 1:
            carry, s = lax.fori_loop(0, n_groups - 1, lambda j, c: group(j * unroll, *c, False), (carry, s))
        carry, _ = group((n_groups - 1) * unroll, carry, s, True)
        return carry

    def emit(acc):
        o = acc[:HEAD_DIM] / acc[HEAD_DIM:HEAD_DIM + 1]
        ot = jnp.concatenate([o, jnp.zeros_like(o)], axis=0).T
        for g in range(ATT_GROUP):
            o_ref[0, :, HEAD_DIM * g:HEAD_DIM * (g + 1)] = _bf(ot[Q_POS * g:Q_POS * (g + 1), :HEAD_DIM])

    def fast_group(base, acc):
        for u in range(unroll):
            acc = acc + pv(base + u, _bf(jnp.exp2(scores(base + u) - shift)))
        return acc

    acc0 = jnp.zeros((VT_ROWS, ncol), F32)
    n_groups = n_kv // unroll
    if n_groups > 1:
        acc = lax.fori_loop(0, n_groups, lambda j, acc: fast_group(j * unroll, acc), acc0)
    else:
        acc = fast_group(0, acc0)
    emit(acc)
    denom_ok = jnp.min(acc[HEAD_DIM:HEAD_DIM + 1]) >= DENOM_FLOOR

    @pl.when(jnp.logical_not(denom_ok))
    def _():
        def step(j, carry, s):
            m, acc = carry
            m_new = jnp.maximum(m, jnp.max(s, axis=0, keepdims=True))
            return m_new, acc * jnp.exp2(m - m_new) + pv(j, _bf(jnp.exp2(s - m_new)))

        _, acc_online = sweep(step, (jnp.full((1, ncol), -jnp.inf, F32), acc0))
        emit(acc_online)


def _attention(qt, qn, kn, k, vt, B, T, tkv):
    nb = T // Q_POS
    n_kv = T // tkv
    kern = functools.partial(_attn_kernel, n_kv=n_kv, tkv=tkv, unroll=_tile(n_kv, KV_UNROLL))
    return pl.pallas_call(
        kern,
        grid=(B, ATT_KV_HEADS, nb),
        in_specs=[
            pl.BlockSpec((1, 1, HEAD_DIM, ATT_GROUP * Q_POS), lambda bi, h, i: (bi, h, 0, i)),
            pl.BlockSpec((1, 1, SUBLANES, ATT_GROUP * Q_POS), lambda bi, h, i: (bi, h, 0, i)),
            pl.BlockSpec((1, SUBLANES, KV_WIDTH), lambda bi, h, i: (bi, 0, 0)),
            pl.BlockSpec((1, T, KV_WIDTH), lambda bi, h, i: (bi, 0, 0)),
            pl.BlockSpec((1, 1, VT_ROWS, T), lambda bi, h, i: (bi, h, 0, 0)),
        ],
        out_specs=pl.BlockSpec((1, Q_POS, ATT_GROUP * HEAD_DIM), lambda bi, h, i: (bi, i, h)),
        out_shape=jax.ShapeDtypeStruct((B, T, ATT_WIDTH), BF16),
        compiler_params=_params(("parallel", "parallel", "arbitrary")),
        name="attention",
    )(qt, qn, kn, k, vt)


def _mix_ffn_kernel(x_ref, yf_ref, yb_ref, g_ref, bg_ref, att_ref, lnw_ref, lnb_ref, hs_ref, wo_r_ref, wo_a_ref,
                    g2_ref, wg_ref, wu_ref, wd_ref, gf_ref, o_ref, h_scr, acc_scr):
    j = pl.program_id(1)

    @pl.when(j == 0)
    def _():
        y = yf_ref[...] + yb_ref[...]
        hs = hs_ref[...]
        mu = _head_sum(y, hs) * (1.0 / HEAD_DIM)
        dy = y - mu
        var = _head_sum(dy * dy, hs) * (1.0 / HEAD_DIM)
        yn = dy * lax.rsqrt(var + LNX_EPS) * lnw_ref[...] + lnb_ref[...]
        yr = yn * g_ref[...].astype(F32) + bg_ref[...].astype(F32)
        x = x_ref[...] + _dot(_bf(yr), wo_r_ref[...]) + _dot(att_ref[...], wo_a_ref[...])
        ms = jnp.mean(x * x, axis=-1, keepdims=True)
        h_scr[...] = _bf(x * lax.rsqrt(ms + NORM_EPS) * g2_ref[...])
        acc_scr[...] = x

    h = h_scr[...]
    gate = _dot(h, wg_ref[...])
    up = _dot(h, wu_ref[...])
    act = gate * _sigmoid(gate) * up
    acc_scr[...] += _dot(_bf(act), wd_ref[...])

    @pl.when(j == pl.num_programs(1) - 1)
    def _():
        xo = acc_scr[...]
        ms = jnp.mean(xo * xo, axis=-1, keepdims=True)
        o_ref[...] = xo * lax.rsqrt(ms + NORM_EPS) * gf_ref[...]


def _mix_ffn(x2, y_f, y_b, g, bg, att, p, tm, tf):
    m = x2.shape[0]
    full = lambda shape: pl.BlockSpec(shape, lambda i, j: tuple(0 for _ in shape))
    tok = pl.BlockSpec((tm, RWKV_WIDTH), lambda i, j: (i, 0))
    wide = pl.BlockSpec((tm, D_MODEL), lambda i, j: (i, 0))
    return pl.pallas_call(
        _mix_ffn_kernel,
        grid=(m // tm, D_FF // tf),
        in_specs=[
            wide, tok, tok, tok, tok, tok,
            full((1, RWKV_WIDTH)), full((1, RWKV_WIDTH)), full((RWKV_WIDTH, RWKV_WIDTH)),
            full((RWKV_WIDTH, D_MODEL)), full((ATT_WIDTH, D_MODEL)),
            full((1, D_MODEL)),
            pl.BlockSpec((D_MODEL, tf), lambda i, j: (0, j)),
            pl.BlockSpec((D_MODEL, tf), lambda i, j: (0, j)),
            pl.BlockSpec((tf, D_MODEL), lambda i, j: (j, 0)),
            full((1, D_MODEL)),
        ],
        out_specs=wide,
        out_shape=jax.ShapeDtypeStruct((m, D_MODEL), F32),
        scratch_shapes=[pltpu.VMEM((tm, D_MODEL), BF16), pltpu.VMEM((tm, D_MODEL), F32)],
        compiler_params=_params(("parallel", "arbitrary")),
        name="mix_ffn",
    )(x2, y_f, y_b, g, bg, att, p["lnx_w"], p["lnx_b"], p["head_ones"], p["wo_r"], p["wo_a"],
      p["norm2_g"], p["ffn_gate"], p["ffn_up"], p["ffn_down"], p["norm_f_g"])


def _rope_tables(T):
    n_rows = T // GRID_W
    t = jnp.arange(T, dtype=jnp.int32)
    row = (t // GRID_W).astype(F32)
    col = (t % GRID_W).astype(F32)
    inv = ROPE_THETA ** (-jnp.arange(ROPE_PAIRS, dtype=F32) / ROPE_PAIRS)
    ar = row[:, None] * inv
    ac = col[:, None] * inv
    cos = jnp.concatenate([jnp.cos(ar), jnp.cos(ar), jnp.cos(ac), jnp.cos(ac)], axis=1)
    sin = jnp.concatenate([-jnp.sin(ar), jnp.sin(ar), -jnp.sin(ac), jnp.sin(ac)], axis=1)
    del n_rows
    return jnp.tile(cos, (1, 2)), jnp.tile(sin, (1, 2))


def _block_diag2(a, b):
    za = jnp.zeros_like(a)
    return jnp.concatenate([jnp.concatenate([a, za], axis=1), jnp.concatenate([za, b], axis=1)], axis=0)


def _prepare_params(norm1_g, w_in, mu_prev, mu_next, k_k, k_a, r_k, w0_f, w_lora_f, w0_b, w_lora_b,
                    a0_f, a_lora_f, a0_b, a_lora_b, g_lora, lnx_w, lnx_b, q_gain, k_gain, w_out,
                    norm2_g, ffn_gate, ffn_up, ffn_down, norm_f_g):
    l = 0
    p = {}
    p["norm1_g"] = norm1_g[l][None]
    p["w_r"] = w_in[l][:, :RWKV_COLS].astype(BF16)
    p["w_a"] = w_in[l][:, RWKV_COLS:].astype(BF16)
    p["mu_prev"] = mu_prev[l][None]
    p["mu_next"] = mu_next[l][None]
    p["k_k"] = k_k[l][None]
    p["k_a"] = k_a[l][None]
    p["r_k"] = r_k[l].reshape(1, RWKV_WIDTH)
    p["w0"] = jnp.concatenate([w0_f[l], w0_b[l]])[None]
    p["w_lora"] = _block_diag2(w_lora_f[l], w_lora_b[l]).astype(BF16)
    p["a0"] = jnp.concatenate([a0_f[l], a0_b[l]])[None]
    p["a_lora"] = _block_diag2(a_lora_f[l], a_lora_b[l]).astype(BF16)
    p["g_lora"] = g_lora[l].astype(BF16)
    p["lnx_w"] = lnx_w[l][None]
    p["lnx_b"] = lnx_b[l][None]
    scale = HEAD_DIM ** -0.5 * float(np.log2(np.e))
    p["qk_gain"] = jnp.concatenate([jnp.tile(q_gain[l] * scale, ATT_Q_HEADS), jnp.tile(k_gain[l], ATT_KV_HEADS)])[None]
    hid = np.arange(640) // HEAD_DIM
    ones = (hid[:, None] == hid[None, :]).astype(np.float32)
    p["head_ones_qk"] = jnp.asarray(ones, BF16)
    p["head_ones"] = jnp.asarray(ones[:512, :512], BF16)
    p["wo_r"] = w_out[l][:RWKV_WIDTH].astype(BF16)
    p["wo_a"] = w_out[l][RWKV_WIDTH:].astype(BF16)
    p["norm2_g"] = norm2_g[l][None]
    p["ffn_gate"] = ffn_gate[l].astype(BF16)
    p["ffn_up"] = ffn_up[l].astype(BF16)
    p["ffn_down"] = ffn_down[l].astype(BF16)
    p["norm_f_g"] = norm_f_g[None]
    return p


def _tile(n, pref):
    t = pref
    while n % t:
        t //= 2
    return t


def _trunk(x, p):
    B, T, D = x.shape
    m = B * T
    x2 = x.reshape(m, D)
    z_r, z_a = _inproj(x2, p["norm1_g"], p["w_r"], p["w_a"], _tile(m, 512))
    v, ash, rsh, bsh, ksh, rfull, bkt, lend, g, bg = _rwkv_prep(z_r, T, p, _tile(T, 256))
    y_f, y_b = _rwkv_scan(v, ash, rsh, bsh, ksh, rfull, bkt, lend, B, T)
    cos_t, sin_t = _rope_tables(T)
    qt, k, vt, qn, kn = _att_prep(z_a, B, T, p, cos_t, sin_t)
    att = _attention(qt, qn, kn, k, vt, B, T, _tile(T, KV_CHUNK)).reshape(m, ATT_WIDTH)
    out = _mix_ffn(x2, y_f, y_b, g, bg, att, p, _tile(m, 512), 1408)
    return out.reshape(B, T, D)


def kernel(x_prompt, x_sample, norm1_g, w_in, mu_prev, mu_next, k_k, k_a, r_k, w0_f, w_lora_f, w0_b, w_lora_b, a0_f, a_lora_f, a0_b, a_lora_b, g_lora, lnx_w, lnx_b, q_gain, k_gain, w_out, norm2_g, ffn_gate, ffn_up, ffn_down, norm_f_g):
    p = _prepare_params(norm1_g, w_in, mu_prev, mu_next, k_k, k_a, r_k, w0_f, w_lora_f, w0_b, w_lora_b,
                        a0_f, a_lora_f, a0_b, a_lora_b, g_lora, lnx_w, lnx_b, q_gain, k_gain, w_out,
                        norm2_g, ffn_gate, ffn_up, ffn_down, norm_f_g)
    return (_trunk(x_prompt, p), _trunk(x_sample, p))
```

```python
import functools

import jax
import jax.numpy as jnp
import numpy as np
from jax import lax
from jax.experimental import pallas as pl
from jax.experimental.pallas import tpu as pltpu

F32 = jnp.float32
BF16 = jnp.bfloat16

D_MODEL = 1024
HEAD_DIM = 64
RWKV_WIDTH = 512
RWKV_HEADS = 8
ATT_WIDTH = 512
ATT_Q_HEADS = 8
ATT_KV_HEADS = 2
ATT_GROUP = 4
KV_WIDTH = 128
RWKV_COLS = 1920
ATT_COLS = 768
D_FF = 2816
GRID_W = 64
ROPE_THETA = 10000.0
ROPE_PAIRS = 16
NORM_EPS = 1e-6
LNX_EPS = 64e-5

LANES = 128
SUBLANES = 8
CHUNK = 128
Q_POS = 256
KV_CHUNK = 2048
KV_UNROLL = 2
VT_ROWS = 80
DECAY_SCALE_LOG2 = float(np.exp(-0.5) * np.log2(np.e))
DENOM_FLOOR = 2.0 ** -100
VMEM_LIMIT = 56 * 1024 * 1024


def _dot(a, b):
    return jnp.dot(a, b, preferred_element_type=F32)


def _bf(x):
    return x.astype(BF16)


def _split3(x):
    hi = _bf(x)
    r1 = x - hi.astype(F32)
    mid = _bf(r1)
    lo = _bf(r1 - mid.astype(F32))
    return hi, mid, lo


def _head_sum(x, ones_bf):
    return _dot(_bf(x), ones_bf)


def _sigmoid(x):
    return 0.5 * jnp.tanh(0.5 * x) + 0.5


def _softplus(x):
    return jnp.maximum(x, 0.0) + jnp.log(1.0 + jnp.exp(-jnp.abs(x)))


def _params(sem):
    return pltpu.CompilerParams(dimension_semantics=sem, vmem_limit_bytes=VMEM_LIMIT)


def _inproj_kernel(x_ref, g_ref, wr_ref, wa_ref, zr_ref, za_ref):
    x = x_ref[...]
    ms = jnp.mean(x * x, axis=-1, keepdims=True)
    h = _bf(x * lax.rsqrt(ms + NORM_EPS) * g_ref[...])
    zr_ref[...] = _dot(h, wr_ref[...])
    za_ref[...] = _dot(h, wa_ref[...])


def _inproj(x2, norm1_g, w_r, w_a, tm):
    m = x2.shape[0]
    return pl.pallas_call(
        _inproj_kernel,
        grid=(m // tm,),
        in_specs=[
            pl.BlockSpec((tm, D_MODEL), lambda i: (i, 0)),
            pl.BlockSpec((1, D_MODEL), lambda i: (0, 0)),
            pl.BlockSpec((D_MODEL, RWKV_COLS), lambda i: (0, 0)),
            pl.BlockSpec((D_MODEL, ATT_COLS), lambda i: (0, 0)),
        ],
        out_specs=[
            pl.BlockSpec((tm, RWKV_COLS), lambda i: (i, 0)),
            pl.BlockSpec((tm, ATT_COLS), lambda i: (i, 0)),
        ],
        out_shape=[
            jax.ShapeDtypeStruct((m, RWKV_COLS), F32),
            jax.ShapeDtypeStruct((m, ATT_COLS), F32),
        ],
        compiler_params=_params(("parallel",)),
        name="inproj",
    )(x2, norm1_g, w_r, w_a)


def _rwkv_prep_kernel(z_ref, zp_ref, zn_ref, mup_ref, mun_ref, kk_ref, ka_ref, rk_ref,
                      w0_ref, wl_ref, a0_ref, al_ref, gl_ref,
                      hs_ref,
                      v_o, ash_o, rsh_o, bsh_o, ksh_o, rfull_o, bkt_o, lend_o, g_o, bg_o,
                      *, tm, blocks_per_seq):
    i = pl.program_id(0)
    pos = i % blocks_per_seq
    z = z_ref[...]
    mup, mun = mup_ref[...], mun_ref[...]
    ri = lax.broadcasted_iota(jnp.int32, (tm, tm), 0)
    ci = lax.broadcasted_iota(jnp.int32, (tm, tm), 1)
    shifts = jnp.concatenate([(ci == ri - 1).astype(BF16), (ci == ri + 1).astype(BF16)], axis=1)
    zf = z * (1.0 - mup - mun) + _dot(shifts, jnp.concatenate([_bf(z * mup), _bf(z * mun)], axis=0))
    prev_row = jnp.where(pos == 0, 0.0, zp_ref[SUBLANES - 1:SUBLANES, :]) * mup
    next_row = jnp.where(pos == blocks_per_seq - 1, 0.0, zn_ref[0:1, :]) * mun
    rows = lax.broadcasted_iota(jnp.int32, (SUBLANES, 1), 0)
    zf = jnp.concatenate([zf[:SUBLANES] + jnp.where(rows == 0, prev_row, 0.0),
                          zf[SUBLANES:tm - SUBLANES],
                          zf[tm - SUBLANES:] + jnp.where(rows == SUBLANES - 1, next_row, 0.0)], axis=0)

    r = zf[:, 0:512]
    k = zf[:, 512:1024]
    v = zf[:, 1024:1536]
    wd = zf[:, 1536:1664]
    ad = zf[:, 1664:1792]
    gd = zf[:, 1792:1920]
    hs = hs_ref[...]

    kk = k * kk_ref[...]
    ss = _head_sum(kk * kk, hs)
    kk = kk * lax.rsqrt(jnp.maximum(ss, 1e-12))

    lw_both = w0_ref[...] + _dot(_bf(jnp.tanh(wd)), wl_ref[...])
    as_both = a0_ref[...] + _dot(_bf(ad), al_ref[...])
    ka_half = 0.5 * ka_ref[...]
    same_chunk = (ri // CHUNK) == (ci // CHUNK)
    n_chunks = tm // CHUNK
    kd_sum = jnp.zeros_like(k)
    for d in range(2):
        lw = -DECAY_SCALE_LOG2 * _sigmoid(lw_both[:, 512 * d:512 * (d + 1)])
        th = jnp.tanh(0.5 * as_both[:, 512 * d:512 * (d + 1)])
        kd = k * (1.0 + (th - 1.0) * ka_half)
        b = kk * (0.5 * th + 0.5)
        kd_sum = kd_sum + kd
        processed = (ci <= ri) if d == 0 else (ci >= ri)
        tri = (same_chunk & processed).astype(BF16)
        hi, mid, lo = _split3(lw)
        cum = _dot(tri, hi) + _dot(tri, mid) + _dot(tri, lo)
        last = CHUNK - 1 if d == 0 else 0
        l_end = [cum[CHUNK * cc + last:CHUNK * cc + last + 1] for cc in range(n_chunks)]
        for cc in range(n_chunks):
            lend_o[d, cc] = jnp.broadcast_to(l_end[cc], (SUBLANES, RWKV_WIDTH))
        rows = lambda vals: jnp.concatenate([jnp.broadcast_to(x, (CHUNK, RWKV_WIDTH)) for x in vals], axis=0)
        l_half = rows([0.5 * x for x in l_end])
        e_half = rows([jnp.exp2(0.5 * x) for x in l_end])
        r_sh = r * jnp.exp2(cum - l_half)
        ash_o[d] = _bf(-kk * jnp.exp2(cum - lw - l_half))
        rsh_o[d] = _bf(r_sh)
        rfull_o[d] = _bf(r_sh * e_half)
        e_b = jnp.exp2(l_half - cum)
        b_sh = b * e_b
        k_sh = kd * e_b
        bsh_o[d] = _bf(b_sh)
        ksh_o[d] = _bf(k_sh)
        b_hat = b_sh * e_half
        k_hat = k_sh * e_half
        for cc in range(n_chunks):
            rs = slice(CHUNK * cc, CHUNK * (cc + 1))
            for p in range(RWKV_HEADS // 2):
                ls = slice(LANES * p, LANES * (p + 1))
                bkt_o[d, cc, ls, 0:CHUNK] = _bf(b_hat[rs, ls].T)
                bkt_o[d, cc, ls, CHUNK:2 * CHUNK] = _bf(k_hat[rs, ls].T)
    coef = _head_sum(r * kd_sum * (0.5 * rk_ref[...]), hs)
    g = _dot(_bf(_sigmoid(gd)), gl_ref[...])
    v_o[...] = _bf(v)
    g_o[...] = _bf(g)
    bg_o[...] = _bf(coef * v * g)


def _rwkv_prep(z_r, T, p, tm):
    m = z_r.shape[0]
    bps = T // tm
    hb = tm // SUBLANES
    nhalo = m // SUBLANES
    full = lambda shape: pl.BlockSpec(shape, lambda i: tuple(0 for _ in shape))
    tok = pl.BlockSpec((tm, RWKV_WIDTH), lambda i: (i, 0))
    tok2 = pl.BlockSpec((2, tm, RWKV_WIDTH), lambda i: (0, i, 0))
    cpb = tm // CHUNK
    nchunk = m // CHUNK
    tok2_shape = jax.ShapeDtypeStruct((2, m, RWKV_WIDTH), BF16)
    kern = functools.partial(_rwkv_prep_kernel, tm=tm, blocks_per_seq=bps)
    return pl.pallas_call(
        kern,
        grid=(m // tm,),
        in_specs=[
            pl.BlockSpec((tm, RWKV_COLS), lambda i: (i, 0)),
            pl.BlockSpec((SUBLANES, RWKV_COLS), lambda i: (jnp.maximum(i * hb - 1, 0), 0)),
            pl.BlockSpec((SUBLANES, RWKV_COLS), lambda i: (jnp.minimum((i + 1) * hb, nhalo - 1), 0)),
            full((1, RWKV_COLS)), full((1, RWKV_COLS)),
            full((1, 512)), full((1, 512)), full((1, 512)),
            full((1, 1024)), full((128, 1024)),
            full((1, 1024)), full((128, 1024)),
            full((128, 512)),
            full((512, 512)),
        ],
        out_specs=[
            tok, tok2, tok2, tok2, tok2, tok2,
            pl.BlockSpec((2, cpb, RWKV_WIDTH, 2 * CHUNK), lambda i: (0, i, 0, 0)),
            pl.BlockSpec((2, cpb, SUBLANES, RWKV_WIDTH), lambda i: (0, i, 0, 0)),
            tok, tok,
        ],
        out_shape=[
            jax.ShapeDtypeStruct((m, RWKV_WIDTH), BF16),
            tok2_shape, tok2_shape, tok2_shape, tok2_shape, tok2_shape,
            jax.ShapeDtypeStruct((2, nchunk, RWKV_WIDTH, 2 * CHUNK), BF16),
            jax.ShapeDtypeStruct((2, nchunk, SUBLANES, RWKV_WIDTH), F32),
            jax.ShapeDtypeStruct((m, RWKV_WIDTH), BF16),
            jax.ShapeDtypeStruct((m, RWKV_WIDTH), BF16),
        ],
        compiler_params=_params(("parallel",)),
        name="rwkv_prep",
    )(z_r, z_r, z_r, p["mu_prev"], p["mu_next"], p["k_k"], p["k_a"], p["r_k"],
      p["w0"], p["w_lora"], p["a0"], p["a_lora"], p["g_lora"],
      p["head_ones"])


def _scan_kernel(*refs):
    C = CHUNK
    c = pl.program_id(1)
    z_scr = refs[-1]

    @pl.when(c == 0)
    def _():
        z_scr[...] = jnp.zeros_like(z_scr)

    row = lax.broadcasted_iota(jnp.int32, (C, C), 0)
    col = lax.broadcasted_iota(jnp.int32, (C, C), 1)
    strict_d = [col < row, col > row]
    incl_d = [col <= row, col >= row]
    eye = (row == col).astype(F32)
    lane = lax.broadcasted_iota(jnp.int32, (1, LANES), 1)
    m0 = lane < HEAD_DIM
    m1 = lane >= HEAD_DIM
    blockdiag = (row < HEAD_DIM) == (col < HEAD_DIM)

    def both_heads(x):
        zero = jnp.zeros((), x.dtype)
        return jnp.concatenate([jnp.where(m0, x, zero), jnp.where(m1, x, zero)], axis=0)

    ppd = RWKV_HEADS // 2
    pairs = range(2 * ppd)
    heads = range(2 * RWKV_HEADS)
    n_in = (len(refs) - 3) // 2
    v, a_sh, r_sh, b_sh, k_sh, r_full, bkt_refs, e_half, p_end, strict, incl = ([] for _ in range(11))
    for pp in pairs:
        d, sl = pp // ppd, slice(LANES * (pp % ppd), LANES * (pp % ppd + 1))
        v_ref, ash_ref, rsh_ref, bsh_ref, ksh_ref, rfull_ref, bkt_ref, lend_ref = refs[n_in * d:n_in * (d + 1)]
        v.append(v_ref[:, sl])
        a_sh.append(ash_ref[0, :, sl])
        r_sh.append(rsh_ref[0, :, sl])
        b_sh.append(bsh_ref[0, :, sl])
        k_sh.append(ksh_ref[0, :, sl])
        r_full.append(rfull_ref[0, :, sl])
        bkt_refs.append((bkt_ref, sl))
        l_end = lend_ref[0, 0, 0:1, sl]
        e_half.append(jnp.exp2(0.5 * l_end))
        p_end.append(jnp.exp2(l_end))
        strict.append(strict_d[d])
        incl.append(incl_d[d])
    y_refs = refs[2 * n_in:2 * n_in + 2]

    amat = []
    for p in pairs:
        ar = jnp.concatenate([a_sh[p], r_sh[p]], axis=0)
        bk = jnp.concatenate([b_sh[p], k_sh[p]], axis=0)
        amat.append(lax.dot_general(both_heads(ar), bk, (((1,), (1,)), ((), ())),
                                    preferred_element_type=F32))
    n_bf, a_ak, a_rb, a_rk, t_inv = [], [], [], [], []
    for hd in heads:
        ah = amat[hd // 2][2 * C * (hd % 2):2 * C * (hd % 2 + 1)]
        n = jnp.where(strict[hd // 2], ah[:C, :C], 0.0)
        n_bf.append(_bf(n))
        t_inv.append(eye + n)
        a_ak.append(jnp.where(strict[hd // 2], ah[:C, C:], 0.0))
        a_rb.append(jnp.where(incl[hd // 2], ah[C:, :C], 0.0))
        a_rk.append(jnp.where(incl[hd // 2], ah[C:, C:], 0.0))
    def pair_products(lhs, rhs):
        out = []
        for p in pairs:
            a, b = rhs[2 * p], rhs[2 * p + 1]
            zero = jnp.zeros_like(a)
            diag = jnp.concatenate([jnp.concatenate([a, zero], axis=1),
                                    jnp.concatenate([zero, b], axis=1)], axis=0)
            prod = _dot(jnp.concatenate(lhs[2 * p:2 * p + 2], axis=1), diag)
            out += [prod[:, :C], prod[:, C:]]
        return out

    pw = [_bf(x) for x in pair_products(n_bf, n_bf)]
    avy = []
    for p in pairs:
        akrk = jnp.concatenate([jnp.concatenate(a_ak[2 * p:2 * p + 2], axis=1),
                                jnp.concatenate(a_rk[2 * p:2 * p + 2], axis=1)], axis=0)
        avy.append(_dot(_bf(akrk), both_heads(v[p])))
    for _ in range(5):
        both = [_dot(pw[hd], jnp.concatenate([pw[hd], _bf(t_inv[hd])], axis=1)) for hd in heads]
        pw = [_bf(both[hd][:, :C]) for hd in heads]
        t_inv = [t_inv[hd] + both[hd][:, C:] for hd in heads]
    last = pair_products(pw, [_bf(t) for t in t_inv])
    t_inv = [t_inv[hd] + last[hd] for hd in heads]

    aw = []
    for p in pairs:
        rhs = jnp.concatenate([both_heads(a_sh[p]), both_heads(_bf(avy[p][:C]))], axis=1)
        aw.append(_dot(_bf(jnp.concatenate(t_inv[2 * p:2 * p + 2], axis=1)), rhs))

    zs = [z_scr[p] for p in pairs]
    xs = []
    for q in range(0, 2 * ppd, 2):
        a_bar = jnp.concatenate([aw[q][:, :LANES] * e_half[q], aw[q + 1][:, :LANES] * e_half[q + 1]], axis=1)
        zero = jnp.zeros((LANES, LANES), BF16)
        z_diag = jnp.concatenate([jnp.concatenate([_bf(zs[q]), zero], axis=1),
                                  jnp.concatenate([zero, _bf(zs[q + 1])], axis=1)], axis=0)
        r_wide = jnp.concatenate([r_full[q], r_full[q + 1]], axis=1)
        x2 = _dot(jnp.concatenate([_bf(a_bar), r_wide], axis=0), z_diag)
        xs += [x2[:, :LANES], x2[:, LANES:]]
    us = [xs[p][:C] + aw[p][:, LANES:] for p in pairs]
    for p in pairs:
        y = xs[p][C:] + avy[p][C:] + _dot(_bf(jnp.concatenate(a_rb[2 * p:2 * p + 2], axis=1)),
                                           _bf(both_heads(us[p])))
        bkt_ref, sl = bkt_refs[p]
        y_refs[p // ppd][:, sl] = y
    for p in pairs:
        bkt_ref, sl = bkt_refs[p]
        bkt = bkt_ref[0, 0, sl, :]
        uv = jnp.concatenate([_bf(us[p]), v[p]], axis=0)
        pend_col = jnp.broadcast_to(p_end[p], (LANES, LANES)).T
        z_new = zs[p] * pend_col + _dot(bkt, uv)
        z_scr[p] = jnp.where(blockdiag, z_new, 0.0)


def _rwkv_scan(v, ash, rsh, bsh, ksh, rfull, bkt, lend, B, T):
    m = v.shape[0]
    nc = T // CHUNK

    def specs(d):
        def blk(bi, c):
            return bi * nc + c + d * (nc - 1 - 2 * c)

        tok = pl.BlockSpec((CHUNK, RWKV_WIDTH), lambda bi, c: (blk(bi, c), 0))
        tok2 = pl.BlockSpec((1, CHUNK, RWKV_WIDTH), lambda bi, c: (d, blk(bi, c), 0))
        return tok, [
            tok, tok2, tok2, tok2, tok2, tok2,
            pl.BlockSpec((1, 1, RWKV_WIDTH, 2 * CHUNK), lambda bi, c: (d, blk(bi, c), 0, 0)),
            pl.BlockSpec((1, 1, SUBLANES, RWKV_WIDTH), lambda bi, c: (d, blk(bi, c), 0, 0)),
        ]

    (out_f, in_f), (out_b, in_b) = specs(0), specs(1)
    operands = (v, ash, rsh, bsh, ksh, rfull, bkt, lend)
    y_shape = jax.ShapeDtypeStruct((m, RWKV_WIDTH), F32)
    return pl.pallas_call(
        _scan_kernel,
        grid=(B, nc),
        in_specs=in_f + in_b,
        out_specs=[out_f, out_b],
        out_shape=[y_shape, y_shape],
        scratch_shapes=[pltpu.VMEM((RWKV_HEADS, LANES, LANES), F32)],
        compiler_params=_params(("parallel", "arbitrary")),
        name="rwkv_scan",
    )(*operands, *operands)


def _att_prep_kernel(z_ref, gain_ref, cos_ref, sin_ref, hs_ref, qt_o, k_o, vt_o, qn_o, kn_o):
    i = pl.program_id(1)
    z = z_ref[...]
    qk = z[:, :640]
    ss = _head_sum(qk * qk, hs_ref[...])
    qk = qk * lax.rsqrt(ss * (1.0 / HEAD_DIM) + NORM_EPS) * gain_ref[...]
    width = qk.shape[1]
    lane = lax.broadcasted_iota(jnp.int32, (1, width), 1)
    first = (lane % (2 * ROPE_PAIRS)) < ROPE_PAIRS
    partner = jnp.where(first, pltpu.roll(qk, width - ROPE_PAIRS, 1), pltpu.roll(qk, ROPE_PAIRS, 1))
    cos = jnp.concatenate([cos_ref[...]] * 5, axis=1)
    sin = jnp.concatenate([sin_ref[...]] * 5, axis=1)
    qk = qk * cos + partner * sin
    for j in range(4):
        st = _bf(qk[:, LANES * j:LANES * (j + 1)].T)
        h = j // 2
        g = (2 * j) % ATT_GROUP
        qt_o[0, h, :, Q_POS * g:Q_POS * (g + 1)] = st[:HEAD_DIM]
        qt_o[0, h, :, Q_POS * (g + 1):Q_POS * (g + 2)] = st[HEAD_DIM:]
        sq = st.astype(F32)
        sq = sq * sq
        for e in range(2):
            nrm = jnp.sum(sq[HEAD_DIM * e:HEAD_DIM * (e + 1)], axis=0, keepdims=True)
            qn_o[0, h, :, Q_POS * (g + e):Q_POS * (g + e + 1)] = jnp.broadcast_to(nrm, (SUBLANES, Q_POS))
    kb = _bf(qk[:, 512:640])
    k_o[0] = kb
    kf = kb.astype(F32)
    kn = jnp.max(_head_sum(kf * kf, hs_ref[512:640, 512:640]), axis=0, keepdims=True)
    kn = jnp.broadcast_to(kn, (SUBLANES, KV_WIDTH))

    @pl.when(i == 0)
    def _():
        kn_o[0] = kn

    @pl.when(i > 0)
    def _():
        kn_o[0] = jnp.maximum(kn_o[0], kn)

    vt = _bf(z[:, 640:768].T)
    ones = jnp.ones((VT_ROWS - HEAD_DIM, Q_POS), BF16)
    for h in range(ATT_KV_HEADS):
        vt_o[0, h, :HEAD_DIM, :] = vt[HEAD_DIM * h:HEAD_DIM * (h + 1)]
        vt_o[0, h, HEAD_DIM:, :] = ones


def _att_prep(z_a, B, T, p, cos_t, sin_t):
    nb = T // Q_POS
    full = lambda shape: pl.BlockSpec(shape, lambda bi, i: tuple(0 for _ in shape))
    return pl.pallas_call(
        _att_prep_kernel,
        grid=(B, nb),
        in_specs=[
            pl.BlockSpec((Q_POS, ATT_COLS), lambda bi, i: (bi * nb + i, 0)),
            full((1, 640)),
            pl.BlockSpec((Q_POS, LANES), lambda bi, i: (i, 0)),
            pl.BlockSpec((Q_POS, LANES), lambda bi, i: (i, 0)),
            full((640, 640)),
        ],
        out_specs=[
            pl.BlockSpec((1, ATT_KV_HEADS, HEAD_DIM, ATT_GROUP * Q_POS), lambda bi, i: (bi, 0, 0, i)),
            pl.BlockSpec((1, Q_POS, KV_WIDTH), lambda bi, i: (bi, i, 0)),
            pl.BlockSpec((1, ATT_KV_HEADS, VT_ROWS, Q_POS), lambda bi, i: (bi, 0, 0, i)),
            pl.BlockSpec((1, ATT_KV_HEADS, SUBLANES, ATT_GROUP * Q_POS), lambda bi, i: (bi, 0, 0, i)),
            pl.BlockSpec((1, SUBLANES, KV_WIDTH), lambda bi, i: (bi, 0, 0)),
        ],
        out_shape=[
            jax.ShapeDtypeStruct((B, ATT_KV_HEADS, HEAD_DIM, ATT_GROUP * T), BF16),
            jax.ShapeDtypeStruct((B, T, KV_WIDTH), BF16),
            jax.ShapeDtypeStruct((B, ATT_KV_HEADS, VT_ROWS, T), BF16),
            jax.ShapeDtypeStruct((B, ATT_KV_HEADS, SUBLANES, ATT_GROUP * T), F32),
            jax.ShapeDtypeStruct((B, SUBLANES, KV_WIDTH), F32),
        ],
        compiler_params=_params(("parallel", "arbitrary")),
        name="att_prep",
    )(z_a, p["qk_gain"], cos_t, sin_t, p["head_ones_qk"])


def _attn_kernel(qt_ref, qn_ref, kn_ref, k_ref, vt_ref, o_ref, *, n_kv, tkv, unroll):
    h = pl.program_id(1)
    ncol = ATT_GROUP * Q_POS
    qt = qt_ref[0, 0]
    rowh = lax.broadcasted_iota(jnp.int32, (KV_WIDTH, 1), 0) // HEAD_DIM
    q2 = jnp.where(rowh == h, jnp.concatenate([qt, qt], axis=0), jnp.zeros((), BF16))
    laneh = lax.broadcasted_iota(jnp.int32, (1, KV_WIDTH), 1) // HEAD_DIM
    kn = jnp.max(jnp.where(laneh == h, kn_ref[0, 0:1, :], 0.0), axis=1, keepdims=True)
    shift = jnp.sqrt(qn_ref[0, 0, 0:1, :] * kn)

    def chunk_start(j):
        return j * tkv if isinstance(j, int) else pl.multiple_of(j * tkv, tkv)

    def scores(j):
        kc = k_ref[0, pl.ds(chunk_start(j), tkv), :]
        return _dot(kc, q2)

    def pv(j, pt):
        vc = vt_ref[0, 0, :, pl.ds(chunk_start(j), tkv)]
        return _dot(vc, pt)

    def sweep(step, carry):
        def group(base, carry, s, final):
            for u in range(unroll):
                s_next = None if (final and u == unroll - 1) else scores(base + u + 1)
                carry = step(base + u, carry, s)
                s = s_next
            return carry, s

        s = scores(0)
        n_groups = n_kv // unroll
        if n_groups > 1:
            carry, s = lax.fori_loop(0, n_groups - 1, lambda j, c: group(j * unroll, *c, False), (carry, s))
        carry, _ = group((n_groups - 1) * unroll, carry, s, True)
        return carry

    def emit(acc):
        o = acc[:HEAD_DIM] / acc[HEAD_DIM:HEAD_DIM + 1]
        ot = jnp.concatenate([o, jnp.zeros_like(o)], axis=0).T
        for g in range(ATT_GROUP):
            o_ref[0, :, HEAD_DIM * g:HEAD_DIM * (g + 1)] = _bf(ot[Q_POS * g:Q_POS * (g + 1), :HEAD_DIM])

    def fast_group(base, acc):
        for u in range(unroll):
            acc = acc + pv(base + u, _bf(jnp.exp2(scores(base + u) - shift)))
        return acc

    acc0 = jnp.zeros((VT_ROWS, ncol), F32)
    n_groups = n_kv // unroll
    if n_groups > 1:
        acc = lax.fori_loop(0, n_groups, lambda j, acc: fast_group(j * unroll, acc), acc0)
    else:
        acc = fast_group(0, acc0)
    emit(acc)
    denom_ok = jnp.min(acc[HEAD_DIM:HEAD_DIM + 1]) >= DENOM_FLOOR

    @pl.when(jnp.logical_not(denom_ok))
    def _():
        def step(j, carry, s):
            m, acc = carry
            m_new = jnp.maximum(m, jnp.max(s, axis=0, keepdims=True))
            return m_new, acc * jnp.exp2(m - m_new) + pv(j, _bf(jnp.exp2(s - m_new)))

        _, acc_online = sweep(step, (jnp.full((1, ncol), -jnp.inf, F32), acc0))
        emit(acc_online)


def _attention(qt, qn, kn, k, vt, B, T, tkv):
    nb = T // Q_POS
    n_kv = T // tkv
    kern = functools.partial(_attn_kernel, n_kv=n_kv, tkv=tkv, unroll=_tile(n_kv, KV_UNROLL))
    return pl.pallas_call(
        kern,
        grid=(B, ATT_KV_HEADS, nb),
        in_specs=[
            pl.BlockSpec((1, 1, HEAD_DIM, ATT_GROUP * Q_POS), lambda bi, h, i: (bi, h, 0, i)),
            pl.BlockSpec((1, 1, SUBLANES, ATT_GROUP * Q_POS), lambda bi, h, i: (bi, h, 0, i)),
            pl.BlockSpec((1, SUBLANES, KV_WIDTH), lambda bi, h, i: (bi, 0, 0)),
            pl.BlockSpec((1, T, KV_WIDTH), lambda bi, h, i: (bi, 0, 0)),
            pl.BlockSpec((1, 1, VT_ROWS, T), lambda bi, h, i: (bi, h, 0, 0)),
        ],
        out_specs=pl.BlockSpec((1, Q_POS, ATT_GROUP * HEAD_DIM), lambda bi, h, i: (bi, i, h)),
        out_shape=jax.ShapeDtypeStruct((B, T, ATT_WIDTH), BF16),
        compiler_params=_params(("parallel", "parallel", "arbitrary")),
        name="attention",
    )(qt, qn, kn, k, vt)


def _mix_ffn_kernel(x_ref, yf_ref, yb_ref, g_ref, bg_ref, att_ref, lnw_ref, lnb_ref, hs_ref, wo_r_ref, wo_a_ref,
                    g2_ref, wg_ref, wu_ref, wd_ref, gf_ref, o_ref, h_scr, acc_scr):
    j = pl.program_id(1)

    @pl.when(j == 0)
    def _():
        y = yf_ref[...] + yb_ref[...]
        hs = hs_ref[...]
        mu = _head_sum(y, hs) * (1.0 / HEAD_DIM)
        dy = y - mu
        var = _head_sum(dy * dy, hs) * (1.0 / HEAD_DIM)
        yn = dy * lax.rsqrt(var + LNX_EPS) * lnw_ref[...] + lnb_ref[...]
        yr = yn * g_ref[...].astype(F32) + bg_ref[...].astype(F32)
        x = x_ref[...] + _dot(_bf(yr), wo_r_ref[...]) + _dot(att_ref[...], wo_a_ref[...])
        ms = jnp.mean(x * x, axis=-1, keepdims=True)
        h_scr[...] = _bf(x * lax.rsqrt(ms + NORM_EPS) * g2_ref[...])
        acc_scr[...] = x

    h = h_scr[...]
    gate = _dot(h, wg_ref[...])
    up = _dot(h, wu_ref[...])
    act = gate * _sigmoid(gate) * up
    acc_scr[...] += _dot(_bf(act), wd_ref[...])

    @pl.when(j == pl.num_programs(1) - 1)
    def _():
        xo = acc_scr[...]
        ms = jnp.mean(xo * xo, axis=-1, keepdims=True)
        o_ref[...] = xo * lax.rsqrt(ms + NORM_EPS) * gf_ref[...]


def _mix_ffn(x2, y_f, y_b, g, bg, att, p, tm, tf):
    m = x2.shape[0]
    full = lambda shape: pl.BlockSpec(shape, lambda i, j: tuple(0 for _ in shape))
    tok = pl.BlockSpec((tm, RWKV_WIDTH), lambda i, j: (i, 0))
    wide = pl.BlockSpec((tm, D_MODEL), lambda i, j: (i, 0))
    return pl.pallas_call(
        _mix_ffn_kernel,
        grid=(m // tm, D_FF // tf),
        in_specs=[
            wide, tok, tok, tok, tok, tok,
            full((1, RWKV_WIDTH)), full((1, RWKV_WIDTH)), full((RWKV_WIDTH, RWKV_WIDTH)),
            full((RWKV_WIDTH, D_MODEL)), full((ATT_WIDTH, D_MODEL)),
            full((1, D_MODEL)),
            pl.BlockSpec((D_MODEL, tf), lambda i, j: (0, j)),
            pl.BlockSpec((D_MODEL, tf), lambda i, j: (0, j)),
            pl.BlockSpec((tf, D_MODEL), lambda i, j: (j, 0)),
            full((1, D_MODEL)),
        ],
        out_specs=wide,
        out_shape=jax.ShapeDtypeStruct((m, D_MODEL), F32),
        scratch_shapes=[pltpu.VMEM((tm, D_MODEL), BF16), pltpu.VMEM((tm, D_MODEL), F32)],
        compiler_params=_params(("parallel", "arbitrary")),
        name="mix_ffn",
    )(x2, y_f, y_b, g, bg, att, p["lnx_w"], p["lnx_b"], p["head_ones"], p["wo_r"], p["wo_a"],
      p["norm2_g"], p["ffn_gate"], p["ffn_up"], p["ffn_down"], p["norm_f_g"])


def _rope_tables(T):
    n_rows = T // GRID_W
    t = jnp.arange(T, dtype=jnp.int32)
    row = (t // GRID_W).astype(F32)
    col = (t % GRID_W).astype(F32)
    inv = ROPE_THETA ** (-jnp.arange(ROPE_PAIRS, dtype=F32) / ROPE_PAIRS)
    ar = row[:, None] * inv
    ac = col[:, None] * inv
    cos = jnp.concatenate([jnp.cos(ar), jnp.cos(ar), jnp.cos(ac), jnp.cos(ac)], axis=1)
    sin = jnp.concatenate([-jnp.sin(ar), jnp.sin(ar), -jnp.sin(ac), jnp.sin(ac)], axis=1)
    del n_rows
    return jnp.tile(cos, (1, 2)), jnp.tile(sin, (1, 2))


def _block_diag2(a, b):
    za = jnp.zeros_like(a)
    return jnp.concatenate([jnp.concatenate([a, za], axis=1), jnp.concatenate([za, b], axis=1)], axis=0)


def _prepare_params(norm1_g, w_in, mu_prev, mu_next, k_k, k_a, r_k, w0_f, w_lora_f, w0_b, w_lora_b,
                    a0_f, a_lora_f, a0_b, a_lora_b, g_lora, lnx_w, lnx_b, q_gain, k_gain, w_out,
                    norm2_g, ffn_gate, ffn_up, ffn_down, norm_f_g):
    l = 0
    p = {}
    p["norm1_g"] = norm1_g[l][None]
    p["w_r"] = w_in[l][:, :RWKV_COLS].astype(BF16)
    p["w_a"] = w_in[l][:, RWKV_COLS:].astype(BF16)
    p["mu_prev"] = mu_prev[l][None]
    p["mu_next"] = mu_next[l][None]
    p["k_k"] = k_k[l][None]
    p["k_a"] = k_a[l][None]
    p["r_k"] = r_k[l].reshape(1, RWKV_WIDTH)
    p["w0"] = jnp.concatenate([w0_f[l], w0_b[l]])[None]
    p["w_lora"] = _block_diag2(w_lora_f[l], w_lora_b[l]).astype(BF16)
    p["a0"] = jnp.concatenate([a0_f[l], a0_b[l]])[None]
    p["a_lora"] = _block_diag2(a_lora_f[l], a_lora_b[l]).astype(BF16)
    p["g_lora"] = g_lora[l].astype(BF16)
    p["lnx_w"] = lnx_w[l][None]
    p["lnx_b"] = lnx_b[l][None]
    scale = HEAD_DIM ** -0.5 * float(np.log2(np.e))
    p["qk_gain"] = jnp.concatenate([jnp.tile(q_gain[l] * scale, ATT_Q_HEADS), jnp.tile(k_gain[l], ATT_KV_HEADS)])[None]
    hid = np.arange(640) // HEAD_DIM
    ones = (hid[:, None] == hid[None, :]).astype(np.float32)
    p["head_ones_qk"] = jnp.asarray(ones, BF16)
    p["head_ones"] = jnp.asarray(ones[:512, :512], BF16)
    p["wo_r"] = w_out[l][:RWKV_WIDTH].astype(BF16)
    p["wo_a"] = w_out[l][RWKV_WIDTH:].astype(BF16)
    p["norm2_g"] = norm2_g[l][None]
    p["ffn_gate"] = ffn_gate[l].astype(BF16)
    p["ffn_up"] = ffn_up[l].astype(BF16)
    p["ffn_down"] = ffn_down[l].astype(BF16)
    p["norm_f_g"] = norm_f_g[None]
    return p


def _tile(n, pref):
    t = pref
    while n % t:
        t //= 2
    return t


def _trunk(x, p):
    B, T, D = x.shape
    m = B * T
    x2 = x.reshape(m, D)
    z_r, z_a = _inproj(x2, p["norm1_g"], p["w_r"], p["w_a"], _tile(m, 512))
    v, ash, rsh, bsh, ksh, rfull, bkt, lend, g, bg = _rwkv_prep(z_r, T, p, _tile(T, 256))
    y_f, y_b = _rwkv_scan(v, ash, rsh, bsh, ksh, rfull, bkt, lend, B, T)
    cos_t, sin_t = _rope_tables(T)
    qt, k, vt, qn, kn = _att_prep(z_a, B, T, p, cos_t, sin_t)
    att = _attention(qt, qn, kn, k, vt, B, T, _tile(T, KV_CHUNK)).reshape(m, ATT_WIDTH)
    out = _mix_ffn(x2, y_f, y_b, g, bg, att, p, _tile(m, 512), 1408)
    return out.reshape(B, T, D)


def kernel(x_prompt, x_sample, norm1_g, w_in, mu_prev, mu_next, k_k, k_a, r_k, w0_f, w_lora_f, w0_b, w_lora_b, a0_f, a_lora_f, a0_b, a_lora_b, g_lora, lnx_w, lnx_b, q_gain, k_gain, w_out, norm2_g, ffn_gate, ffn_up, ffn_down, norm_f_g):
    p = _prepare_params(norm1_g, w_in, mu_prev, mu_next, k_k, k_a, r_k, w0_f, w_lora_f, w0_b, w_lora_b,
                        a0_f, a_lora_f, a0_b, a_lora_b, g_lora, lnx_w, lnx_b, q_gain, k_gain, w_out,
                        norm2_g, ffn_gate, ffn_up, ffn_down, norm_f_g)
    return (_trunk(x_prompt, p), _trunk(x_sample, p))
```

```python
import functools

import jax
import jax.numpy as jnp
import numpy as np
from jax import lax
from jax.experimental import pallas as pl
from jax.experimental.pallas import tpu as pltpu

F32 = jnp.float32
BF16 = jnp.bfloat16

D_MODEL = 1024
HEAD_DIM = 64
RWKV_WIDTH = 512
RWKV_HEADS = 8
ATT_WIDTH = 512
ATT_Q_HEADS = 8
ATT_KV_HEADS = 2
ATT_GROUP = 4
KV_WIDTH = 128
RWKV_COLS = 1920
ATT_COLS = 768
D_FF = 2816
GRID_W = 64
ROPE_THETA = 10000.0
ROPE_PAIRS = 16
NORM_EPS = 1e-6
LNX_EPS = 64e-5

LANES = 128
SUBLANES = 8
CHUNK = 128
Q_POS = 256
Q_POS_SHORT = 512
KV_CHUNK = 2048
KV_UNROLL = 2
VT_ROWS = 80
DECAY_SCALE_LOG2 = float(np.exp(-0.5) * np.log2(np.e))
DENOM_FLOOR = 2.0 ** -100
VMEM_LIMIT = 56 * 1024 * 1024


def _dot(a, b):
    return jnp.dot(a, b, preferred_element_type=F32)


def _bf(x):
    return x.astype(BF16)


def _split3(x):
    hi = _bf(x)
    r1 = x - hi.astype(F32)
    mid = _bf(r1)
    lo = _bf(r1 - mid.astype(F32))
    return hi, mid, lo


def _head_sum(x, ones_bf):
    return _dot(_bf(x), ones_bf)


def _sigmoid(x):
    return 0.5 * jnp.tanh(0.5 * x) + 0.5


def _softplus(x):
    return jnp.maximum(x, 0.0) + jnp.log(1.0 + jnp.exp(-jnp.abs(x)))


def _params(sem):
    return pltpu.CompilerParams(dimension_semantics=sem, vmem_limit_bytes=VMEM_LIMIT)


def _inproj_kernel(x_ref, g_ref, wr_ref, wa_ref, zr_ref, za_ref):
    x = x_ref[...]
    ms = jnp.mean(x * x, axis=-1, keepdims=True)
    h = _bf(x * lax.rsqrt(ms + NORM_EPS) * g_ref[...])
    zr_ref[...] = _dot(h, wr_ref[...])
    za_ref[...] = _dot(h, wa_ref[...])


def _inproj(x2, norm1_g, w_r, w_a, tm):
    m = x2.shape[0]
    return pl.pallas_call(
        _inproj_kernel,
        grid=(m // tm,),
        in_specs=[
            pl.BlockSpec((tm, D_MODEL), lambda i: (i, 0)),
            pl.BlockSpec((1, D_MODEL), lambda i: (0, 0)),
            pl.BlockSpec((D_MODEL, RWKV_COLS), lambda i: (0, 0)),
            pl.BlockSpec((D_MODEL, ATT_COLS), lambda i: (0, 0)),
        ],
        out_specs=[
            pl.BlockSpec((tm, RWKV_COLS), lambda i: (i, 0)),
            pl.BlockSpec((tm, ATT_COLS), lambda i: (i, 0)),
        ],
        out_shape=[
            jax.ShapeDtypeStruct((m, RWKV_COLS), F32),
            jax.ShapeDtypeStruct((m, ATT_COLS), F32),
        ],
        compiler_params=_params(("parallel",)),
        name="inproj",
    )(x2, norm1_g, w_r, w_a)


def _rwkv_prep_kernel(z_ref, zp_ref, zn_ref, mup_ref, mun_ref, kk_ref, ka_ref, rk_ref,
                      w0_ref, wl_ref, a0_ref, al_ref, gl_ref,
                      hs_ref,
                      v_o, ash_o, rsh_o, bsh_o, ksh_o, rfull_o, bkt_o, lend_o, g_o, bg_o,
                      *, tm, blocks_per_seq):
    i = pl.program_id(0)
    pos = i % blocks_per_seq
    z = z_ref[...]
    mup, mun = mup_ref[...], mun_ref[...]
    ri = lax.broadcasted_iota(jnp.int32, (tm, tm), 0)
    ci = lax.broadcasted_iota(jnp.int32, (tm, tm), 1)
    shifts = jnp.concatenate([(ci == ri - 1).astype(BF16), (ci == ri + 1).astype(BF16)], axis=1)
    zf = z * (1.0 - mup - mun) + _dot(shifts, jnp.concatenate([_bf(z * mup), _bf(z * mun)], axis=0))
    prev_row = jnp.where(pos == 0, 0.0, zp_ref[SUBLANES - 1:SUBLANES, :]) * mup
    next_row = jnp.where(pos == blocks_per_seq - 1, 0.0, zn_ref[0:1, :]) * mun
    rows = lax.broadcasted_iota(jnp.int32, (SUBLANES, 1), 0)
    zf = jnp.concatenate([zf[:SUBLANES] + jnp.where(rows == 0, prev_row, 0.0),
                          zf[SUBLANES:tm - SUBLANES],
                          zf[tm - SUBLANES:] + jnp.where(rows == SUBLANES - 1, next_row, 0.0)], axis=0)

    r = zf[:, 0:512]
    k = zf[:, 512:1024]
    v = zf[:, 1024:1536]
    wd = zf[:, 1536:1664]
    ad = zf[:, 1664:1792]
    gd = zf[:, 1792:1920]
    hs = hs_ref[...]

    kk = k * kk_ref[...]
    ss = _head_sum(kk * kk, hs)
    kk = kk * lax.rsqrt(jnp.maximum(ss, 1e-12))

    lw_both = w0_ref[...] + _dot(_bf(jnp.tanh(wd)), wl_ref[...])
    as_both = a0_ref[...] + _dot(_bf(ad), al_ref[...])
    ka_half = 0.5 * ka_ref[...]
    same_chunk = (ri // CHUNK) == (ci // CHUNK)
    n_chunks = tm // CHUNK
    kd_sum = jnp.zeros_like(k)
    for d in range(2):
        lw = -DECAY_SCALE_LOG2 * _sigmoid(lw_both[:, 512 * d:512 * (d + 1)])
        th = jnp.tanh(0.5 * as_both[:, 512 * d:512 * (d + 1)])
        kd = k * (1.0 + (th - 1.0) * ka_half)
        b = kk * (0.5 * th + 0.5)
        kd_sum = kd_sum + kd
        processed = (ci <= ri) if d == 0 else (ci >= ri)
        tri = (same_chunk & processed).astype(BF16)
        hi, mid, lo = _split3(lw)
        cum = _dot(tri, hi) + _dot(tri, mid) + _dot(tri, lo)
        last = CHUNK - 1 if d == 0 else 0
        l_end = [cum[CHUNK * cc + last:CHUNK * cc + last + 1] for cc in range(n_chunks)]
        for cc in range(n_chunks):
            lend_o[d, cc] = jnp.broadcast_to(l_end[cc], (SUBLANES, RWKV_WIDTH))
        rows = lambda vals: jnp.concatenate([jnp.broadcast_to(x, (CHUNK, RWKV_WIDTH)) for x in vals], axis=0)
        l_half = rows([0.5 * x for x in l_end])
        e_half = rows([jnp.exp2(0.5 * x) for x in l_end])
        r_sh = r * jnp.exp2(cum - l_half)
        ash_o[d] = _bf(-kk * jnp.exp2(cum - lw - l_half))
        rsh_o[d] = _bf(r_sh)
        rfull_o[d] = _bf(r_sh * e_half)
        e_b = jnp.exp2(l_half - cum)
        b_sh = b * e_b
        k_sh = kd * e_b
        bsh_o[d] = _bf(b_sh)
        ksh_o[d] = _bf(k_sh)
        b_hat = b_sh * e_half
        k_hat = k_sh * e_half
        for cc in range(n_chunks):
            rs = slice(CHUNK * cc, CHUNK * (cc + 1))
            for p in range(RWKV_HEADS // 2):
                ls = slice(LANES * p, LANES * (p + 1))
                bkt_o[d, cc, ls, 0:CHUNK] = _bf(b_hat[rs, ls].T)
                bkt_o[d, cc, ls, CHUNK:2 * CHUNK] = _bf(k_hat[rs, ls].T)
    coef = _head_sum(r * kd_sum * (0.5 * rk_ref[...]), hs)
    g = _dot(_bf(_sigmoid(gd)), gl_ref[...])
    v_o[...] = _bf(v)
    g_o[...] = _bf(g)
    bg_o[...] = _bf(coef * v * g)


def _rwkv_prep(z_r, T, p, tm):
    m = z_r.shape[0]
    bps = T // tm
    hb = tm // SUBLANES
    nhalo = m // SUBLANES
    full = lambda shape: pl.BlockSpec(shape, lambda i: tuple(0 for _ in shape))
    tok = pl.BlockSpec((tm, RWKV_WIDTH), lambda i: (i, 0))
    tok2 = pl.BlockSpec((2, tm, RWKV_WIDTH), lambda i: (0, i, 0))
    cpb = tm // CHUNK
    nchunk = m // CHUNK
    tok2_shape = jax.ShapeDtypeStruct((2, m, RWKV_WIDTH), BF16)
    kern = functools.partial(_rwkv_prep_kernel, tm=tm, blocks_per_seq=bps)
    return pl.pallas_call(
        kern,
        grid=(m // tm,),
        in_specs=[
            pl.BlockSpec((tm, RWKV_COLS), lambda i: (i, 0)),
            pl.BlockSpec((SUBLANES, RWKV_COLS), lambda i: (jnp.maximum(i * hb - 1, 0), 0)),
            pl.BlockSpec((SUBLANES, RWKV_COLS), lambda i: (jnp.minimum((i + 1) * hb, nhalo - 1), 0)),
            full((1, RWKV_COLS)), full((1, RWKV_COLS)),
            full((1, 512)), full((1, 512)), full((1, 512)),
            full((1, 1024)), full((128, 1024)),
            full((1, 1024)), full((128, 1024)),
            full((128, 512)),
            full((512, 512)),
        ],
        out_specs=[
            tok, tok2, tok2, tok2, tok2, tok2,
            pl.BlockSpec((2, cpb, RWKV_WIDTH, 2 * CHUNK), lambda i: (0, i, 0, 0)),
            pl.BlockSpec((2, cpb, SUBLANES, RWKV_WIDTH), lambda i: (0, i, 0, 0)),
            tok, tok,
        ],
        out_shape=[
            jax.ShapeDtypeStruct((m, RWKV_WIDTH), BF16),
            tok2_shape, tok2_shape, tok2_shape, tok2_shape, tok2_shape,
            jax.ShapeDtypeStruct((2, nchunk, RWKV_WIDTH, 2 * CHUNK), BF16),
            jax.ShapeDtypeStruct((2, nchunk, SUBLANES, RWKV_WIDTH), F32),
            jax.ShapeDtypeStruct((m, RWKV_WIDTH), BF16),
            jax.ShapeDtypeStruct((m, RWKV_WIDTH), BF16),
        ],
        compiler_params=_params(("parallel",)),
        name="rwkv_prep",
    )(z_r, z_r, z_r, p["mu_prev"], p["mu_next"], p["k_k"], p["k_a"], p["r_k"],
      p["w0"], p["w_lora"], p["a0"], p["a_lora"], p["g_lora"],
      p["head_ones"])


def _scan_kernel(*refs):
    C = CHUNK
    c = pl.program_id(1)
    z_scr = refs[-1]

    @pl.when(c == 0)
    def _():
        z_scr[...] = jnp.zeros_like(z_scr)

    row = lax.broadcasted_iota(jnp.int32, (C, C), 0)
    col = lax.broadcasted_iota(jnp.int32, (C, C), 1)
    strict_d = [col < row, col > row]
    incl_d = [col <= row, col >= row]
    eye = (row == col).astype(F32)
    lane = lax.broadcasted_iota(jnp.int32, (1, LANES), 1)
    m0 = lane < HEAD_DIM
    m1 = lane >= HEAD_DIM
    blockdiag = (row < HEAD_DIM) == (col < HEAD_DIM)

    def both_heads(x):
        zero = jnp.zeros((), x.dtype)
        return jnp.concatenate([jnp.where(m0, x, zero), jnp.where(m1, x, zero)], axis=0)

    ppd = RWKV_HEADS // 2
    pairs = range(2 * ppd)
    heads = range(2 * RWKV_HEADS)
    n_in = (len(refs) - 3) // 2
    v, a_sh, r_sh, b_sh, k_sh, r_full, bkt_refs, e_half, p_end, strict, incl = ([] for _ in range(11))
    for pp in pairs:
        d, sl = pp // ppd, slice(LANES * (pp % ppd), LANES * (pp % ppd + 1))
        v_ref, ash_ref, rsh_ref, bsh_ref, ksh_ref, rfull_ref, bkt_ref, lend_ref = refs[n_in * d:n_in * (d + 1)]
        v.append(v_ref[:, sl])
        a_sh.append(ash_ref[0, :, sl])
        r_sh.append(rsh_ref[0, :, sl])
        b_sh.append(bsh_ref[0, :, sl])
        k_sh.append(ksh_ref[0, :, sl])
        r_full.append(rfull_ref[0, :, sl])
        bkt_refs.append((bkt_ref, sl))
        l_end = lend_ref[0, 0, 0:1, sl]
        e_half.append(jnp.exp2(0.5 * l_end))
        p_end.append(jnp.exp2(l_end))
        strict.append(strict_d[d])
        incl.append(incl_d[d])
    y_refs = refs[2 * n_in:2 * n_in + 2]

    amat = []
    for p in pairs:
        ar = jnp.concatenate([a_sh[p], r_sh[p]], axis=0)
        bk = jnp.concatenate([b_sh[p], k_sh[p]], axis=0)
        amat.append(lax.dot_general(both_heads(ar), bk, (((1,), (1,)), ((), ())),
                                    preferred_element_type=F32))
    n_bf, a_ak, a_rb, a_rk, t_inv = [], [], [], [], []
    for hd in heads:
        ah = amat[hd // 2][2 * C * (hd % 2):2 * C * (hd % 2 + 1)]
        n = jnp.where(strict[hd // 2], ah[:C, :C], 0.0)
        n_bf.append(_bf(n))
        t_inv.append(eye + n)
        a_ak.append(jnp.where(strict[hd // 2], ah[:C, C:], 0.0))
        a_rb.append(jnp.where(incl[hd // 2], ah[C:, :C], 0.0))
        a_rk.append(jnp.where(incl[hd // 2], ah[C:, C:], 0.0))
    def pair_products(lhs, rhs):
        out = []
        for p in pairs:
            a, b = rhs[2 * p], rhs[2 * p + 1]
            zero = jnp.zeros_like(a)
            diag = jnp.concatenate([jnp.concatenate([a, zero], axis=1),
                                    jnp.concatenate([zero, b], axis=1)], axis=0)
            prod = _dot(jnp.concatenate(lhs[2 * p:2 * p + 2], axis=1), diag)
            out += [prod[:, :C], prod[:, C:]]
        return out

    pw = [_bf(x) for x in pair_products(n_bf, n_bf)]
    avy = []
    for p in pairs:
        akrk = jnp.concatenate([jnp.concatenate(a_ak[2 * p:2 * p + 2], axis=1),
                                jnp.concatenate(a_rk[2 * p:2 * p + 2], axis=1)], axis=0)
        avy.append(_dot(_bf(akrk), both_heads(v[p])))
    for _ in range(5):
        both = [_dot(pw[hd], jnp.concatenate([pw[hd], _bf(t_inv[hd])], axis=1)) for hd in heads]
        pw = [_bf(both[hd][:, :C]) for hd in heads]
        t_inv = [t_inv[hd] + both[hd][:, C:] for hd in heads]
    last = pair_products(pw, [_bf(t) for t in t_inv])
    t_inv = [t_inv[hd] + last[hd] for hd in heads]

    aw = []
    for p in pairs:
        rhs = jnp.concatenate([both_heads(a_sh[p]), both_heads(_bf(avy[p][:C]))], axis=1)
        aw.append(_dot(_bf(jnp.concatenate(t_inv[2 * p:2 * p + 2], axis=1)), rhs))

    zs = [z_scr[p] for p in pairs]
    xs = []
    for q in range(0, 2 * ppd, 2):
        a_bar = jnp.concatenate([aw[q][:, :LANES] * e_half[q], aw[q + 1][:, :LANES] * e_half[q + 1]], axis=1)
        zero = jnp.zeros((LANES, LANES), BF16)
        z_diag = jnp.concatenate([jnp.concatenate([_bf(zs[q]), zero], axis=1),
                                  jnp.concatenate([zero, _bf(zs[q + 1])], axis=1)], axis=0)
        r_wide = jnp.concatenate([r_full[q], r_full[q + 1]], axis=1)
        x2 = _dot(jnp.concatenate([_bf(a_bar), r_wide], axis=0), z_diag)
        xs += [x2[:, :LANES], x2[:, LANES:]]
    us = [xs[p][:C] + aw[p][:, LANES:] for p in pairs]
    for p in pairs:
        y = xs[p][C:] + avy[p][C:] + _dot(_bf(jnp.concatenate(a_rb[2 * p:2 * p + 2], axis=1)),
                                           _bf(both_heads(us[p])))
        bkt_ref, sl = bkt_refs[p]
        y_refs[p // ppd][:, sl] = y
    for p in pairs:
        bkt_ref, sl = bkt_refs[p]
        bkt = bkt_ref[0, 0, sl, :]
        uv = jnp.concatenate([_bf(us[p]), v[p]], axis=0)
        pend_col = jnp.broadcast_to(p_end[p], (LANES, LANES)).T
        z_new = zs[p] * pend_col + _dot(bkt, uv)
        z_scr[p] = jnp.where(blockdiag, z_new, 0.0)


def _rwkv_scan(v, ash, rsh, bsh, ksh, rfull, bkt, lend, B, T):
    m = v.shape[0]
    nc = T // CHUNK

    def specs(d):
        def blk(bi, c):
            return bi * nc + c + d * (nc - 1 - 2 * c)

        tok = pl.BlockSpec((CHUNK, RWKV_WIDTH), lambda bi, c: (blk(bi, c), 0))
        tok2 = pl.BlockSpec((1, CHUNK, RWKV_WIDTH), lambda bi, c: (d, blk(bi, c), 0))
        return tok, [
            tok, tok2, tok2, tok2, tok2, tok2,
            pl.BlockSpec((1, 1, RWKV_WIDTH, 2 * CHUNK), lambda bi, c: (d, blk(bi, c), 0, 0)),
            pl.BlockSpec((1, 1, SUBLANES, RWKV_WIDTH), lambda bi, c: (d, blk(bi, c), 0, 0)),
        ]

    (out_f, in_f), (out_b, in_b) = specs(0), specs(1)
    operands = (v, ash, rsh, bsh, ksh, rfull, bkt, lend)
    y_shape = jax.ShapeDtypeStruct((m, RWKV_WIDTH), F32)
    return pl.pallas_call(
        _scan_kernel,
        grid=(B, nc),
        in_specs=in_f + in_b,
        out_specs=[out_f, out_b],
        out_shape=[y_shape, y_shape],
        scratch_shapes=[pltpu.VMEM((RWKV_HEADS, LANES, LANES), F32)],
        compiler_params=_params(("parallel", "arbitrary")),
        name="rwkv_scan",
    )(*operands, *operands)


def _att_prep_kernel(z_ref, gain_ref, cos_ref, sin_ref, hs_ref, qt_o, k_o, vt_o, qn_o, kn_o):
    i = pl.program_id(1)
    q_pos = z_ref.shape[0]
    z = z_ref[...]
    qk = z[:, :640]
    ss = _head_sum(qk * qk, hs_ref[...])
    qk = qk * lax.rsqrt(ss * (1.0 / HEAD_DIM) + NORM_EPS) * gain_ref[...]
    width = qk.shape[1]
    lane = lax.broadcasted_iota(jnp.int32, (1, width), 1)
    first = (lane % (2 * ROPE_PAIRS)) < ROPE_PAIRS
    partner = jnp.where(first, pltpu.roll(qk, width - ROPE_PAIRS, 1), pltpu.roll(qk, ROPE_PAIRS, 1))
    cos = jnp.concatenate([cos_ref[...]] * 5, axis=1)
    sin = jnp.concatenate([sin_ref[...]] * 5, axis=1)
    qk = qk * cos + partner * sin
    for j in range(4):
        st = _bf(qk[:, LANES * j:LANES * (j + 1)].T)
        h = j // 2
        g = (2 * j) % ATT_GROUP
        qt_o[0, h, :, q_pos * g:q_pos * (g + 1)] = st[:HEAD_DIM]
        qt_o[0, h, :, q_pos * (g + 1):q_pos * (g + 2)] = st[HEAD_DIM:]
        sq = st.astype(F32)
        sq = sq * sq
        for e in range(2):
            nrm = jnp.sum(sq[HEAD_DIM * e:HEAD_DIM * (e + 1)], axis=0, keepdims=True)
            qn_o[0, h, :, q_pos * (g + e):q_pos * (g + e + 1)] = jnp.broadcast_to(nrm, (SUBLANES, q_pos))
    kb = _bf(qk[:, 512:640])
    k_o[0] = kb
    kf = kb.astype(F32)
    kn = jnp.max(_head_sum(kf * kf, hs_ref[512:640, 512:640]), axis=0, keepdims=True)
    kn = jnp.broadcast_to(kn, (SUBLANES, KV_WIDTH))

    @pl.when(i == 0)
    def _():
        kn_o[0] = kn

    @pl.when(i > 0)
    def _():
        kn_o[0] = jnp.maximum(kn_o[0], kn)

    vt = _bf(z[:, 640:768].T)
    ones = jnp.ones((VT_ROWS - HEAD_DIM, q_pos), BF16)
    for h in range(ATT_KV_HEADS):
        vt_o[0, h, :HEAD_DIM, :] = vt[HEAD_DIM * h:HEAD_DIM * (h + 1)]
        vt_o[0, h, HEAD_DIM:, :] = ones


def _att_prep(z_a, B, T, p, cos_t, sin_t, q_pos):
    nb = T // q_pos
    full = lambda shape: pl.BlockSpec(shape, lambda bi, i: tuple(0 for _ in shape))
    return pl.pallas_call(
        _att_prep_kernel,
        grid=(B, nb),
        in_specs=[
            pl.BlockSpec((q_pos, ATT_COLS), lambda bi, i: (bi * nb + i, 0)),
            full((1, 640)),
            pl.BlockSpec((q_pos, LANES), lambda bi, i: (i, 0)),
            pl.BlockSpec((q_pos, LANES), lambda bi, i: (i, 0)),
            full((640, 640)),
        ],
        out_specs=[
            pl.BlockSpec((1, ATT_KV_HEADS, HEAD_DIM, ATT_GROUP * q_pos), lambda bi, i: (bi, 0, 0, i)),
            pl.BlockSpec((1, q_pos, KV_WIDTH), lambda bi, i: (bi, i, 0)),
            pl.BlockSpec((1, ATT_KV_HEADS, VT_ROWS, q_pos), lambda bi, i: (bi, 0, 0, i)),
            pl.BlockSpec((1, ATT_KV_HEADS, SUBLANES, ATT_GROUP * q_pos), lambda bi, i: (bi, 0, 0, i)),
            pl.BlockSpec((1, SUBLANES, KV_WIDTH), lambda bi, i: (bi, 0, 0)),
        ],
        out_shape=[
            jax.ShapeDtypeStruct((B, ATT_KV_HEADS, HEAD_DIM, ATT_GROUP * T), BF16),
            jax.ShapeDtypeStruct((B, T, KV_WIDTH), BF16),
            jax.ShapeDtypeStruct((B, ATT_KV_HEADS, VT_ROWS, T), BF16),
            jax.ShapeDtypeStruct((B, ATT_KV_HEADS, SUBLANES, ATT_GROUP * T), F32),
            jax.ShapeDtypeStruct((B, SUBLANES, KV_WIDTH), F32),
        ],
        compiler_params=_params(("parallel", "arbitrary")),
        name="att_prep",
    )(z_a, p["qk_gain"], cos_t, sin_t, p["head_ones_qk"])


def _attn_kernel(qt_ref, qn_ref, kn_ref, k_ref, vt_ref, o_ref, *, n_kv, tkv, unroll):
    h = pl.program_id(1)
    ncol = qt_ref.shape[-1]
    q_pos = ncol // ATT_GROUP
    qt = qt_ref[0, 0]
    rowh = lax.broadcasted_iota(jnp.int32, (KV_WIDTH, 1), 0) // HEAD_DIM
    q2 = jnp.where(rowh == h, jnp.concatenate([qt, qt], axis=0), jnp.zeros((), BF16))
    laneh = lax.broadcasted_iota(jnp.int32, (1, KV_WIDTH), 1) // HEAD_DIM
    kn = jnp.max(jnp.where(laneh == h, kn_ref[0, 0:1, :], 0.0), axis=1, keepdims=True)
    shift = jnp.sqrt(qn_ref[0, 0, 0:1, :] * kn)

    def chunk_start(j):
        return j * tkv if isinstance(j, int) else pl.multiple_of(j * tkv, tkv)

    def scores(j):
        kc = k_ref[0, pl.ds(chunk_start(j), tkv), :]
        return _dot(kc, q2)

    def pv(j, pt):
        vc = vt_ref[0, 0, :, pl.ds(chunk_start(j), tkv)]
        return _dot(vc, pt)

    def sweep(step, carry):
        def group(base, carry, s, final):
            for u in range(unroll):
                s_next = None if (final and u == unroll - 1) else scores(base + u + 1)
                carry = step(base + u, carry, s)
                s = s_next
            return carry, s

        s = scores(0)
        n_groups = n_kv // unroll
        if n_groups > 1:
            carry, s = lax.fori_loop(0, n_groups - 1, lambda j, c: group(j * unroll, *c, False), (carry, s))
        carry, _ = group((n_groups - 1) * unroll, carry, s, True)
        return carry

    def emit(acc):
        o = acc[:HEAD_DIM] / acc[HEAD_DIM:HEAD_DIM + 1]
        ot = jnp.concatenate([o, jnp.zeros_like(o)], axis=0).T
        for g in range(ATT_GROUP):
            o_ref[0, :, HEAD_DIM * g:HEAD_DIM * (g + 1)] = _bf(ot[q_pos * g:q_pos * (g + 1), :HEAD_DIM])

    def fast_group(base, acc):
        for u in range(unroll):
            acc = acc + pv(base + u, _bf(jnp.exp2(scores(base + u) - shift)))
        return acc

    acc0 = jnp.zeros((VT_ROWS, ncol), F32)
    n_groups = n_kv // unroll
    if n_groups > 1:
        acc = lax.fori_loop(0, n_groups, lambda j, acc: fast_group(j * unroll, acc), acc0)
    else:
        acc = fast_group(0, acc0)
    emit(acc)
    denom_ok = jnp.min(acc[HEAD_DIM:HEAD_DIM + 1]) >= DENOM_FLOOR

    @pl.when(jnp.logical_not(denom_ok))
    def _():
        def step(j, carry, s):
            m, acc = carry
            m_new = jnp.maximum(m, jnp.max(s, axis=0, keepdims=True))
            return m_new, acc * jnp.exp2(m - m_new) + pv(j, _bf(jnp.exp2(s - m_new)))

        _, acc_online = sweep(step, (jnp.full((1, ncol), -jnp.inf, F32), acc0))
        emit(acc_online)


def _attention(qt, qn, kn, k, vt, B, T, q_pos, tkv):
    nb = T // q_pos
    n_kv = T // tkv
    kern = functools.partial(_attn_kernel, n_kv=n_kv, tkv=tkv, unroll=_tile(n_kv, KV_UNROLL))
    return pl.pallas_call(
        kern,
        grid=(B, ATT_KV_HEADS, nb),
        in_specs=[
            pl.BlockSpec((1, 1, HEAD_DIM, ATT_GROUP * q_pos), lambda bi, h, i: (bi, h, 0, i)),
            pl.BlockSpec((1, 1, SUBLANES, ATT_GROUP * q_pos), lambda bi, h, i: (bi, h, 0, i)),
            pl.BlockSpec((1, SUBLANES, KV_WIDTH), lambda bi, h, i: (bi, 0, 0)),
            pl.BlockSpec((1, T, KV_WIDTH), lambda bi, h, i: (bi, 0, 0)),
            pl.BlockSpec((1, 1, VT_ROWS, T), lambda bi, h, i: (bi, h, 0, 0)),
        ],
        out_specs=pl.BlockSpec((1, q_pos, ATT_GROUP * HEAD_DIM), lambda bi, h, i: (bi, i, h)),
        out_shape=jax.ShapeDtypeStruct((B, T, ATT_WIDTH), BF16),
        compiler_params=_params(("parallel", "parallel", "arbitrary")),
        name="attention",
    )(qt, qn, kn, k, vt)


def _mix_ffn_kernel(x_ref, yf_ref, yb_ref, g_ref, bg_ref, att_ref, lnw_ref, lnb_ref, hs_ref, wo_r_ref, wo_a_ref,
                    g2_ref, wg_ref, wu_ref, wd_ref, gf_ref, o_ref, h_scr, acc_scr):
    j = pl.program_id(1)

    @pl.when(j == 0)
    def _():
        y = yf_ref[...] + yb_ref[...]
        hs = hs_ref[...]
        mu = _head_sum(y, hs) * (1.0 / HEAD_DIM)
        dy = y - mu
        var = _head_sum(dy * dy, hs) * (1.0 / HEAD_DIM)
        yn = dy * lax.rsqrt(var + LNX_EPS) * lnw_ref[...] + lnb_ref[...]
        yr = yn * g_ref[...].astype(F32) + bg_ref[...].astype(F32)
        x = x_ref[...] + _dot(_bf(yr), wo_r_ref[...]) + _dot(att_ref[...], wo_a_ref[...])
        ms = jnp.mean(x * x, axis=-1, keepdims=True)
        h_scr[...] = _bf(x * lax.rsqrt(ms + NORM_EPS) * g2_ref[...])
        acc_scr[...] = x

    h = h_scr[...]
    gate = _dot(h, wg_ref[...])
    up = _dot(h, wu_ref[...])
    act = gate * _sigmoid(gate) * up
    acc_scr[...] += _dot(_bf(act), wd_ref[...])

    @pl.when(j == pl.num_programs(1) - 1)
    def _():
        xo = acc_scr[...]
        ms = jnp.mean(xo * xo, axis=-1, keepdims=True)
        o_ref[...] = xo * lax.rsqrt(ms + NORM_EPS) * gf_ref[...]


def _mix_ffn(x2, y_f, y_b, g, bg, att, p, tm, tf):
    m = x2.shape[0]
    full = lambda shape: pl.BlockSpec(shape, lambda i, j: tuple(0 for _ in shape))
    tok = pl.BlockSpec((tm, RWKV_WIDTH), lambda i, j: (i, 0))
    wide = pl.BlockSpec((tm, D_MODEL), lambda i, j: (i, 0))
    return pl.pallas_call(
        _mix_ffn_kernel,
        grid=(m // tm, D_FF // tf),
        in_specs=[
            wide, tok, tok, tok, tok, tok,
            full((1, RWKV_WIDTH)), full((1, RWKV_WIDTH)), full((RWKV_WIDTH, RWKV_WIDTH)),
            full((RWKV_WIDTH, D_MODEL)), full((ATT_WIDTH, D_MODEL)),
            full((1, D_MODEL)),
            pl.BlockSpec((D_MODEL, tf), lambda i, j: (0, j)),
            pl.BlockSpec((D_MODEL, tf), lambda i, j: (0, j)),
            pl.BlockSpec((tf, D_MODEL), lambda i, j: (j, 0)),
            full((1, D_MODEL)),
        ],
        out_specs=wide,
        out_shape=jax.ShapeDtypeStruct((m, D_MODEL), F32),
        scratch_shapes=[pltpu.VMEM((tm, D_MODEL), BF16), pltpu.VMEM((tm, D_MODEL), F32)],
        compiler_params=_params(("parallel", "arbitrary")),
        name="mix_ffn",
    )(x2, y_f, y_b, g, bg, att, p["lnx_w"], p["lnx_b"], p["head_ones"], p["wo_r"], p["wo_a"],
      p["norm2_g"], p["ffn_gate"], p["ffn_up"], p["ffn_down"], p["norm_f_g"])


def _rope_tables(T):
    n_rows = T // GRID_W
    t = jnp.arange(T, dtype=jnp.int32)
    row = (t // GRID_W).astype(F32)
    col = (t % GRID_W).astype(F32)
    inv = ROPE_THETA ** (-jnp.arange(ROPE_PAIRS, dtype=F32) / ROPE_PAIRS)
    ar = row[:, None] * inv
    ac = col[:, None] * inv
    cos = jnp.concatenate([jnp.cos(ar), jnp.cos(ar), jnp.cos(ac), jnp.cos(ac)], axis=1)
    sin = jnp.concatenate([-jnp.sin(ar), jnp.sin(ar), -jnp.sin(ac), jnp.sin(ac)], axis=1)
    del n_rows
    return jnp.tile(cos, (1, 2)), jnp.tile(sin, (1, 2))


def _block_diag2(a, b):
    za = jnp.zeros_like(a)
    return jnp.concatenate([jnp.concatenate([a, za], axis=1), jnp.concatenate([za, b], axis=1)], axis=0)


def _prepare_params(norm1_g, w_in, mu_prev, mu_next, k_k, k_a, r_k, w0_f, w_lora_f, w0_b, w_lora_b,
                    a0_f, a_lora_f, a0_b, a_lora_b, g_lora, lnx_w, lnx_b, q_gain, k_gain, w_out,
                    norm2_g, ffn_gate, ffn_up, ffn_down, norm_f_g):
    l = 0
    p = {}
    p["norm1_g"] = norm1_g[l][None]
    p["w_r"] = w_in[l][:, :RWKV_COLS].astype(BF16)
    p["w_a"] = w_in[l][:, RWKV_COLS:].astype(BF16)
    p["mu_prev"] = mu_prev[l][None]
    p["mu_next"] = mu_next[l][None]
    p["k_k"] = k_k[l][None]
    p["k_a"] = k_a[l][None]
    p["r_k"] = r_k[l].reshape(1, RWKV_WIDTH)
    p["w0"] = jnp.concatenate([w0_f[l], w0_b[l]])[None]
    p["w_lora"] = _block_diag2(w_lora_f[l], w_lora_b[l]).astype(BF16)
    p["a0"] = jnp.concatenate([a0_f[l], a0_b[l]])[None]
    p["a_lora"] = _block_diag2(a_lora_f[l], a_lora_b[l]).astype(BF16)
    p["g_lora"] = g_lora[l].astype(BF16)
    p["lnx_w"] = lnx_w[l][None]
    p["lnx_b"] = lnx_b[l][None]
    scale = HEAD_DIM ** -0.5 * float(np.log2(np.e))
    p["qk_gain"] = jnp.concatenate([jnp.tile(q_gain[l] * scale, ATT_Q_HEADS), jnp.tile(k_gain[l], ATT_KV_HEADS)])[None]
    hid = np.arange(640) // HEAD_DIM
    ones = (hid[:, None] == hid[None, :]).astype(np.float32)
    p["head_ones_qk"] = jnp.asarray(ones, BF16)
    p["head_ones"] = jnp.asarray(ones[:512, :512], BF16)
    p["wo_r"] = w_out[l][:RWKV_WIDTH].astype(BF16)
    p["wo_a"] = w_out[l][RWKV_WIDTH:].astype(BF16)
    p["norm2_g"] = norm2_g[l][None]
    p["ffn_gate"] = ffn_gate[l].astype(BF16)
    p["ffn_up"] = ffn_up[l].astype(BF16)
    p["ffn_down"] = ffn_down[l].astype(BF16)
    p["norm_f_g"] = norm_f_g[None]
    return p


def _tile(n, pref):
    t = pref
    while n % t:
        t //= 2
    return t


def _trunk(x, p):
    B, T, D = x.shape
    m = B * T
    x2 = x.reshape(m, D)
    z_r, z_a = _inproj(x2, p["norm1_g"], p["w_r"], p["w_a"], _tile(m, 512))
    v, ash, rsh, bsh, ksh, rfull, bkt, lend, g, bg = _rwkv_prep(z_r, T, p, _tile(T, 256))
    y_f, y_b = _rwkv_scan(v, ash, rsh, bsh, ksh, rfull, bkt, lend, B, T)
    cos_t, sin_t = _rope_tables(T)
    tkv = _tile(T, KV_CHUNK)
    q_pos = _tile(T, Q_POS_SHORT if tkv == T else Q_POS)
    qt, k, vt, qn, kn = _att_prep(z_a, B, T, p, cos_t, sin_t, q_pos)
    att = _attention(qt, qn, kn, k, vt, B, T, q_pos, tkv).reshape(m, ATT_WIDTH)
    out = _mix_ffn(x2, y_f, y_b, g, bg, att, p, _tile(m, 512), 1408)
    return out.reshape(B, T, D)


def kernel(x_prompt, x_sample, norm1_g, w_in, mu_prev, mu_next, k_k, k_a, r_k, w0_f, w_lora_f, w0_b, w_lora_b, a0_f, a_lora_f, a0_b, a_lora_b, g_lora, lnx_w, lnx_b, q_gain, k_gain, w_out, norm2_g, ffn_gate, ffn_up, ffn_down, norm_f_g):
    p = _prepare_params(norm1_g, w_in, mu_prev, mu_next, k_k, k_a, r_k, w0_f, w_lora_f, w0_b, w_lora_b,
                        a0_f, a_lora_f, a0_b, a_lora_b, g_lora, lnx_w, lnx_b, q_gain, k_gain, w_out,
                        norm2_g, ffn_gate, ffn_up, ffn_down, norm_f_g)
    return (_trunk(x_prompt, p), _trunk(x_sample, p))
```

```python
import functools

import jax
import jax.numpy as jnp
import numpy as np
from jax import lax
from jax.experimental import pallas as pl
from jax.experimental.pallas import tpu as pltpu

F32 = jnp.float32
BF16 = jnp.bfloat16

D_MODEL = 1024
HEAD_DIM = 64
RWKV_WIDTH = 512
RWKV_HEADS = 8
ATT_WIDTH = 512
ATT_Q_HEADS = 8
ATT_KV_HEADS = 2
ATT_GROUP = 4
KV_WIDTH = 128
RWKV_COLS = 1920
ATT_COLS = 768
D_FF = 2816
GRID_W = 64
ROPE_THETA = 10000.0
ROPE_PAIRS = 16
NORM_EPS = 1e-6
LNX_EPS = 64e-5

LANES = 128
SUBLANES = 8
CHUNK = 128
Q_POS = 256
Q_POS_SHORT = 512
KV_CHUNK = 2048
KV_UNROLL = 8
VT_ROWS = 80
DECAY_SCALE_LOG2 = float(np.exp(-0.5) * np.log2(np.e))
DENOM_FLOOR = 2.0 ** -100
VMEM_LIMIT = 56 * 1024 * 1024


def _dot(a, b):
    return jnp.dot(a, b, preferred_element_type=F32)


def _bf(x):
    return x.astype(BF16)


def _split3(x):
    hi = _bf(x)
    r1 = x - hi.astype(F32)
    mid = _bf(r1)
    lo = _bf(r1 - mid.astype(F32))
    return hi, mid, lo


def _head_sum(x, ones_bf):
    return _dot(_bf(x), ones_bf)


def _sigmoid(x):
    return 0.5 * jnp.tanh(0.5 * x) + 0.5


def _softplus(x):
    return jnp.maximum(x, 0.0) + jnp.log(1.0 + jnp.exp(-jnp.abs(x)))


def _params(sem):
    return pltpu.CompilerParams(dimension_semantics=sem, vmem_limit_bytes=VMEM_LIMIT)


def _inproj_kernel(x_ref, g_ref, wr_ref, wa_ref, zr_ref, za_ref):
    x = x_ref[...]
    ms = jnp.mean(x * x, axis=-1, keepdims=True)
    h = _bf(x * lax.rsqrt(ms + NORM_EPS) * g_ref[...])
    zr_ref[...] = _dot(h, wr_ref[...])
    za_ref[...] = _dot(h, wa_ref[...])


def _inproj(x2, norm1_g, w_r, w_a, tm):
    m = x2.shape[0]
    return pl.pallas_call(
        _inproj_kernel,
        grid=(m // tm,),
        in_specs=[
            pl.BlockSpec((tm, D_MODEL), lambda i: (i, 0)),
            pl.BlockSpec((1, D_MODEL), lambda i: (0, 0)),
            pl.BlockSpec((D_MODEL, RWKV_COLS), lambda i: (0, 0)),
            pl.BlockSpec((D_MODEL, ATT_COLS), lambda i: (0, 0)),
        ],
        out_specs=[
            pl.BlockSpec((tm, RWKV_COLS), lambda i: (i, 0)),
            pl.BlockSpec((tm, ATT_COLS), lambda i: (i, 0)),
        ],
        out_shape=[
            jax.ShapeDtypeStruct((m, RWKV_COLS), F32),
            jax.ShapeDtypeStruct((m, ATT_COLS), F32),
        ],
        compiler_params=_params(("parallel",)),
        name="inproj",
    )(x2, norm1_g, w_r, w_a)


def _rwkv_prep_kernel(z_ref, zp_ref, zn_ref, mup_ref, mun_ref, kk_ref, ka_ref, rk_ref,
                      w0_ref, wl_ref, a0_ref, al_ref, gl_ref,
                      hs_ref,
                      v_o, ash_o, rsh_o, bsh_o, ksh_o, rfull_o, bkt_o, lend_o, g_o, bg_o,
                      *, tm, blocks_per_seq):
    i = pl.program_id(0)
    pos = i % blocks_per_seq
    z = z_ref[...]
    mup, mun = mup_ref[...], mun_ref[...]
    ri = lax.broadcasted_iota(jnp.int32, (tm, tm), 0)
    ci = lax.broadcasted_iota(jnp.int32, (tm, tm), 1)
    shifts = jnp.concatenate([(ci == ri - 1).astype(BF16), (ci == ri + 1).astype(BF16)], axis=1)
    zf = z * (1.0 - mup - mun) + _dot(shifts, jnp.concatenate([_bf(z * mup), _bf(z * mun)], axis=0))
    prev_row = jnp.where(pos == 0, 0.0, zp_ref[SUBLANES - 1:SUBLANES, :]) * mup
    next_row = jnp.where(pos == blocks_per_seq - 1, 0.0, zn_ref[0:1, :]) * mun
    rows = lax.broadcasted_iota(jnp.int32, (SUBLANES, 1), 0)
    zf = jnp.concatenate([zf[:SUBLANES] + jnp.where(rows == 0, prev_row, 0.0),
                          zf[SUBLANES:tm - SUBLANES],
                          zf[tm - SUBLANES:] + jnp.where(rows == SUBLANES - 1, next_row, 0.0)], axis=0)

    r = zf[:, 0:512]
    k = zf[:, 512:1024]
    v = zf[:, 1024:1536]
    wd = zf[:, 1536:1664]
    ad = zf[:, 1664:1792]
    gd = zf[:, 1792:1920]
    hs = hs_ref[...]

    kk = k * kk_ref[...]
    ss = _head_sum(kk * kk, hs)
    kk = kk * lax.rsqrt(jnp.maximum(ss, 1e-12))

    lw_both = w0_ref[...] + _dot(_bf(jnp.tanh(wd)), wl_ref[...])
    as_both = a0_ref[...] + _dot(_bf(ad), al_ref[...])
    ka_half = 0.5 * ka_ref[...]
    same_chunk = (ri // CHUNK) == (ci // CHUNK)
    n_chunks = tm // CHUNK
    kd_sum = jnp.zeros_like(k)
    for d in range(2):
        lw = -DECAY_SCALE_LOG2 * _sigmoid(lw_both[:, 512 * d:512 * (d + 1)])
        th = jnp.tanh(0.5 * as_both[:, 512 * d:512 * (d + 1)])
        kd = k * (1.0 + (th - 1.0) * ka_half)
        b = kk * (0.5 * th + 0.5)
        kd_sum = kd_sum + kd
        processed = (ci <= ri) if d == 0 else (ci >= ri)
        tri = (same_chunk & processed).astype(BF16)
        hi, mid, lo = _split3(lw)
        cum = _dot(tri, hi) + _dot(tri, mid) + _dot(tri, lo)
        last = CHUNK - 1 if d == 0 else 0
        l_end = [cum[CHUNK * cc + last:CHUNK * cc + last + 1] for cc in range(n_chunks)]
        for cc in range(n_chunks):
            lend_o[d, cc] = jnp.broadcast_to(l_end[cc], (SUBLANES, RWKV_WIDTH))
        rows = lambda vals: jnp.concatenate([jnp.broadcast_to(x, (CHUNK, RWKV_WIDTH)) for x in vals], axis=0)
        l_half = rows([0.5 * x for x in l_end])
        e_half = rows([jnp.exp2(0.5 * x) for x in l_end])
        r_sh = r * jnp.exp2(cum - l_half)
        ash_o[d] = _bf(-kk * jnp.exp2(cum - lw - l_half))
        rsh_o[d] = _bf(r_sh)
        rfull_o[d] = _bf(r_sh * e_half)
        e_b = jnp.exp2(l_half - cum)
        b_sh = b * e_b
        k_sh = kd * e_b
        bsh_o[d] = _bf(b_sh)
        ksh_o[d] = _bf(k_sh)
        b_hat = b_sh * e_half
        k_hat = k_sh * e_half
        for cc in range(n_chunks):
            rs = slice(CHUNK * cc, CHUNK * (cc + 1))
            for p in range(RWKV_HEADS // 2):
                ls = slice(LANES * p, LANES * (p + 1))
                bkt_o[d, cc, ls, 0:CHUNK] = _bf(b_hat[rs, ls].T)
                bkt_o[d, cc, ls, CHUNK:2 * CHUNK] = _bf(k_hat[rs, ls].T)
    coef = _head_sum(r * kd_sum * (0.5 * rk_ref[...]), hs)
    g = _dot(_bf(_sigmoid(gd)), gl_ref[...])
    v_o[...] = _bf(v)
    g_o[...] = _bf(g)
    bg_o[...] = _bf(coef * v * g)


def _rwkv_prep(z_r, T, p, tm):
    m = z_r.shape[0]
    bps = T // tm
    hb = tm // SUBLANES
    nhalo = m // SUBLANES
    full = lambda shape: pl.BlockSpec(shape, lambda i: tuple(0 for _ in shape))
    tok = pl.BlockSpec((tm, RWKV_WIDTH), lambda i: (i, 0))
    tok2 = pl.BlockSpec((2, tm, RWKV_WIDTH), lambda i: (0, i, 0))
    cpb = tm // CHUNK
    nchunk = m // CHUNK
    tok2_shape = jax.ShapeDtypeStruct((2, m, RWKV_WIDTH), BF16)
    kern = functools.partial(_rwkv_prep_kernel, tm=tm, blocks_per_seq=bps)
    return pl.pallas_call(
        kern,
        grid=(m // tm,),
        in_specs=[
            pl.BlockSpec((tm, RWKV_COLS), lambda i: (i, 0)),
            pl.BlockSpec((SUBLANES, RWKV_COLS), lambda i: (jnp.maximum(i * hb - 1, 0), 0)),
            pl.BlockSpec((SUBLANES, RWKV_COLS), lambda i: (jnp.minimum((i + 1) * hb, nhalo - 1), 0)),
            full((1, RWKV_COLS)), full((1, RWKV_COLS)),
            full((1, 512)), full((1, 512)), full((1, 512)),
            full((1, 1024)), full((128, 1024)),
            full((1, 1024)), full((128, 1024)),
            full((128, 512)),
            full((512, 512)),
        ],
        out_specs=[
            tok, tok2, tok2, tok2, tok2, tok2,
            pl.BlockSpec((2, cpb, RWKV_WIDTH, 2 * CHUNK), lambda i: (0, i, 0, 0)),
            pl.BlockSpec((2, cpb, SUBLANES, RWKV_WIDTH), lambda i: (0, i, 0, 0)),
            tok, tok,
        ],
        out_shape=[
            jax.ShapeDtypeStruct((m, RWKV_WIDTH), BF16),
            tok2_shape, tok2_shape, tok2_shape, tok2_shape, tok2_shape,
            jax.ShapeDtypeStruct((2, nchunk, RWKV_WIDTH, 2 * CHUNK), BF16),
            jax.ShapeDtypeStruct((2, nchunk, SUBLANES, RWKV_WIDTH), F32),
            jax.ShapeDtypeStruct((m, RWKV_WIDTH), BF16),
            jax.ShapeDtypeStruct((m, RWKV_WIDTH), BF16),
        ],
        compiler_params=_params(("parallel",)),
        name="rwkv_prep",
    )(z_r, z_r, z_r, p["mu_prev"], p["mu_next"], p["k_k"], p["k_a"], p["r_k"],
      p["w0"], p["w_lora"], p["a0"], p["a_lora"], p["g_lora"],
      p["head_ones"])


def _scan_kernel(*refs):
    C = CHUNK
    c = pl.program_id(1)
    z_scr = refs[-1]

    @pl.when(c == 0)
    def _():
        z_scr[...] = jnp.zeros_like(z_scr)

    row = lax.broadcasted_iota(jnp.int32, (C, C), 0)
    col = lax.broadcasted_iota(jnp.int32, (C, C), 1)
    strict_d = [col < row, col > row]
    incl_d = [col <= row, col >= row]
    eye = (row == col).astype(F32)
    lane = lax.broadcasted_iota(jnp.int32, (1, LANES), 1)
    m0 = lane < HEAD_DIM
    m1 = lane >= HEAD_DIM
    blockdiag = (row < HEAD_DIM) == (col < HEAD_DIM)

    def both_heads(x):
        zero = jnp.zeros((), x.dtype)
        return jnp.concatenate([jnp.where(m0, x, zero), jnp.where(m1, x, zero)], axis=0)

    ppd = RWKV_HEADS // 2
    pairs = range(2 * ppd)
    heads = range(2 * RWKV_HEADS)
    n_in = (len(refs) - 3) // 2
    v, a_sh, r_sh, b_sh, k_sh, r_full, bkt_refs, e_half, p_end, strict, incl = ([] for _ in range(11))
    for pp in pairs:
        d, sl = pp // ppd, slice(LANES * (pp % ppd), LANES * (pp % ppd + 1))
        v_ref, ash_ref, rsh_ref, bsh_ref, ksh_ref, rfull_ref, bkt_ref, lend_ref = refs[n_in * d:n_in * (d + 1)]
        v.append(v_ref[:, sl])
        a_sh.append(ash_ref[0, :, sl])
        r_sh.append(rsh_ref[0, :, sl])
        b_sh.append(bsh_ref[0, :, sl])
        k_sh.append(ksh_ref[0, :, sl])
        r_full.append(rfull_ref[0, :, sl])
        bkt_refs.append((bkt_ref, sl))
        l_end = lend_ref[0, 0, 0:1, sl]
        e_half.append(jnp.exp2(0.5 * l_end))
        p_end.append(jnp.exp2(l_end))
        strict.append(strict_d[d])
        incl.append(incl_d[d])
    y_refs = refs[2 * n_in:2 * n_in + 2]

    amat = []
    for p in pairs:
        ar = jnp.concatenate([a_sh[p], r_sh[p]], axis=0)
        bk = jnp.concatenate([b_sh[p], k_sh[p]], axis=0)
        amat.append(lax.dot_general(both_heads(ar), bk, (((1,), (1,)), ((), ())),
                                    preferred_element_type=F32))
    n_bf, a_ak, a_rb, a_rk, t_inv = [], [], [], [], []
    for hd in heads:
        ah = amat[hd // 2][2 * C * (hd % 2):2 * C * (hd % 2 + 1)]
        n = jnp.where(strict[hd // 2], ah[:C, :C], 0.0)
        n_bf.append(_bf(n))
        t_inv.append(eye + n)
        a_ak.append(jnp.where(strict[hd // 2], ah[:C, C:], 0.0))
        a_rb.append(jnp.where(incl[hd // 2], ah[C:, :C], 0.0))
        a_rk.append(jnp.where(incl[hd // 2], ah[C:, C:], 0.0))
    def pair_products(lhs, rhs):
        out = []
        for p in pairs:
            a, b = rhs[2 * p], rhs[2 * p + 1]
            zero = jnp.zeros_like(a)
            diag = jnp.concatenate([jnp.concatenate([a, zero], axis=1),
                                    jnp.concatenate([zero, b], axis=1)], axis=0)
            prod = _dot(jnp.concatenate(lhs[2 * p:2 * p + 2], axis=1), diag)
            out += [prod[:, :C], prod[:, C:]]
        return out

    pw = [_bf(x) for x in pair_products(n_bf, n_bf)]
    avy = []
    for p in pairs:
        akrk = jnp.concatenate([jnp.concatenate(a_ak[2 * p:2 * p + 2], axis=1),
                                jnp.concatenate(a_rk[2 * p:2 * p + 2], axis=1)], axis=0)
        avy.append(_dot(_bf(akrk), both_heads(v[p])))
    for _ in range(5):
        both = [_dot(pw[hd], jnp.concatenate([pw[hd], _bf(t_inv[hd])], axis=1)) for hd in heads]
        pw = [_bf(both[hd][:, :C]) for hd in heads]
        t_inv = [t_inv[hd] + both[hd][:, C:] for hd in heads]
    last = pair_products(pw, [_bf(t) for t in t_inv])
    t_inv = [t_inv[hd] + last[hd] for hd in heads]

    aw = []
    for p in pairs:
        rhs = jnp.concatenate([both_heads(a_sh[p]), both_heads(_bf(avy[p][:C]))], axis=1)
        aw.append(_dot(_bf(jnp.concatenate(t_inv[2 * p:2 * p + 2], axis=1)), rhs))

    zs = [z_scr[p] for p in pairs]
    xs = []
    for q in range(0, 2 * ppd, 2):
        a_bar = jnp.concatenate([aw[q][:, :LANES] * e_half[q], aw[q + 1][:, :LANES] * e_half[q + 1]], axis=1)
        zero = jnp.zeros((LANES, LANES), BF16)
        z_diag = jnp.concatenate([jnp.concatenate([_bf(zs[q]), zero], axis=1),
                                  jnp.concatenate([zero, _bf(zs[q + 1])], axis=1)], axis=0)
        r_wide = jnp.concatenate([r_full[q], r_full[q + 1]], axis=1)
        x2 = _dot(jnp.concatenate([_bf(a_bar), r_wide], axis=0), z_diag)
        xs += [x2[:, :LANES], x2[:, LANES:]]
    us = [xs[p][:C] + aw[p][:, LANES:] for p in pairs]
    for p in pairs:
        y = xs[p][C:] + avy[p][C:] + _dot(_bf(jnp.concatenate(a_rb[2 * p:2 * p + 2], axis=1)),
                                           _bf(both_heads(us[p])))
        bkt_ref, sl = bkt_refs[p]
        y_refs[p // ppd][:, sl] = y
    for p in pairs:
        bkt_ref, sl = bkt_refs[p]
        bkt = bkt_ref[0, 0, sl, :]
        uv = jnp.concatenate([_bf(us[p]), v[p]], axis=0)
        pend_col = jnp.broadcast_to(p_end[p], (LANES, LANES)).T
        z_new = zs[p] * pend_col + _dot(bkt, uv)
        z_scr[p] = jnp.where(blockdiag, z_new, 0.0)


def _rwkv_scan(v, ash, rsh, bsh, ksh, rfull, bkt, lend, B, T):
    m = v.shape[0]
    nc = T // CHUNK

    def specs(d):
        def blk(bi, c):
            return bi * nc + c + d * (nc - 1 - 2 * c)

        tok = pl.BlockSpec((CHUNK, RWKV_WIDTH), lambda bi, c: (blk(bi, c), 0))
        tok2 = pl.BlockSpec((1, CHUNK, RWKV_WIDTH), lambda bi, c: (d, blk(bi, c), 0))
        return tok, [
            tok, tok2, tok2, tok2, tok2, tok2,
            pl.BlockSpec((1, 1, RWKV_WIDTH, 2 * CHUNK), lambda bi, c: (d, blk(bi, c), 0, 0)),
            pl.BlockSpec((1, 1, SUBLANES, RWKV_WIDTH), lambda bi, c: (d, blk(bi, c), 0, 0)),
        ]

    (out_f, in_f), (out_b, in_b) = specs(0), specs(1)
    operands = (v, ash, rsh, bsh, ksh, rfull, bkt, lend)
    y_shape = jax.ShapeDtypeStruct((m, RWKV_WIDTH), F32)
    return pl.pallas_call(
        _scan_kernel,
        grid=(B, nc),
        in_specs=in_f + in_b,
        out_specs=[out_f, out_b],
        out_shape=[y_shape, y_shape],
        scratch_shapes=[pltpu.VMEM((RWKV_HEADS, LANES, LANES), F32)],
        compiler_params=_params(("parallel", "arbitrary")),
        name="rwkv_scan",
    )(*operands, *operands)


def _att_prep_kernel(z_ref, gain_ref, cos_ref, sin_ref, hs_ref, qt_o, k_o, vt_o, qn_o, kn_o):
    i = pl.program_id(1)
    q_pos = z_ref.shape[0]
    z = z_ref[...]
    qk = z[:, :640]
    ss = _head_sum(qk * qk, hs_ref[...])
    qk = qk * lax.rsqrt(ss * (1.0 / HEAD_DIM) + NORM_EPS) * gain_ref[...]
    width = qk.shape[1]
    lane = lax.broadcasted_iota(jnp.int32, (1, width), 1)
    first = (lane % (2 * ROPE_PAIRS)) < ROPE_PAIRS
    partner = jnp.where(first, pltpu.roll(qk, width - ROPE_PAIRS, 1), pltpu.roll(qk, ROPE_PAIRS, 1))
    cos = jnp.concatenate([cos_ref[...]] * 5, axis=1)
    sin = jnp.concatenate([sin_ref[...]] * 5, axis=1)
    qk = qk * cos + partner * sin
    for j in range(4):
        st = _bf(qk[:, LANES * j:LANES * (j + 1)].T)
        h = j // 2
        g = (2 * j) % ATT_GROUP
        qt_o[0, h, :, q_pos * g:q_pos * (g + 1)] = st[:HEAD_DIM]
        qt_o[0, h, :, q_pos * (g + 1):q_pos * (g + 2)] = st[HEAD_DIM:]
        sq = st.astype(F32)
        sq = sq * sq
        for e in range(2):
            nrm = jnp.sum(sq[HEAD_DIM * e:HEAD_DIM * (e + 1)], axis=0, keepdims=True)
            qn_o[0, h, :, q_pos * (g + e):q_pos * (g + e + 1)] = jnp.broadcast_to(nrm, (SUBLANES, q_pos))
    kb = _bf(qk[:, 512:640])
    k_o[0] = kb
    kf = kb.astype(F32)
    kn = jnp.max(_head_sum(kf * kf, hs_ref[512:640, 512:640]), axis=0, keepdims=True)
    kn = jnp.broadcast_to(kn, (SUBLANES, KV_WIDTH))

    @pl.when(i == 0)
    def _():
        kn_o[0] = kn

    @pl.when(i > 0)
    def _():
        kn_o[0] = jnp.maximum(kn_o[0], kn)

    vt = _bf(z[:, 640:768].T)
    ones = jnp.ones((VT_ROWS - HEAD_DIM, q_pos), BF16)
    for h in range(ATT_KV_HEADS):
        vt_o[0, h, :HEAD_DIM, :] = vt[HEAD_DIM * h:HEAD_DIM * (h + 1)]
        vt_o[0, h, HEAD_DIM:, :] = ones


def _att_prep(z_a, B, T, p, cos_t, sin_t, q_pos):
    nb = T // q_pos
    full = lambda shape: pl.BlockSpec(shape, lambda bi, i: tuple(0 for _ in shape))
    return pl.pallas_call(
        _att_prep_kernel,
        grid=(B, nb),
        in_specs=[
            pl.BlockSpec((q_pos, ATT_COLS), lambda bi, i: (bi * nb + i, 0)),
            full((1, 640)),
            pl.BlockSpec((q_pos, LANES), lambda bi, i: (i, 0)),
            pl.BlockSpec((q_pos, LANES), lambda bi, i: (i, 0)),
            full((640, 640)),
        ],
        out_specs=[
            pl.BlockSpec((1, ATT_KV_HEADS, HEAD_DIM, ATT_GROUP * q_pos), lambda bi, i: (bi, 0, 0, i)),
            pl.BlockSpec((1, q_pos, KV_WIDTH), lambda bi, i: (bi, i, 0)),
            pl.BlockSpec((1, ATT_KV_HEADS, VT_ROWS, q_pos), lambda bi, i: (bi, 0, 0, i)),
            pl.BlockSpec((1, ATT_KV_HEADS, SUBLANES, ATT_GROUP * q_pos), lambda bi, i: (bi, 0, 0, i)),
            pl.BlockSpec((1, SUBLANES, KV_WIDTH), lambda bi, i: (bi, 0, 0)),
        ],
        out_shape=[
            jax.ShapeDtypeStruct((B, ATT_KV_HEADS, HEAD_DIM, ATT_GROUP * T), BF16),
            jax.ShapeDtypeStruct((B, T, KV_WIDTH), BF16),
            jax.ShapeDtypeStruct((B, ATT_KV_HEADS, VT_ROWS, T), BF16),
            jax.ShapeDtypeStruct((B, ATT_KV_HEADS, SUBLANES, ATT_GROUP * T), F32),
            jax.ShapeDtypeStruct((B, SUBLANES, KV_WIDTH), F32),
        ],
        compiler_params=_params(("parallel", "arbitrary")),
        name="att_prep",
    )(z_a, p["qk_gain"], cos_t, sin_t, p["head_ones_qk"])


def _attn_kernel(qt_ref, qn_ref, kn_ref, k_ref, vt_ref, o_ref, *, n_kv, tkv, unroll):
    h = pl.program_id(1)
    ncol = qt_ref.shape[-1]
    q_pos = ncol // ATT_GROUP
    qt = qt_ref[0, 0]
    rowh = lax.broadcasted_iota(jnp.int32, (KV_WIDTH, 1), 0) // HEAD_DIM
    q2 = jnp.where(rowh == h, jnp.concatenate([qt, qt], axis=0), jnp.zeros((), BF16))
    laneh = lax.broadcasted_iota(jnp.int32, (1, KV_WIDTH), 1) // HEAD_DIM
    kn = jnp.max(jnp.where(laneh == h, kn_ref[0, 0:1, :], 0.0), axis=1, keepdims=True)
    shift = jnp.sqrt(qn_ref[0, 0, 0:1, :] * kn)

    def chunk_start(j):
        return j * tkv if isinstance(j, int) else pl.multiple_of(j * tkv, tkv)

    def scores(j):
        kc = k_ref[0, pl.ds(chunk_start(j), tkv), :]
        return _dot(kc, q2)

    def pv(j, pt):
        vc = vt_ref[0, 0, :, pl.ds(chunk_start(j), tkv)]
        return _dot(vc, pt)

    def sweep(step, carry):
        def body(j, c):
            carry, s = c
            s_next = scores(j + 1)
            return step(j, carry, s), s_next

        s = scores(0)
        if n_kv > 1:
            carry, s = lax.fori_loop(0, n_kv - 1, body, (carry, s))
        return step(n_kv - 1, carry, s)

    def emit(acc):
        o = acc[:HEAD_DIM] / acc[HEAD_DIM:HEAD_DIM + 1]
        ot = jnp.concatenate([o, jnp.zeros_like(o)], axis=0).T
        for g in range(ATT_GROUP):
            o_ref[0, :, HEAD_DIM * g:HEAD_DIM * (g + 1)] = _bf(ot[q_pos * g:q_pos * (g + 1), :HEAD_DIM])

    def fast_group(base, acc):
        for u in range(unroll):
            acc = acc + pv(base + u, _bf(jnp.exp2(scores(base + u) - shift)))
        return acc

    acc0 = jnp.zeros((VT_ROWS, ncol), F32)
    n_groups = n_kv // unroll
    if n_groups > 1:
        acc = lax.fori_loop(0, n_groups, lambda j, acc: fast_group(j * unroll, acc), acc0)
    else:
        acc = fast_group(0, acc0)
    emit(acc)
    denom_ok = jnp.min(acc[HEAD_DIM:HEAD_DIM + 1]) >= DENOM_FLOOR

    @pl.when(jnp.logical_not(denom_ok))
    def _():
        def step(j, carry, s):
            m, acc = carry
            m_new = jnp.maximum(m, jnp.max(s, axis=0, keepdims=True))
            return m_new, acc * jnp.exp2(m - m_new) + pv(j, _bf(jnp.exp2(s - m_new)))

        _, acc_online = sweep(step, (jnp.full((1, ncol), -jnp.inf, F32), acc0))
        emit(acc_online)


def _attention(qt, qn, kn, k, vt, B, T, q_pos, tkv):
    nb = T // q_pos
    n_kv = T // tkv
    kern = functools.partial(_attn_kernel, n_kv=n_kv, tkv=tkv, unroll=_tile(n_kv, KV_UNROLL))
    return pl.pallas_call(
        kern,
        grid=(B, ATT_KV_HEADS, nb),
        in_specs=[
            pl.BlockSpec((1, 1, HEAD_DIM, ATT_GROUP * q_pos), lambda bi, h, i: (bi, h, 0, i)),
            pl.BlockSpec((1, 1, SUBLANES, ATT_GROUP * q_pos), lambda bi, h, i: (bi, h, 0, i)),
            pl.BlockSpec((1, SUBLANES, KV_WIDTH), lambda bi, h, i: (bi, 0, 0)),
            pl.BlockSpec((1, T, KV_WIDTH), lambda bi, h, i: (bi, 0, 0)),
            pl.BlockSpec((1, 1, VT_ROWS, T), lambda bi, h, i: (bi, h, 0, 0)),
        ],
        out_specs=pl.BlockSpec((1, q_pos, ATT_GROUP * HEAD_DIM), lambda bi, h, i: (bi, i, h)),
        out_shape=jax.ShapeDtypeStruct((B, T, ATT_WIDTH), BF16),
        compiler_params=_params(("parallel", "parallel", "arbitrary")),
        name="attention",
    )(qt, qn, kn, k, vt)


def _mix_ffn_kernel(x_ref, yf_ref, yb_ref, g_ref, bg_ref, att_ref, lnw_ref, lnb_ref, hs_ref, wo_r_ref, wo_a_ref,
                    g2_ref, wg_ref, wu_ref, wd_ref, gf_ref, o_ref, h_scr, acc_scr):
    j = pl.program_id(1)

    @pl.when(j == 0)
    def _():
        y = yf_ref[...] + yb_ref[...]
        hs = hs_ref[...]
        mu = _head_sum(y, hs) * (1.0 / HEAD_DIM)
        dy = y - mu
        var = _head_sum(dy * dy, hs) * (1.0 / HEAD_DIM)
        yn = dy * lax.rsqrt(var + LNX_EPS) * lnw_ref[...] + lnb_ref[...]
        yr = yn * g_ref[...].astype(F32) + bg_ref[...].astype(F32)
        x = x_ref[...] + _dot(_bf(yr), wo_r_ref[...]) + _dot(att_ref[...], wo_a_ref[...])
        ms = jnp.mean(x * x, axis=-1, keepdims=True)
        h_scr[...] = _bf(x * lax.rsqrt(ms + NORM_EPS) * g2_ref[...])
        acc_scr[...] = x

    h = h_scr[...]
    gate = _dot(h, wg_ref[...])
    up = _dot(h, wu_ref[...])
    act = gate * _sigmoid(gate) * up
    acc_scr[...] += _dot(_bf(act), wd_ref[...])

    @pl.when(j == pl.num_programs(1) - 1)
    def _():
        xo = acc_scr[...]
        ms = jnp.mean(xo * xo, axis=-1, keepdims=True)
        o_ref[...] = xo * lax.rsqrt(ms + NORM_EPS) * gf_ref[...]


def _mix_ffn(x2, y_f, y_b, g, bg, att, p, tm, tf):
    m = x2.shape[0]
    full = lambda shape: pl.BlockSpec(shape, lambda i, j: tuple(0 for _ in shape))
    tok = pl.BlockSpec((tm, RWKV_WIDTH), lambda i, j: (i, 0))
    wide = pl.BlockSpec((tm, D_MODEL), lambda i, j: (i, 0))
    return pl.pallas_call(
        _mix_ffn_kernel,
        grid=(m // tm, D_FF // tf),
        in_specs=[
            wide, tok, tok, tok, tok, tok,
            full((1, RWKV_WIDTH)), full((1, RWKV_WIDTH)), full((RWKV_WIDTH, RWKV_WIDTH)),
            full((RWKV_WIDTH, D_MODEL)), full((ATT_WIDTH, D_MODEL)),
            full((1, D_MODEL)),
            pl.BlockSpec((D_MODEL, tf), lambda i, j: (0, j)),
            pl.BlockSpec((D_MODEL, tf), lambda i, j: (0, j)),
            pl.BlockSpec((tf, D_MODEL), lambda i, j: (j, 0)),
            full((1, D_MODEL)),
        ],
        out_specs=wide,
        out_shape=jax.ShapeDtypeStruct((m, D_MODEL), F32),
        scratch_shapes=[pltpu.VMEM((tm, D_MODEL), BF16), pltpu.VMEM((tm, D_MODEL), F32)],
        compiler_params=_params(("parallel", "arbitrary")),
        name="mix_ffn",
    )(x2, y_f, y_b, g, bg, att, p["lnx_w"], p["lnx_b"], p["head_ones"], p["wo_r"], p["wo_a"],
      p["norm2_g"], p["ffn_gate"], p["ffn_up"], p["ffn_down"], p["norm_f_g"])


def _rope_tables(T):
    n_rows = T // GRID_W
    t = jnp.arange(T, dtype=jnp.int32)
    row = (t // GRID_W).astype(F32)
    col = (t % GRID_W).astype(F32)
    inv = ROPE_THETA ** (-jnp.arange(ROPE_PAIRS, dtype=F32) / ROPE_PAIRS)
    ar = row[:, None] * inv
    ac = col[:, None] * inv
    cos = jnp.concatenate([jnp.cos(ar), jnp.cos(ar), jnp.cos(ac), jnp.cos(ac)], axis=1)
    sin = jnp.concatenate([-jnp.sin(ar), jnp.sin(ar), -jnp.sin(ac), jnp.sin(ac)], axis=1)
    del n_rows
    return jnp.tile(cos, (1, 2)), jnp.tile(sin, (1, 2))


def _block_diag2(a, b):
    za = jnp.zeros_like(a)
    return jnp.concatenate([jnp.concatenate([a, za], axis=1), jnp.concatenate([za, b], axis=1)], axis=0)


def _prepare_params(norm1_g, w_in, mu_prev, mu_next, k_k, k_a, r_k, w0_f, w_lora_f, w0_b, w_lora_b,
                    a0_f, a_lora_f, a0_b, a_lora_b, g_lora, lnx_w, lnx_b, q_gain, k_gain, w_out,
                    norm2_g, ffn_gate, ffn_up, ffn_down, norm_f_g):
    l = 0
    p = {}
    p["norm1_g"] = norm1_g[l][None]
    p["w_r"] = w_in[l][:, :RWKV_COLS].astype(BF16)
    p["w_a"] = w_in[l][:, RWKV_COLS:].astype(BF16)
    p["mu_prev"] = mu_prev[l][None]
    p["mu_next"] = mu_next[l][None]
    p["k_k"] = k_k[l][None]
    p["k_a"] = k_a[l][None]
    p["r_k"] = r_k[l].reshape(1, RWKV_WIDTH)
    p["w0"] = jnp.concatenate([w0_f[l], w0_b[l]])[None]
    p["w_lora"] = _block_diag2(w_lora_f[l], w_lora_b[l]).astype(BF16)
    p["a0"] = jnp.concatenate([a0_f[l], a0_b[l]])[None]
    p["a_lora"] = _block_diag2(a_lora_f[l], a_lora_b[l]).astype(BF16)
    p["g_lora"] = g_lora[l].astype(BF16)
    p["lnx_w"] = lnx_w[l][None]
    p["lnx_b"] = lnx_b[l][None]
    scale = HEAD_DIM ** -0.5 * float(np.log2(np.e))
    p["qk_gain"] = jnp.concatenate([jnp.tile(q_gain[l] * scale, ATT_Q_HEADS), jnp.tile(k_gain[l], ATT_KV_HEADS)])[None]
    hid = np.arange(640) // HEAD_DIM
    ones = (hid[:, None] == hid[None, :]).astype(np.float32)
    p["head_ones_qk"] = jnp.asarray(ones, BF16)
    p["head_ones"] = jnp.asarray(ones[:512, :512], BF16)
    p["wo_r"] = w_out[l][:RWKV_WIDTH].astype(BF16)
    p["wo_a"] = w_out[l][RWKV_WIDTH:].astype(BF16)
    p["norm2_g"] = norm2_g[l][None]
    p["ffn_gate"] = ffn_gate[l].astype(BF16)
    p["ffn_up"] = ffn_up[l].astype(BF16)
    p["ffn_down"] = ffn_down[l].astype(BF16)
    p["norm_f_g"] = norm_f_g[None]
    return p


def _tile(n, pref):
    t = pref
    while n % t:
        t //= 2
    return t


def _trunk(x, p):
    B, T, D = x.shape
    m = B * T
    x2 = x.reshape(m, D)
    z_r, z_a = _inproj(x2, p["norm1_g"], p["w_r"], p["w_a"], _tile(m, 512))
    v, ash, rsh, bsh, ksh, rfull, bkt, lend, g, bg = _rwkv_prep(z_r, T, p, _tile(T, 256))
    y_f, y_b = _rwkv_scan(v, ash, rsh, bsh, ksh, rfull, bkt, lend, B, T)
    cos_t, sin_t = _rope_tables(T)
    tkv = _tile(T, KV_CHUNK)
    q_pos = _tile(T, Q_POS_SHORT if tkv == T else Q_POS)
    qt, k, vt, qn, kn = _att_prep(z_a, B, T, p, cos_t, sin_t, q_pos)
    att = _attention(qt, qn, kn, k, vt, B, T, q_pos, tkv).reshape(m, ATT_WIDTH)
    out = _mix_ffn(x2, y_f, y_b, g, bg, att, p, _tile(m, 512), 1408)
    return out.reshape(B, T, D)


def kernel(x_prompt, x_sample, norm1_g, w_in, mu_prev, mu_next, k_k, k_a, r_k, w0_f, w_lora_f, w0_b, w_lora_b, a0_f, a_lora_f, a0_b, a_lora_b, g_lora, lnx_w, lnx_b, q_gain, k_gain, w_out, norm2_g, ffn_gate, ffn_up, ffn_down, norm_f_g):
    p = _prepare_params(norm1_g, w_in, mu_prev, mu_next, k_k, k_a, r_k, w0_f, w_lora_f, w0_b, w_lora_b,
                        a0_f, a_lora_f, a0_b, a_lora_b, g_lora, lnx_w, lnx_b, q_gain, k_gain, w_out,
                        norm2_g, ffn_gate, ffn_up, ffn_down, norm_f_g)
    return (_trunk(x_prompt, p), _trunk(x_sample, p))
```

```python
import functools

import jax
import jax.numpy as jnp
import numpy as np
from jax import lax
from jax.experimental import pallas as pl
from jax.experimental.pallas import tpu as pltpu

F32 = jnp.float32
BF16 = jnp.bfloat16

D_MODEL = 1024
HEAD_DIM = 64
RWKV_WIDTH = 512
RWKV_HEADS = 8
ATT_WIDTH = 512
ATT_Q_HEADS = 8
ATT_KV_HEADS = 2
ATT_GROUP = 4
KV_WIDTH = 128
RWKV_COLS = 1920
ATT_COLS = 768
D_FF = 2816
GRID_W = 64
ROPE_THETA = 10000.0
ROPE_PAIRS = 16
NORM_EPS = 1e-6
LNX_EPS = 64e-5

LANES = 128
SUBLANES = 8
CHUNK = 128
Q_POS = 256
Q_POS_SHORT = 512
KV_CHUNK = 2048
KV_UNROLL = 8
VT_ROWS = 80
DECAY_SCALE_LOG2 = float(np.exp(-0.5) * np.log2(np.e))
DENOM_FLOOR = 2.0 ** -100
VMEM_LIMIT = 56 * 1024 * 1024


def _dot(a, b):
    return jnp.dot(a, b, preferred_element_type=F32)


def _bf(x):
    return x.astype(BF16)


def _split3(x):
    hi = _bf(x)
    r1 = x - hi.astype(F32)
    mid = _bf(r1)
    lo = _bf(r1 - mid.astype(F32))
    return hi, mid, lo


def _head_sum(x, ones_bf):
    return _dot(_bf(x), ones_bf)


def _sigmoid(x):
    return 0.5 * jnp.tanh(0.5 * x) + 0.5


def _softplus(x):
    return jnp.maximum(x, 0.0) + jnp.log(1.0 + jnp.exp(-jnp.abs(x)))


def _params(sem):
    return pltpu.CompilerParams(dimension_semantics=sem, vmem_limit_bytes=VMEM_LIMIT)


def _inproj_kernel(x_ref, g_ref, wr_ref, wa_ref, gain_ref, cos_ref, sin_ref, hs_ref,
                   zr_ref, qt_o, k_o, vt_o, qn_o, kn_o, *, q_pos, blocks_per_seq):
    x = x_ref[...]
    ms = jnp.mean(x * x, axis=-1, keepdims=True)
    h = _bf(x * lax.rsqrt(ms + NORM_EPS) * g_ref[...])
    z = _dot(h, wa_ref[...])
    ss = _head_sum(z[:, :640] * z[:, :640], hs_ref[...])
    zr_ref[...] = _dot(h, wr_ref[...])
    first_block = pl.program_id(0) % blocks_per_seq == 0
    _att_operands(z, ss, gain_ref, cos_ref, sin_ref, hs_ref, qt_o, k_o, vt_o, qn_o, kn_o, first_block, q_pos)


def _inproj(x2, B, T, p, cos_t, sin_t, tm, q_pos):
    m = x2.shape[0]
    bps = T // tm
    full = lambda shape: pl.BlockSpec(shape, lambda i: tuple(0 for _ in shape))
    kern = functools.partial(_inproj_kernel, q_pos=q_pos, blocks_per_seq=bps)
    return pl.pallas_call(
        kern,
        grid=(m // tm,),
        in_specs=[
            pl.BlockSpec((tm, D_MODEL), lambda i: (i, 0)),
            full((1, D_MODEL)), full((D_MODEL, RWKV_COLS)), full((D_MODEL, ATT_COLS)),
            full((1, 640)),
            pl.BlockSpec((tm, LANES), lambda i: (i % bps, 0)),
            pl.BlockSpec((tm, LANES), lambda i: (i % bps, 0)),
            full((640, 640)),
        ],
        out_specs=[
            pl.BlockSpec((tm, RWKV_COLS), lambda i: (i, 0)),
            pl.BlockSpec((1, ATT_KV_HEADS, HEAD_DIM, ATT_GROUP * tm), lambda i: (i // bps, 0, 0, i % bps)),
            pl.BlockSpec((1, tm, KV_WIDTH), lambda i: (i // bps, i % bps, 0)),
            pl.BlockSpec((1, ATT_KV_HEADS, VT_ROWS, tm), lambda i: (i // bps, 0, 0, i % bps)),
            pl.BlockSpec((1, ATT_KV_HEADS, SUBLANES, ATT_GROUP * tm), lambda i: (i // bps, 0, 0, i % bps)),
            pl.BlockSpec((1, SUBLANES, KV_WIDTH), lambda i: (i // bps, 0, 0)),
        ],
        out_shape=[
            jax.ShapeDtypeStruct((m, RWKV_COLS), F32),
            jax.ShapeDtypeStruct((B, ATT_KV_HEADS, HEAD_DIM, ATT_GROUP * T), BF16),
            jax.ShapeDtypeStruct((B, T, KV_WIDTH), BF16),
            jax.ShapeDtypeStruct((B, ATT_KV_HEADS, VT_ROWS, T), BF16),
            jax.ShapeDtypeStruct((B, ATT_KV_HEADS, SUBLANES, ATT_GROUP * T), F32),
            jax.ShapeDtypeStruct((B, SUBLANES, KV_WIDTH), F32),
        ],
        compiler_params=_params(("arbitrary",)),
        name="inproj",
    )(x2, p["norm1_g"], p["w_r"], p["w_a"], p["qk_gain"], cos_t, sin_t, p["head_ones_qk"])


def _rwkv_prep_kernel(z_ref, zp_ref, zn_ref, mup_ref, mun_ref, kk_ref, ka_ref, rk_ref,
                      w0_ref, wl_ref, a0_ref, al_ref, gl_ref,
                      hs_ref,
                      v_o, ash_o, rsh_o, bsh_o, ksh_o, rfull_o, bkt_o, lend_o, g_o, bg_o,
                      *, tm, blocks_per_seq):
    i = pl.program_id(0)
    pos = i % blocks_per_seq
    z = z_ref[...]
    mup, mun = mup_ref[...], mun_ref[...]
    ri = lax.broadcasted_iota(jnp.int32, (tm, tm), 0)
    ci = lax.broadcasted_iota(jnp.int32, (tm, tm), 1)
    shifts = jnp.concatenate([(ci == ri - 1).astype(BF16), (ci == ri + 1).astype(BF16)], axis=1)
    zf = z * (1.0 - mup - mun) + _dot(shifts, jnp.concatenate([_bf(z * mup), _bf(z * mun)], axis=0))
    prev_row = jnp.where(pos == 0, 0.0, zp_ref[SUBLANES - 1:SUBLANES, :]) * mup
    next_row = jnp.where(pos == blocks_per_seq - 1, 0.0, zn_ref[0:1, :]) * mun
    rows = lax.broadcasted_iota(jnp.int32, (SUBLANES, 1), 0)
    zf = jnp.concatenate([zf[:SUBLANES] + jnp.where(rows == 0, prev_row, 0.0),
                          zf[SUBLANES:tm - SUBLANES],
                          zf[tm - SUBLANES:] + jnp.where(rows == SUBLANES - 1, next_row, 0.0)], axis=0)

    r = zf[:, 0:512]
    k = zf[:, 512:1024]
    v = zf[:, 1024:1536]
    wd = zf[:, 1536:1664]
    ad = zf[:, 1664:1792]
    gd = zf[:, 1792:1920]
    hs = hs_ref[...]

    kk = k * kk_ref[...]
    ss = _head_sum(kk * kk, hs)
    kk = kk * lax.rsqrt(jnp.maximum(ss, 1e-12))

    lw_both = w0_ref[...] + _dot(_bf(jnp.tanh(wd)), wl_ref[...])
    as_both = a0_ref[...] + _dot(_bf(ad), al_ref[...])
    ka_half = 0.5 * ka_ref[...]
    same_chunk = (ri // CHUNK) == (ci // CHUNK)
    n_chunks = tm // CHUNK
    kd_sum = jnp.zeros_like(k)
    for d in range(2):
        lw = -DECAY_SCALE_LOG2 * _sigmoid(lw_both[:, 512 * d:512 * (d + 1)])
        th = jnp.tanh(0.5 * as_both[:, 512 * d:512 * (d + 1)])
        kd = k * (1.0 + (th - 1.0) * ka_half)
        b = kk * (0.5 * th + 0.5)
        kd_sum = kd_sum + kd
        processed = (ci <= ri) if d == 0 else (ci >= ri)
        tri = (same_chunk & processed).astype(BF16)
        hi, mid, lo = _split3(lw)
        cum = _dot(tri, hi) + _dot(tri, mid) + _dot(tri, lo)
        last = CHUNK - 1 if d == 0 else 0
        l_end = [cum[CHUNK * cc + last:CHUNK * cc + last + 1] for cc in range(n_chunks)]
        for cc in range(n_chunks):
            lend_o[d, cc] = jnp.broadcast_to(l_end[cc], (SUBLANES, RWKV_WIDTH))
        rows = lambda vals: jnp.concatenate([jnp.broadcast_to(x, (CHUNK, RWKV_WIDTH)) for x in vals], axis=0)
        l_half = rows([0.5 * x for x in l_end])
        e_half = rows([jnp.exp2(0.5 * x) for x in l_end])
        r_sh = r * jnp.exp2(cum - l_half)
        ash_o[d] = _bf(-kk * jnp.exp2(cum - lw - l_half))
        rsh_o[d] = _bf(r_sh)
        rfull_o[d] = _bf(r_sh * e_half)
        e_b = jnp.exp2(l_half - cum)
        b_sh = b * e_b
        k_sh = kd * e_b
        bsh_o[d] = _bf(b_sh)
        ksh_o[d] = _bf(k_sh)
        b_hat = b_sh * e_half
        k_hat = k_sh * e_half
        for cc in range(n_chunks):
            rs = slice(CHUNK * cc, CHUNK * (cc + 1))
            for p in range(RWKV_HEADS // 2):
                ls = slice(LANES * p, LANES * (p + 1))
                bkt_o[d, cc, ls, 0:CHUNK] = _bf(b_hat[rs, ls].T)
                bkt_o[d, cc, ls, CHUNK:2 * CHUNK] = _bf(k_hat[rs, ls].T)
    coef = _head_sum(r * kd_sum * (0.5 * rk_ref[...]), hs)
    g = _dot(_bf(_sigmoid(gd)), gl_ref[...])
    v_o[...] = _bf(v)
    g_o[...] = _bf(g)
    bg_o[...] = _bf(coef * v * g)


def _rwkv_prep(z_r, T, p, tm):
    m = z_r.shape[0]
    bps = T // tm
    hb = tm // SUBLANES
    nhalo = m // SUBLANES
    full = lambda shape: pl.BlockSpec(shape, lambda i: tuple(0 for _ in shape))
    tok = pl.BlockSpec((tm, RWKV_WIDTH), lambda i: (i, 0))
    tok2 = pl.BlockSpec((2, tm, RWKV_WIDTH), lambda i: (0, i, 0))
    cpb = tm // CHUNK
    nchunk = m // CHUNK
    tok2_shape = jax.ShapeDtypeStruct((2, m, RWKV_WIDTH), BF16)
    kern = functools.partial(_rwkv_prep_kernel, tm=tm, blocks_per_seq=bps)
    return pl.pallas_call(
        kern,
        grid=(m // tm,),
        in_specs=[
            pl.BlockSpec((tm, RWKV_COLS), lambda i: (i, 0)),
            pl.BlockSpec((SUBLANES, RWKV_COLS), lambda i: (jnp.maximum(i * hb - 1, 0), 0)),
            pl.BlockSpec((SUBLANES, RWKV_COLS), lambda i: (jnp.minimum((i + 1) * hb, nhalo - 1), 0)),
            full((1, RWKV_COLS)), full((1, RWKV_COLS)),
            full((1, 512)), full((1, 512)), full((1, 512)),
            full((1, 1024)), full((128, 1024)),
            full((1, 1024)), full((128, 1024)),
            full((128, 512)),
            full((512, 512)),
        ],
        out_specs=[
            tok, tok2, tok2, tok2, tok2, tok2,
            pl.BlockSpec((2, cpb, RWKV_WIDTH, 2 * CHUNK), lambda i: (0, i, 0, 0)),
            pl.BlockSpec((2, cpb, SUBLANES, RWKV_WIDTH), lambda i: (0, i, 0, 0)),
            tok, tok,
        ],
        out_shape=[
            jax.ShapeDtypeStruct((m, RWKV_WIDTH), BF16),
            tok2_shape, tok2_shape, tok2_shape, tok2_shape, tok2_shape,
            jax.ShapeDtypeStruct((2, nchunk, RWKV_WIDTH, 2 * CHUNK), BF16),
            jax.ShapeDtypeStruct((2, nchunk, SUBLANES, RWKV_WIDTH), F32),
            jax.ShapeDtypeStruct((m, RWKV_WIDTH), BF16),
            jax.ShapeDtypeStruct((m, RWKV_WIDTH), BF16),
        ],
        compiler_params=_params(("parallel",)),
        name="rwkv_prep",
    )(z_r, z_r, z_r, p["mu_prev"], p["mu_next"], p["k_k"], p["k_a"], p["r_k"],
      p["w0"], p["w_lora"], p["a0"], p["a_lora"], p["g_lora"],
      p["head_ones"])


def _scan_kernel(*refs):
    C = CHUNK
    c = pl.program_id(1)
    z_scr = refs[-1]

    @pl.when(c == 0)
    def _():
        z_scr[...] = jnp.zeros_like(z_scr)

    row = lax.broadcasted_iota(jnp.int32, (C, C), 0)
    col = lax.broadcasted_iota(jnp.int32, (C, C), 1)
    strict_d = [col < row, col > row]
    incl_d = [col <= row, col >= row]
    eye = (row == col).astype(F32)
    lane = lax.broadcasted_iota(jnp.int32, (1, LANES), 1)
    m0 = lane < HEAD_DIM
    m1 = lane >= HEAD_DIM
    blockdiag = (row < HEAD_DIM) == (col < HEAD_DIM)

    def both_heads(x):
        zero = jnp.zeros((), x.dtype)
        return jnp.concatenate([jnp.where(m0, x, zero), jnp.where(m1, x, zero)], axis=0)

    ppd = RWKV_HEADS // 2
    pairs = range(2 * ppd)
    heads = range(2 * RWKV_HEADS)
    n_in = (len(refs) - 3) // 2
    v, a_sh, r_sh, b_sh, k_sh, r_full, bkt_refs, e_half, p_end, strict, incl = ([] for _ in range(11))
    for pp in pairs:
        d, sl = pp // ppd, slice(LANES * (pp % ppd), LANES * (pp % ppd + 1))
        v_ref, ash_ref, rsh_ref, bsh_ref, ksh_ref, rfull_ref, bkt_ref, lend_ref = refs[n_in * d:n_in * (d + 1)]
        v.append(v_ref[:, sl])
        a_sh.append(ash_ref[0, :, sl])
        r_sh.append(rsh_ref[0, :, sl])
        b_sh.append(bsh_ref[0, :, sl])
        k_sh.append(ksh_ref[0, :, sl])
        r_full.append(rfull_ref[0, :, sl])
        bkt_refs.append((bkt_ref, sl))
        l_end = lend_ref[0, 0, 0:1, sl]
        e_half.append(jnp.exp2(0.5 * l_end))
        p_end.append(jnp.exp2(l_end))
        strict.append(strict_d[d])
        incl.append(incl_d[d])
    y_refs = refs[2 * n_in:2 * n_in + 2]

    amat = []
    for p in pairs:
        ar = jnp.concatenate([a_sh[p], r_sh[p]], axis=0)
        bk = jnp.concatenate([b_sh[p], k_sh[p]], axis=0)
        amat.append(lax.dot_general(both_heads(ar), bk, (((1,), (1,)), ((), ())),
                                    preferred_element_type=F32))
    n_bf, a_ak, a_rb, a_rk, t_inv = [], [], [], [], []
    for hd in heads:
        ah = amat[hd // 2][2 * C * (hd % 2):2 * C * (hd % 2 + 1)]
        n = jnp.where(strict[hd // 2], ah[:C, :C], 0.0)
        n_bf.append(_bf(n))
        t_inv.append(eye + n)
        a_ak.append(jnp.where(strict[hd // 2], ah[:C, C:], 0.0))
        a_rb.append(jnp.where(incl[hd // 2], ah[C:, :C], 0.0))
        a_rk.append(jnp.where(incl[hd // 2], ah[C:, C:], 0.0))
    def pair_products(lhs, rhs):
        out = []
        for p in pairs:
            a, b = rhs[2 * p], rhs[2 * p + 1]
            zero = jnp.zeros_like(a)
            diag = jnp.concatenate([jnp.concatenate([a, zero], axis=1),
                                    jnp.concatenate([zero, b], axis=1)], axis=0)
            prod = _dot(jnp.concatenate(lhs[2 * p:2 * p + 2], axis=1), diag)
            out += [prod[:, :C], prod[:, C:]]
        return out

    pw = [_bf(x) for x in pair_products(n_bf, n_bf)]
    avy = []
    for p in pairs:
        akrk = jnp.concatenate([jnp.concatenate(a_ak[2 * p:2 * p + 2], axis=1),
                                jnp.concatenate(a_rk[2 * p:2 * p + 2], axis=1)], axis=0)
        avy.append(_dot(_bf(akrk), both_heads(v[p])))
    for _ in range(5):
        both = [_dot(pw[hd], jnp.concatenate([pw[hd], _bf(t_inv[hd])], axis=1)) for hd in heads]
        pw = [_bf(both[hd][:, :C]) for hd in heads]
        t_inv = [t_inv[hd] + both[hd][:, C:] for hd in heads]
    last = pair_products(pw, [_bf(t) for t in t_inv])
    t_inv = [t_inv[hd] + last[hd] for hd in heads]

    aw = []
    for p in pairs:
        rhs = jnp.concatenate([both_heads(a_sh[p]), both_heads(_bf(avy[p][:C]))], axis=1)
        aw.append(_dot(_bf(jnp.concatenate(t_inv[2 * p:2 * p + 2], axis=1)), rhs))

    zs = [z_scr[p] for p in pairs]
    xs = []
    for q in range(0, 2 * ppd, 2):
        a_bar = jnp.concatenate([aw[q][:, :LANES] * e_half[q], aw[q + 1][:, :LANES] * e_half[q + 1]], axis=1)
        zero = jnp.zeros((LANES, LANES), BF16)
        z_diag = jnp.concatenate([jnp.concatenate([_bf(zs[q]), zero], axis=1),
                                  jnp.concatenate([zero, _bf(zs[q + 1])], axis=1)], axis=0)
        r_wide = jnp.concatenate([r_full[q], r_full[q + 1]], axis=1)
        x2 = _dot(jnp.concatenate([_bf(a_bar), r_wide], axis=0), z_diag)
        xs += [x2[:, :LANES], x2[:, LANES:]]
    us = [xs[p][:C] + aw[p][:, LANES:] for p in pairs]
    for p in pairs:
        y = xs[p][C:] + avy[p][C:] + _dot(_bf(jnp.concatenate(a_rb[2 * p:2 * p + 2], axis=1)),
                                           _bf(both_heads(us[p])))
        bkt_ref, sl = bkt_refs[p]
        y_refs[p // ppd][:, sl] = y
    for p in pairs:
        bkt_ref, sl = bkt_refs[p]
        bkt = bkt_ref[0, 0, sl, :]
        uv = jnp.concatenate([_bf(us[p]), v[p]], axis=0)
        pend_col = jnp.broadcast_to(p_end[p], (LANES, LANES)).T
        z_new = zs[p] * pend_col + _dot(bkt, uv)
        z_scr[p] = jnp.where(blockdiag, z_new, 0.0)


def _rwkv_scan(v, ash, rsh, bsh, ksh, rfull, bkt, lend, B, T):
    m = v.shape[0]
    nc = T // CHUNK

    def specs(d):
        def blk(bi, c):
            return bi * nc + c + d * (nc - 1 - 2 * c)

        tok = pl.BlockSpec((CHUNK, RWKV_WIDTH), lambda bi, c: (blk(bi, c), 0))
        tok2 = pl.BlockSpec((1, CHUNK, RWKV_WIDTH), lambda bi, c: (d, blk(bi, c), 0))
        return tok, [
            tok, tok2, tok2, tok2, tok2, tok2,
            pl.BlockSpec((1, 1, RWKV_WIDTH, 2 * CHUNK), lambda bi, c: (d, blk(bi, c), 0, 0)),
            pl.BlockSpec((1, 1, SUBLANES, RWKV_WIDTH), lambda bi, c: (d, blk(bi, c), 0, 0)),
        ]

    (out_f, in_f), (out_b, in_b) = specs(0), specs(1)
    operands = (v, ash, rsh, bsh, ksh, rfull, bkt, lend)
    y_shape = jax.ShapeDtypeStruct((m, RWKV_WIDTH), F32)
    return pl.pallas_call(
        _scan_kernel,
        grid=(B, nc),
        in_specs=in_f + in_b,
        out_specs=[out_f, out_b],
        out_shape=[y_shape, y_shape],
        scratch_shapes=[pltpu.VMEM((RWKV_HEADS, LANES, LANES), F32)],
        compiler_params=_params(("parallel", "arbitrary")),
        name="rwkv_scan",
    )(*operands, *operands)


def _att_operands(z, ss, gain_ref, cos_ref, sin_ref, hs_ref, qt_o, k_o, vt_o, qn_o, kn_o, first_block, q_pos):
    tm = z.shape[0]
    qk = z[:, :640]
    qk = qk * lax.rsqrt(ss * (1.0 / HEAD_DIM) + NORM_EPS) * gain_ref[...]
    width = qk.shape[1]
    lane = lax.broadcasted_iota(jnp.int32, (1, width), 1)
    first = (lane % (2 * ROPE_PAIRS)) < ROPE_PAIRS
    partner = jnp.where(first, pltpu.roll(qk, width - ROPE_PAIRS, 1), pltpu.roll(qk, ROPE_PAIRS, 1))
    cos = jnp.concatenate([cos_ref[...]] * 5, axis=1)
    sin = jnp.concatenate([sin_ref[...]] * 5, axis=1)
    qk = qk * cos + partner * sin
    for sb in range(tm // q_pos):
        rows = slice(q_pos * sb, q_pos * (sb + 1))
        for j in range(4):
            st = _bf(qk[rows, LANES * j:LANES * (j + 1)].T)
            h = j // 2
            c0 = q_pos * (ATT_GROUP * sb + (2 * j) % ATT_GROUP)
            qt_o[0, h, :, c0:c0 + q_pos] = st[:HEAD_DIM]
            qt_o[0, h, :, c0 + q_pos:c0 + 2 * q_pos] = st[HEAD_DIM:]
            sq = st.astype(F32)
            sq = sq * sq
            for e in range(2):
                nrm = jnp.sum(sq[HEAD_DIM * e:HEAD_DIM * (e + 1)], axis=0, keepdims=True)
                qn_o[0, h, :, c0 + q_pos * e:c0 + q_pos * (e + 1)] = jnp.broadcast_to(nrm, (SUBLANES, q_pos))
    kb = _bf(qk[:, 512:640])
    k_o[0] = kb
    kf = kb.astype(F32)
    kn = jnp.max(_head_sum(kf * kf, hs_ref[512:640, 512:640]), axis=0, keepdims=True)
    kn = jnp.broadcast_to(kn, (SUBLANES, KV_WIDTH))

    @pl.when(first_block)
    def _():
        kn_o[0] = kn

    @pl.when(jnp.logical_not(first_block))
    def _():
        kn_o[0] = jnp.maximum(kn_o[0], kn)

    vt = _bf(z[:, 640:768].T)
    ones = jnp.ones((VT_ROWS - HEAD_DIM, tm), BF16)
    for h in range(ATT_KV_HEADS):
        vt_o[0, h, :HEAD_DIM, :] = vt[HEAD_DIM * h:HEAD_DIM * (h + 1)]
        vt_o[0, h, HEAD_DIM:, :] = ones


def _attn_kernel(qt_ref, qn_ref, kn_ref, k_ref, vt_ref, o_ref, *, n_kv, tkv, unroll):
    h = pl.program_id(1)
    ncol = qt_ref.shape[-1]
    q_pos = ncol // ATT_GROUP
    qt = qt_ref[0, 0]
    rowh = lax.broadcasted_iota(jnp.int32, (KV_WIDTH, 1), 0) // HEAD_DIM
    q2 = jnp.where(rowh == h, jnp.concatenate([qt, qt], axis=0), jnp.zeros((), BF16))
    laneh = lax.broadcasted_iota(jnp.int32, (1, KV_WIDTH), 1) // HEAD_DIM
    kn = jnp.max(jnp.where(laneh == h, kn_ref[0, 0:1, :], 0.0), axis=1, keepdims=True)
    shift = jnp.sqrt(qn_ref[0, 0, 0:1, :] * kn)

    def chunk_start(j):
        return j * tkv if isinstance(j, int) else pl.multiple_of(j * tkv, tkv)

    def scores(j):
        kc = k_ref[0, pl.ds(chunk_start(j), tkv), :]
        return _dot(kc, q2)

    def pv(j, pt):
        vc = vt_ref[0, 0, :, pl.ds(chunk_start(j), tkv)]
        return _dot(vc, pt)

    def sweep(step, carry):
        def body(j, c):
            carry, s = c
            s_next = scores(j + 1)
            return step(j, carry, s), s_next

        s = scores(0)
        if n_kv > 1:
            carry, s = lax.fori_loop(0, n_kv - 1, body, (carry, s))
        return step(n_kv - 1, carry, s)

    def emit(acc):
        o = acc[:HEAD_DIM] / acc[HEAD_DIM:HEAD_DIM + 1]
        ot = jnp.concatenate([o, jnp.zeros_like(o)], axis=0).T
        for g in range(ATT_GROUP):
            o_ref[0, :, HEAD_DIM * g:HEAD_DIM * (g + 1)] = _bf(ot[q_pos * g:q_pos * (g + 1), :HEAD_DIM])

    def fast_group(base, acc):
        for u in range(unroll):
            acc = acc + pv(base + u, _bf(jnp.exp2(scores(base + u) - shift)))
        return acc

    acc0 = jnp.zeros((VT_ROWS, ncol), F32)
    n_groups = n_kv // unroll
    if n_groups > 1:
        acc = lax.fori_loop(0, n_groups, lambda j, acc: fast_group(j * unroll, acc), acc0)
    else:
        acc = fast_group(0, acc0)
    emit(acc)
    denom_ok = jnp.min(acc[HEAD_DIM:HEAD_DIM + 1]) >= DENOM_FLOOR

    @pl.when(jnp.logical_not(denom_ok))
    def _():
        def step(j, carry, s):
            m, acc = carry
            m_new = jnp.maximum(m, jnp.max(s, axis=0, keepdims=True))
            return m_new, acc * jnp.exp2(m - m_new) + pv(j, _bf(jnp.exp2(s - m_new)))

        _, acc_online = sweep(step, (jnp.full((1, ncol), -jnp.inf, F32), acc0))
        emit(acc_online)


def _attention(qt, qn, kn, k, vt, B, T, q_pos, tkv):
    nb = T // q_pos
    n_kv = T // tkv
    kern = functools.partial(_attn_kernel, n_kv=n_kv, tkv=tkv, unroll=_tile(n_kv, KV_UNROLL))
    return pl.pallas_call(
        kern,
        grid=(B, ATT_KV_HEADS, nb),
        in_specs=[
            pl.BlockSpec((1, 1, HEAD_DIM, ATT_GROUP * q_pos), lambda bi, h, i: (bi, h, 0, i)),
            pl.BlockSpec((1, 1, SUBLANES, ATT_GROUP * q_pos), lambda bi, h, i: (bi, h, 0, i)),
            pl.BlockSpec((1, SUBLANES, KV_WIDTH), lambda bi, h, i: (bi, 0, 0)),
            pl.BlockSpec((1, T, KV_WIDTH), lambda bi, h, i: (bi, 0, 0)),
            pl.BlockSpec((1, 1, VT_ROWS, T), lambda bi, h, i: (bi, h, 0, 0)),
        ],
        out_specs=pl.BlockSpec((1, q_pos, ATT_GROUP * HEAD_DIM), lambda bi, h, i: (bi, i, h)),
        out_shape=jax.ShapeDtypeStruct((B, T, ATT_WIDTH), BF16),
        compiler_params=_params(("parallel", "parallel", "arbitrary")),
        name="attention",
    )(qt, qn, kn, k, vt)


def _mix_ffn_kernel(x_ref, yf_ref, yb_ref, g_ref, bg_ref, att_ref, lnw_ref, lnb_ref, hs_ref, wo_r_ref, wo_a_ref,
                    g2_ref, wg_ref, wu_ref, wd_ref, gf_ref, o_ref, h_scr, acc_scr):
    j = pl.program_id(1)

    @pl.when(j == 0)
    def _():
        y = yf_ref[...] + yb_ref[...]
        hs = hs_ref[...]
        mu = _head_sum(y, hs) * (1.0 / HEAD_DIM)
        dy = y - mu
        var = _head_sum(dy * dy, hs) * (1.0 / HEAD_DIM)
        yn = dy * lax.rsqrt(var + LNX_EPS) * lnw_ref[...] + lnb_ref[...]
        yr = yn * g_ref[...].astype(F32) + bg_ref[...].astype(F32)
        x = x_ref[...] + _dot(_bf(yr), wo_r_ref[...]) + _dot(att_ref[...], wo_a_ref[...])
        ms = jnp.mean(x * x, axis=-1, keepdims=True)
        h_scr[...] = _bf(x * lax.rsqrt(ms + NORM_EPS) * g2_ref[...])
        acc_scr[...] = x

    h = h_scr[...]
    gate = _dot(h, wg_ref[...])
    up = _dot(h, wu_ref[...])
    act = gate * _sigmoid(gate) * up
    acc_scr[...] += _dot(_bf(act), wd_ref[...])

    @pl.when(j == pl.num_programs(1) - 1)
    def _():
        xo = acc_scr[...]
        ms = jnp.mean(xo * xo, axis=-1, keepdims=True)
        o_ref[...] = xo * lax.rsqrt(ms + NORM_EPS) * gf_ref[...]


def _mix_ffn(x2, y_f, y_b, g, bg, att, p, tm, tf):
    m = x2.shape[0]
    full = lambda shape: pl.BlockSpec(shape, lambda i, j: tuple(0 for _ in shape))
    tok = pl.BlockSpec((tm, RWKV_WIDTH), lambda i, j: (i, 0))
    wide = pl.BlockSpec((tm, D_MODEL), lambda i, j: (i, 0))
    return pl.pallas_call(
        _mix_ffn_kernel,
        grid=(m // tm, D_FF // tf),
        in_specs=[
            wide, tok, tok, tok, tok, tok,
            full((1, RWKV_WIDTH)), full((1, RWKV_WIDTH)), full((RWKV_WIDTH, RWKV_WIDTH)),
            full((RWKV_WIDTH, D_MODEL)), full((ATT_WIDTH, D_MODEL)),
            full((1, D_MODEL)),
            pl.BlockSpec((D_MODEL, tf), lambda i, j: (0, j)),
            pl.BlockSpec((D_MODEL, tf), lambda i, j: (0, j)),
            pl.BlockSpec((tf, D_MODEL), lambda i, j: (j, 0)),
            full((1, D_MODEL)),
        ],
        out_specs=wide,
        out_shape=jax.ShapeDtypeStruct((m, D_MODEL), F32),
        scratch_shapes=[pltpu.VMEM((tm, D_MODEL), BF16), pltpu.VMEM((tm, D_MODEL), F32)],
        compiler_params=_params(("parallel", "arbitrary")),
        name="mix_ffn",
    )(x2, y_f, y_b, g, bg, att, p["lnx_w"], p["lnx_b"], p["head_ones"], p["wo_r"], p["wo_a"],
      p["norm2_g"], p["ffn_gate"], p["ffn_up"], p["ffn_down"], p["norm_f_g"])


def _rope_tables(T):
    n_rows = T // GRID_W
    t = jnp.arange(T, dtype=jnp.int32)
    row = (t // GRID_W).astype(F32)
    col = (t % GRID_W).astype(F32)
    inv = ROPE_THETA ** (-jnp.arange(ROPE_PAIRS, dtype=F32) / ROPE_PAIRS)
    ar = row[:, None] * inv
    ac = col[:, None] * inv
    cos = jnp.concatenate([jnp.cos(ar), jnp.cos(ar), jnp.cos(ac), jnp.cos(ac)], axis=1)
    sin = jnp.concatenate([-jnp.sin(ar), jnp.sin(ar), -jnp.sin(ac), jnp.sin(ac)], axis=1)
    del n_rows
    return jnp.tile(cos, (1, 2)), jnp.tile(sin, (1, 2))


def _block_diag2(a, b):
    za = jnp.zeros_like(a)
    return jnp.concatenate([jnp.concatenate([a, za], axis=1), jnp.concatenate([za, b], axis=1)], axis=0)


def _prepare_params(norm1_g, w_in, mu_prev, mu_next, k_k, k_a, r_k, w0_f, w_lora_f, w0_b, w_lora_b,
                    a0_f, a_lora_f, a0_b, a_lora_b, g_lora, lnx_w, lnx_b, q_gain, k_gain, w_out,
                    norm2_g, ffn_gate, ffn_up, ffn_down, norm_f_g):
    l = 0
    p = {}
    p["norm1_g"] = norm1_g[l][None]
    p["w_r"] = w_in[l][:, :RWKV_COLS].astype(BF16)
    p["w_a"] = w_in[l][:, RWKV_COLS:].astype(BF16)
    p["mu_prev"] = mu_prev[l][None]
    p["mu_next"] = mu_next[l][None]
    p["k_k"] = k_k[l][None]
    p["k_a"] = k_a[l][None]
    p["r_k"] = r_k[l].reshape(1, RWKV_WIDTH)
    p["w0"] = jnp.concatenate([w0_f[l], w0_b[l]])[None]
    p["w_lora"] = _block_diag2(w_lora_f[l], w_lora_b[l]).astype(BF16)
    p["a0"] = jnp.concatenate([a0_f[l], a0_b[l]])[None]
    p["a_lora"] = _block_diag2(a_lora_f[l], a_lora_b[l]).astype(BF16)
    p["g_lora"] = g_lora[l].astype(BF16)
    p["lnx_w"] = lnx_w[l][None]
    p["lnx_b"] = lnx_b[l][None]
    scale = HEAD_DIM ** -0.5 * float(np.log2(np.e))
    p["qk_gain"] = jnp.concatenate([jnp.tile(q_gain[l] * scale, ATT_Q_HEADS), jnp.tile(k_gain[l], ATT_KV_HEADS)])[None]
    hid = np.arange(640) // HEAD_DIM
    ones = (hid[:, None] == hid[None, :]).astype(np.float32)
    p["head_ones_qk"] = jnp.asarray(ones, BF16)
    p["head_ones"] = jnp.asarray(ones[:512, :512], BF16)
    p["wo_r"] = w_out[l][:RWKV_WIDTH].astype(BF16)
    p["wo_a"] = w_out[l][RWKV_WIDTH:].astype(BF16)
    p["norm2_g"] = norm2_g[l][None]
    p["ffn_gate"] = ffn_gate[l].astype(BF16)
    p["ffn_up"] = ffn_up[l].astype(BF16)
    p["ffn_down"] = ffn_down[l].astype(BF16)
    p["norm_f_g"] = norm_f_g[None]
    return p


def _tile(n, pref):
    t = pref
    while n % t:
        t //= 2
    return t


def _trunk(x, p):
    B, T, D = x.shape
    m = B * T
    x2 = x.reshape(m, D)
    cos_t, sin_t = _rope_tables(T)
    tkv = _tile(T, KV_CHUNK)
    q_pos = _tile(T, Q_POS_SHORT if tkv == T else Q_POS)
    z_r, qt, k, vt, qn, kn = _inproj(x2, B, T, p, cos_t, sin_t, _tile(T, 512), q_pos)
    v, ash, rsh, bsh, ksh, rfull, bkt, lend, g, bg = _rwkv_prep(z_r, T, p, _tile(T, 256))
    y_f, y_b = _rwkv_scan(v, ash, rsh, bsh, ksh, rfull, bkt, lend, B, T)
    att = _attention(qt, qn, kn, k, vt, B, T, q_pos, tkv).reshape(m, ATT_WIDTH)
    out = _mix_ffn(x2, y_f, y_b, g, bg, att, p, _tile(m, 512), 1408)
    return out.reshape(B, T, D)


def kernel(x_prompt, x_sample, norm1_g, w_in, mu_prev, mu_next, k_k, k_a, r_k, w0_f, w_lora_f, w0_b, w_lora_b, a0_f, a_lora_f, a0_b, a_lora_b, g_lora, lnx_w, lnx_b, q_gain, k_gain, w_out, norm2_g, ffn_gate, ffn_up, ffn_down, norm_f_g):
    p = _prepare_params(norm1_g, w_in, mu_prev, mu_next, k_k, k_a, r_k, w0_f, w_lora_f, w0_b, w_lora_b,
                        a0_f, a_lora_f, a0_b, a_lora_b, g_lora, lnx_w, lnx_b, q_gain, k_gain, w_out,
                        norm2_g, ffn_gate, ffn_up, ffn_down, norm_f_g)
    return (_trunk(x_prompt, p), _trunk(x_sample, p))
```

```python
import functools

import jax
import jax.numpy as jnp
import numpy as np
from jax import lax
from jax.experimental import pallas as pl
from jax.experimental.pallas import tpu as pltpu

F32 = jnp.float32
BF16 = jnp.bfloat16

D_MODEL = 1024
HEAD_DIM = 64
RWKV_WIDTH = 512
RWKV_HEADS = 8
ATT_WIDTH = 512
ATT_Q_HEADS = 8
ATT_KV_HEADS = 2
ATT_GROUP = 4
KV_WIDTH = 128
RWKV_COLS = 1920
ATT_COLS = 768
D_FF = 2816
GRID_W = 64
ROPE_THETA = 10000.0
ROPE_PAIRS = 16
NORM_EPS = 1e-6
LNX_EPS = 64e-5

LANES = 128
SUBLANES = 8
CHUNK = 128
Q_POS = 256
Q_POS_SHORT = 512
KV_CHUNK = 2048
KV_UNROLL = 8
VT_ROWS = 80
DECAY_SCALE_LOG2 = float(np.exp(-0.5) * np.log2(np.e))
DENOM_FLOOR = 2.0 ** -100
VMEM_LIMIT = 56 * 1024 * 1024


def _dot(a, b):
    return jnp.dot(a, b, preferred_element_type=F32)


def _bf(x):
    return x.astype(BF16)


def _split3(x):
    hi = _bf(x)
    r1 = x - hi.astype(F32)
    mid = _bf(r1)
    lo = _bf(r1 - mid.astype(F32))
    return hi, mid, lo


def _head_sum(x, ones_bf):
    return _dot(_bf(x), ones_bf)


def _sigmoid(x):
    return 0.5 * jnp.tanh(0.5 * x) + 0.5


def _params(sem):
    return pltpu.CompilerParams(dimension_semantics=sem, vmem_limit_bytes=VMEM_LIMIT)


def _inproj_kernel(x_ref, g_ref, wr_ref, wa_ref, gain_ref, cos_ref, sin_ref, hs_ref,
                   zr_ref, qt_o, k_o, vt_o, qn_o, kn_o, *, q_pos, blocks_per_seq):
    x = x_ref[...]
    ms = jnp.mean(x * x, axis=-1, keepdims=True)
    h = _bf(x * lax.rsqrt(ms + NORM_EPS) * g_ref[...])
    z = _dot(h, wa_ref[...])
    ss = _head_sum(z[:, :640] * z[:, :640], hs_ref[...])
    zr_ref[...] = _dot(h, wr_ref[...])
    first_block = pl.program_id(0) % blocks_per_seq == 0
    _att_operands(z, ss, gain_ref, cos_ref, sin_ref, hs_ref, qt_o, k_o, vt_o, qn_o, kn_o, first_block, q_pos)


def _inproj(x2, B, T, p, cos_t, sin_t, tm, q_pos):
    m = x2.shape[0]
    bps = T // tm
    full = lambda shape: pl.BlockSpec(shape, lambda i: tuple(0 for _ in shape))
    kern = functools.partial(_inproj_kernel, q_pos=q_pos, blocks_per_seq=bps)
    return pl.pallas_call(
        kern,
        grid=(m // tm,),
        in_specs=[
            pl.BlockSpec((tm, D_MODEL), lambda i: (i, 0)),
            full((1, D_MODEL)), full((D_MODEL, RWKV_COLS)), full((D_MODEL, ATT_COLS)),
            full((1, 640)),
            pl.BlockSpec((tm, LANES), lambda i: (i % bps, 0)),
            pl.BlockSpec((tm, LANES), lambda i: (i % bps, 0)),
            full((640, 640)),
        ],
        out_specs=[
            pl.BlockSpec((tm, RWKV_COLS), lambda i: (i, 0)),
            pl.BlockSpec((1, ATT_KV_HEADS, HEAD_DIM, ATT_GROUP * tm), lambda i: (i // bps, 0, 0, i % bps)),
            pl.BlockSpec((1, tm, KV_WIDTH), lambda i: (i // bps, i % bps, 0)),
            pl.BlockSpec((1, ATT_KV_HEADS, VT_ROWS, tm), lambda i: (i // bps, 0, 0, i % bps)),
            pl.BlockSpec((1, ATT_KV_HEADS, SUBLANES, ATT_GROUP * tm), lambda i: (i // bps, 0, 0, i % bps)),
            pl.BlockSpec((1, SUBLANES, KV_WIDTH), lambda i: (i // bps, 0, 0)),
        ],
        out_shape=[
            jax.ShapeDtypeStruct((m, RWKV_COLS), F32),
            jax.ShapeDtypeStruct((B, ATT_KV_HEADS, HEAD_DIM, ATT_GROUP * T), BF16),
            jax.ShapeDtypeStruct((B, T, KV_WIDTH), BF16),
            jax.ShapeDtypeStruct((B, ATT_KV_HEADS, VT_ROWS, T), BF16),
            jax.ShapeDtypeStruct((B, ATT_KV_HEADS, SUBLANES, ATT_GROUP * T), F32),
            jax.ShapeDtypeStruct((B, SUBLANES, KV_WIDTH), F32),
        ],
        compiler_params=_params(("arbitrary",)),
        name="inproj",
    )(x2, p["norm1_g"], p["w_r"], p["w_a"], p["qk_gain"], cos_t, sin_t, p["head_ones_qk"])


def _rwkv_prep_kernel(z_ref, zp_ref, zn_ref, mup_ref, mun_ref, kk_ref, ka_ref, rk_ref,
                      w0_ref, wl_ref, a0_ref, al_ref, gl_ref,
                      hs_ref,
                      v_o, ash_o, rsh_o, bsh_o, ksh_o, rfull_o, bkt_o, lend_o, g_o, bg_o,
                      *, tm, blocks_per_seq):
    i = pl.program_id(0)
    pos = i % blocks_per_seq
    z = z_ref[...]
    mup, mun = mup_ref[...], mun_ref[...]
    ri = lax.broadcasted_iota(jnp.int32, (tm, tm), 0)
    ci = lax.broadcasted_iota(jnp.int32, (tm, tm), 1)
    shifts = jnp.concatenate([(ci == ri - 1).astype(BF16), (ci == ri + 1).astype(BF16)], axis=1)
    zf = z * (1.0 - mup - mun) + _dot(shifts, jnp.concatenate([_bf(z * mup), _bf(z * mun)], axis=0))
    prev_row = jnp.where(pos == 0, 0.0, zp_ref[SUBLANES - 1:SUBLANES, :]) * mup
    next_row = jnp.where(pos == blocks_per_seq - 1, 0.0, zn_ref[0:1, :]) * mun
    rows = lax.broadcasted_iota(jnp.int32, (SUBLANES, 1), 0)
    zf = jnp.concatenate([zf[:SUBLANES] + jnp.where(rows == 0, prev_row, 0.0),
                          zf[SUBLANES:tm - SUBLANES],
                          zf[tm - SUBLANES:] + jnp.where(rows == SUBLANES - 1, next_row, 0.0)], axis=0)

    r = zf[:, 0:512]
    k = zf[:, 512:1024]
    v = zf[:, 1024:1536]
    wd = zf[:, 1536:1664]
    ad = zf[:, 1664:1792]
    gd = zf[:, 1792:1920]
    hs = hs_ref[...]

    kk = k * kk_ref[...]
    ss = _head_sum(kk * kk, hs)
    kk = kk * lax.rsqrt(jnp.maximum(ss, 1e-12))

    lw_both = w0_ref[...] + _dot(_bf(jnp.tanh(wd)), wl_ref[...])
    as_both = a0_ref[...] + _dot(_bf(ad), al_ref[...])
    ka_half = 0.5 * ka_ref[...]
    same_chunk = (ri // CHUNK) == (ci // CHUNK)
    n_chunks = tm // CHUNK
    kd_sum = jnp.zeros_like(k)
    for d in range(2):
        lw = -DECAY_SCALE_LOG2 * _sigmoid(lw_both[:, 512 * d:512 * (d + 1)])
        th = jnp.tanh(0.5 * as_both[:, 512 * d:512 * (d + 1)])
        kd = k * (1.0 + (th - 1.0) * ka_half)
        b = kk * (0.5 * th + 0.5)
        kd_sum = kd_sum + kd
        processed = (ci <= ri) if d == 0 else (ci >= ri)
        tri = (same_chunk & processed).astype(BF16)
        hi, mid, lo = _split3(lw)
        cum = _dot(tri, hi) + _dot(tri, mid) + _dot(tri, lo)
        last = CHUNK - 1 if d == 0 else 0
        l_end = [cum[CHUNK * cc + last:CHUNK * cc + last + 1] for cc in range(n_chunks)]
        for cc in range(n_chunks):
            lend_o[d, cc] = jnp.broadcast_to(l_end[cc], (SUBLANES, RWKV_WIDTH))
        rows = lambda vals: jnp.concatenate([jnp.broadcast_to(x, (CHUNK, RWKV_WIDTH)) for x in vals], axis=0)
        l_half = rows([0.5 * x for x in l_end])
        e_half = rows([jnp.exp2(0.5 * x) for x in l_end])
        r_sh = r * jnp.exp2(cum - l_half)
        ash_o[d] = _bf(-kk * jnp.exp2(cum - lw - l_half))
        rsh_o[d] = _bf(r_sh)
        rfull_o[d] = _bf(r_sh * e_half)
        e_b = jnp.exp2(l_half - cum)
        b_sh = b * e_b
        k_sh = kd * e_b
        bsh_o[d] = _bf(b_sh)
        ksh_o[d] = _bf(k_sh)
        b_hat = b_sh * e_half
        k_hat = k_sh * e_half
        for cc in range(n_chunks):
            rs = slice(CHUNK * cc, CHUNK * (cc + 1))
            for p in range(RWKV_HEADS // 2):
                ls = slice(LANES * p, LANES * (p + 1))
                bkt_o[d, cc, ls, 0:CHUNK] = _bf(b_hat[rs, ls].T)
                bkt_o[d, cc, ls, CHUNK:2 * CHUNK] = _bf(k_hat[rs, ls].T)
    coef = _head_sum(r * kd_sum * (0.5 * rk_ref[...]), hs)
    g = _dot(_bf(_sigmoid(gd)), gl_ref[...])
    v_o[...] = _bf(v)
    g_o[...] = _bf(g)
    bg_o[...] = _bf(coef * v * g)


def _rwkv_prep(z_r, T, p, tm):
    m = z_r.shape[0]
    bps = T // tm
    hb = tm // SUBLANES
    nhalo = m // SUBLANES
    full = lambda shape: pl.BlockSpec(shape, lambda i: tuple(0 for _ in shape))
    tok = pl.BlockSpec((tm, RWKV_WIDTH), lambda i: (i, 0))
    tok2 = pl.BlockSpec((2, tm, RWKV_WIDTH), lambda i: (0, i, 0))
    cpb = tm // CHUNK
    nchunk = m // CHUNK
    tok2_shape = jax.ShapeDtypeStruct((2, m, RWKV_WIDTH), BF16)
    kern = functools.partial(_rwkv_prep_kernel, tm=tm, blocks_per_seq=bps)
    return pl.pallas_call(
        kern,
        grid=(m // tm,),
        in_specs=[
            pl.BlockSpec((tm, RWKV_COLS), lambda i: (i, 0)),
            pl.BlockSpec((SUBLANES, RWKV_COLS), lambda i: (jnp.maximum(i * hb - 1, 0), 0)),
            pl.BlockSpec((SUBLANES, RWKV_COLS), lambda i: (jnp.minimum((i + 1) * hb, nhalo - 1), 0)),
            full((1, RWKV_COLS)), full((1, RWKV_COLS)),
            full((1, 512)), full((1, 512)), full((1, 512)),
            full((1, 1024)), full((128, 1024)),
            full((1, 1024)), full((128, 1024)),
            full((128, 512)),
            full((512, 512)),
        ],
        out_specs=[
            tok, tok2, tok2, tok2, tok2, tok2,
            pl.BlockSpec((2, cpb, RWKV_WIDTH, 2 * CHUNK), lambda i: (0, i, 0, 0)),
            pl.BlockSpec((2, cpb, SUBLANES, RWKV_WIDTH), lambda i: (0, i, 0, 0)),
            tok, tok,
        ],
        out_shape=[
            jax.ShapeDtypeStruct((m, RWKV_WIDTH), BF16),
            tok2_shape, tok2_shape, tok2_shape, tok2_shape, tok2_shape,
            jax.ShapeDtypeStruct((2, nchunk, RWKV_WIDTH, 2 * CHUNK), BF16),
            jax.ShapeDtypeStruct((2, nchunk, SUBLANES, RWKV_WIDTH), F32),
            jax.ShapeDtypeStruct((m, RWKV_WIDTH), BF16),
            jax.ShapeDtypeStruct((m, RWKV_WIDTH), BF16),
        ],
        compiler_params=_params(("parallel",)),
        name="rwkv_prep",
    )(z_r, z_r, z_r, p["mu_prev"], p["mu_next"], p["k_k"], p["k_a"], p["r_k"],
      p["w0"], p["w_lora"], p["a0"], p["a_lora"], p["g_lora"],
      p["head_ones"])


def _scan_kernel(*refs):
    C = CHUNK
    c = pl.program_id(1)
    z_scr = refs[-1]

    @pl.when(c == 0)
    def _():
        z_scr[...] = jnp.zeros_like(z_scr)

    row = lax.broadcasted_iota(jnp.int32, (C, C), 0)
    col = lax.broadcasted_iota(jnp.int32, (C, C), 1)
    strict_d = [col < row, col > row]
    incl_d = [col <= row, col >= row]
    eye = (row == col).astype(F32)
    lane = lax.broadcasted_iota(jnp.int32, (1, LANES), 1)
    m0 = lane < HEAD_DIM
    m1 = lane >= HEAD_DIM
    blockdiag = (row < HEAD_DIM) == (col < HEAD_DIM)

    def both_heads(x):
        zero = jnp.zeros((), x.dtype)
        return jnp.concatenate([jnp.where(m0, x, zero), jnp.where(m1, x, zero)], axis=0)

    ppd = RWKV_HEADS // 2
    pairs = range(2 * ppd)
    heads = range(2 * RWKV_HEADS)
    n_in = (len(refs) - 3) // 2
    v, a_sh, r_sh, b_sh, k_sh, r_full, bkt_refs, e_half, p_end, strict, incl = ([] for _ in range(11))
    for pp in pairs:
        d, sl = pp // ppd, slice(LANES * (pp % ppd), LANES * (pp % ppd + 1))
        v_ref, ash_ref, rsh_ref, bsh_ref, ksh_ref, rfull_ref, bkt_ref, lend_ref = refs[n_in * d:n_in * (d + 1)]
        v.append(v_ref[:, sl])
        a_sh.append(ash_ref[0, :, sl])
        r_sh.append(rsh_ref[0, :, sl])
        b_sh.append(bsh_ref[0, :, sl])
        k_sh.append(ksh_ref[0, :, sl])
        r_full.append(rfull_ref[0, :, sl])
        bkt_refs.append((bkt_ref, sl))
        l_end = lend_ref[0, 0, 0:1, sl]
        e_half.append(jnp.exp2(0.5 * l_end))
        p_end.append(jnp.exp2(l_end))
        strict.append(strict_d[d])
        incl.append(incl_d[d])
    y_refs = refs[2 * n_in:2 * n_in + 2]

    amat = []
    for p in pairs:
        ar = jnp.concatenate([a_sh[p], r_sh[p]], axis=0)
        bk = jnp.concatenate([b_sh[p], k_sh[p]], axis=0)
        amat.append(lax.dot_general(both_heads(ar), bk, (((1,), (1,)), ((), ())),
                                    preferred_element_type=F32))
    n_bf, a_ak, a_rb, a_rk, t_inv = [], [], [], [], []
    for hd in heads:
        ah = amat[hd // 2][2 * C * (hd % 2):2 * C * (hd % 2 + 1)]
        n = jnp.where(strict[hd // 2], ah[:C, :C], 0.0)
        n_bf.append(_bf(n))
        t_inv.append(eye + n)
        a_ak.append(jnp.where(strict[hd // 2], ah[:C, C:], 0.0))
        a_rb.append(jnp.where(incl[hd // 2], ah[C:, :C], 0.0))
        a_rk.append(jnp.where(incl[hd // 2], ah[C:, C:], 0.0))
    def pair_products(lhs, rhs):
        out = []
        for p in pairs:
            a, b = rhs[2 * p], rhs[2 * p + 1]
            zero = jnp.zeros_like(a)
            diag = jnp.concatenate([jnp.concatenate([a, zero], axis=1),
                                    jnp.concatenate([zero, b], axis=1)], axis=0)
            prod = _dot(jnp.concatenate(lhs[2 * p:2 * p + 2], axis=1), diag)
            out += [prod[:, :C], prod[:, C:]]
        return out

    pw = [_bf(x) for x in pair_products(n_bf, n_bf)]
    avy = []
    for p in pairs:
        akrk = jnp.concatenate([jnp.concatenate(a_ak[2 * p:2 * p + 2], axis=1),
                                jnp.concatenate(a_rk[2 * p:2 * p + 2], axis=1)], axis=0)
        avy.append(_dot(_bf(akrk), both_heads(v[p])))
    for _ in range(5):
        both = [_dot(pw[hd], jnp.concatenate([pw[hd], _bf(t_inv[hd])], axis=1)) for hd in heads]
        pw = [_bf(both[hd][:, :C]) for hd in heads]
        t_inv = [t_inv[hd] + both[hd][:, C:] for hd in heads]
    last = pair_products(pw, [_bf(t) for t in t_inv])
    t_inv = [t_inv[hd] + last[hd] for hd in heads]

    aw = []
    for p in pairs:
        rhs = jnp.concatenate([both_heads(a_sh[p]), both_heads(_bf(avy[p][:C]))], axis=1)
        aw.append(_dot(_bf(jnp.concatenate(t_inv[2 * p:2 * p + 2], axis=1)), rhs))

    zs = [z_scr[p] for p in pairs]
    xs = []
    for q in range(0, 2 * ppd, 2):
        a_bar = jnp.concatenate([aw[q][:, :LANES] * e_half[q], aw[q + 1][:, :LANES] * e_half[q + 1]], axis=1)
        zero = jnp.zeros((LANES, LANES), BF16)
        z_diag = jnp.concatenate([jnp.concatenate([_bf(zs[q]), zero], axis=1),
                                  jnp.concatenate([zero, _bf(zs[q + 1])], axis=1)], axis=0)
        r_wide = jnp.concatenate([r_full[q], r_full[q + 1]], axis=1)
        x2 = _dot(jnp.concatenate([_bf(a_bar), r_wide], axis=0), z_diag)
        xs += [x2[:, :LANES], x2[:, LANES:]]
    us = [xs[p][:C] + aw[p][:, LANES:] for p in pairs]
    for p in pairs:
        y = xs[p][C:] + avy[p][C:] + _dot(_bf(jnp.concatenate(a_rb[2 * p:2 * p + 2], axis=1)),
                                           _bf(both_heads(us[p])))
        bkt_ref, sl = bkt_refs[p]
        y_refs[p // ppd][:, sl] = y
    for p in pairs:
        bkt_ref, sl = bkt_refs[p]
        bkt = bkt_ref[0, 0, sl, :]
        uv = jnp.concatenate([_bf(us[p]), v[p]], axis=0)
        pend_col = jnp.broadcast_to(p_end[p], (LANES, LANES)).T
        z_new = zs[p] * pend_col + _dot(bkt, uv)
        z_scr[p] = jnp.where(blockdiag, z_new, 0.0)


def _rwkv_scan(v, ash, rsh, bsh, ksh, rfull, bkt, lend, B, T):
    m = v.shape[0]
    nc = T // CHUNK

    def specs(d):
        def blk(bi, c):
            return bi * nc + c + d * (nc - 1 - 2 * c)

        tok = pl.BlockSpec((CHUNK, RWKV_WIDTH), lambda bi, c: (blk(bi, c), 0))
        tok2 = pl.BlockSpec((1, CHUNK, RWKV_WIDTH), lambda bi, c: (d, blk(bi, c), 0))
        return tok, [
            tok, tok2, tok2, tok2, tok2, tok2,
            pl.BlockSpec((1, 1, RWKV_WIDTH, 2 * CHUNK), lambda bi, c: (d, blk(bi, c), 0, 0)),
            pl.BlockSpec((1, 1, SUBLANES, RWKV_WIDTH), lambda bi, c: (d, blk(bi, c), 0, 0)),
        ]

    (out_f, in_f), (out_b, in_b) = specs(0), specs(1)
    operands = (v, ash, rsh, bsh, ksh, rfull, bkt, lend)
    y_shape = jax.ShapeDtypeStruct((m, RWKV_WIDTH), F32)
    return pl.pallas_call(
        _scan_kernel,
        grid=(B, nc),
        in_specs=in_f + in_b,
        out_specs=[out_f, out_b],
        out_shape=[y_shape, y_shape],
        scratch_shapes=[pltpu.VMEM((RWKV_HEADS, LANES, LANES), F32)],
        compiler_params=_params(("parallel", "arbitrary")),
        name="rwkv_scan",
    )(*operands, *operands)


def _att_operands(z, ss, gain_ref, cos_ref, sin_ref, hs_ref, qt_o, k_o, vt_o, qn_o, kn_o, first_block, q_pos):
    tm = z.shape[0]
    qk = z[:, :640]
    qk = qk * lax.rsqrt(ss * (1.0 / HEAD_DIM) + NORM_EPS) * gain_ref[...]
    width = qk.shape[1]
    lane = lax.broadcasted_iota(jnp.int32, (1, width), 1)
    first = (lane % (2 * ROPE_PAIRS)) < ROPE_PAIRS
    partner = jnp.where(first, pltpu.roll(qk, width - ROPE_PAIRS, 1), pltpu.roll(qk, ROPE_PAIRS, 1))
    cos = jnp.concatenate([cos_ref[...]] * 5, axis=1)
    sin = jnp.concatenate([sin_ref[...]] * 5, axis=1)
    qk = qk * cos + partner * sin
    for sb in range(tm // q_pos):
        rows = slice(q_pos * sb, q_pos * (sb + 1))
        for j in range(4):
            st = _bf(qk[rows, LANES * j:LANES * (j + 1)].T)
            h = j // 2
            c0 = q_pos * (ATT_GROUP * sb + (2 * j) % ATT_GROUP)
            qt_o[0, h, :, c0:c0 + q_pos] = st[:HEAD_DIM]
            qt_o[0, h, :, c0 + q_pos:c0 + 2 * q_pos] = st[HEAD_DIM:]
            sq = st.astype(F32)
            sq = sq * sq
            for e in range(2):
                nrm = jnp.sum(sq[HEAD_DIM * e:HEAD_DIM * (e + 1)], axis=0, keepdims=True)
                qn_o[0, h, :, c0 + q_pos * e:c0 + q_pos * (e + 1)] = jnp.broadcast_to(nrm, (SUBLANES, q_pos))
    kb = _bf(qk[:, 512:640])
    k_o[0] = kb
    kf = kb.astype(F32)
    kn = jnp.max(_head_sum(kf * kf, hs_ref[512:640, 512:640]), axis=0, keepdims=True)
    kn = jnp.broadcast_to(kn, (SUBLANES, KV_WIDTH))

    @pl.when(first_block)
    def _():
        kn_o[0] = kn

    @pl.when(jnp.logical_not(first_block))
    def _():
        kn_o[0] = jnp.maximum(kn_o[0], kn)

    vt = _bf(z[:, 640:768].T)
    ones = jnp.ones((VT_ROWS - HEAD_DIM, tm), BF16)
    for h in range(ATT_KV_HEADS):
        vt_o[0, h, :HEAD_DIM, :] = vt[HEAD_DIM * h:HEAD_DIM * (h + 1)]
        vt_o[0, h, HEAD_DIM:, :] = ones


def _attn_kernel(qt_ref, qn_ref, kn_ref, k_ref, vt_ref, o_ref, *, n_kv, tkv, unroll):
    h = pl.program_id(1)
    ncol = qt_ref.shape[-1]
    q_pos = ncol // ATT_GROUP
    qt = qt_ref[0, 0]
    rowh = lax.broadcasted_iota(jnp.int32, (KV_WIDTH, 1), 0) // HEAD_DIM
    q2 = jnp.where(rowh == h, jnp.concatenate([qt, qt], axis=0), jnp.zeros((), BF16))
    laneh = lax.broadcasted_iota(jnp.int32, (1, KV_WIDTH), 1) // HEAD_DIM
    kn = jnp.max(jnp.where(laneh == h, kn_ref[0, 0:1, :], 0.0), axis=1, keepdims=True)
    shift = jnp.sqrt(qn_ref[0, 0, 0:1, :] * kn)

    def chunk_start(j):
        return j * tkv if isinstance(j, int) else pl.multiple_of(j * tkv, tkv)

    def scores(j):
        kc = k_ref[0, pl.ds(chunk_start(j), tkv), :]
        return _dot(kc, q2)

    def pv(j, pt):
        vc = vt_ref[0, 0, :, pl.ds(chunk_start(j), tkv)]
        return _dot(vc, pt)

    def sweep(step, carry):
        def body(j, c):
            carry, s = c
            s_next = scores(j + 1)
            return step(j, carry, s), s_next

        s = scores(0)
        if n_kv > 1:
            carry, s = lax.fori_loop(0, n_kv - 1, body, (carry, s))
        return step(n_kv - 1, carry, s)

    def emit(acc):
        o = acc[:HEAD_DIM] / acc[HEAD_DIM:HEAD_DIM + 1]
        ot = jnp.concatenate([o, jnp.zeros_like(o)], axis=0).T
        for g in range(ATT_GROUP):
            o_ref[0, :, HEAD_DIM * g:HEAD_DIM * (g + 1)] = _bf(ot[q_pos * g:q_pos * (g + 1), :HEAD_DIM])

    def fast_group(base, acc):
        for u in range(unroll):
            acc = acc + pv(base + u, _bf(jnp.exp2(scores(base + u) - shift)))
        return acc

    acc0 = jnp.zeros((VT_ROWS, ncol), F32)
    n_groups = n_kv // unroll
    if n_groups > 1:
        acc = lax.fori_loop(0, n_groups, lambda j, acc: fast_group(j * unroll, acc), acc0)
    else:
        acc = fast_group(0, acc0)
    emit(acc)
    denom_ok = jnp.min(acc[HEAD_DIM:HEAD_DIM + 1]) >= DENOM_FLOOR

    @pl.when(jnp.logical_not(denom_ok))
    def _():
        def step(j, carry, s):
            m, acc = carry
            m_new = jnp.maximum(m, jnp.max(s, axis=0, keepdims=True))
            return m_new, acc * jnp.exp2(m - m_new) + pv(j, _bf(jnp.exp2(s - m_new)))

        _, acc_online = sweep(step, (jnp.full((1, ncol), -jnp.inf, F32), acc0))
        emit(acc_online)


def _attention(qt, qn, kn, k, vt, B, T, q_pos, tkv):
    nb = T // q_pos
    n_kv = T // tkv
    kern = functools.partial(_attn_kernel, n_kv=n_kv, tkv=tkv, unroll=_tile(n_kv, KV_UNROLL))
    return pl.pallas_call(
        kern,
        grid=(B, ATT_KV_HEADS, nb),
        in_specs=[
            pl.BlockSpec((1, 1, HEAD_DIM, ATT_GROUP * q_pos), lambda bi, h, i: (bi, h, 0, i)),
            pl.BlockSpec((1, 1, SUBLANES, ATT_GROUP * q_pos), lambda bi, h, i: (bi, h, 0, i)),
            pl.BlockSpec((1, SUBLANES, KV_WIDTH), lambda bi, h, i: (bi, 0, 0)),
            pl.BlockSpec((1, T, KV_WIDTH), lambda bi, h, i: (bi, 0, 0)),
            pl.BlockSpec((1, 1, VT_ROWS, T), lambda bi, h, i: (bi, h, 0, 0)),
        ],
        out_specs=pl.BlockSpec((1, q_pos, ATT_GROUP * HEAD_DIM), lambda bi, h, i: (bi, i, h)),
        out_shape=jax.ShapeDtypeStruct((B, T, ATT_WIDTH), BF16),
        compiler_params=_params(("parallel", "parallel", "arbitrary")),
        name="attention",
    )(qt, qn, kn, k, vt)


def _mix_ffn_kernel(x_ref, yf_ref, yb_ref, g_ref, bg_ref, att_ref, lnw_ref, lnb_ref, hs_ref, wo_r_ref, wo_a_ref,
                    g2_ref, wg_ref, wu_ref, wd_ref, gf_ref, o_ref, h_scr, acc_scr):
    j = pl.program_id(1)

    @pl.when(j == 0)
    def _():
        y = yf_ref[...] + yb_ref[...]
        hs = hs_ref[...]
        mu = _head_sum(y, hs) * (1.0 / HEAD_DIM)
        dy = y - mu
        var = _head_sum(dy * dy, hs) * (1.0 / HEAD_DIM)
        yn = dy * lax.rsqrt(var + LNX_EPS) * lnw_ref[...] + lnb_ref[...]
        yr = yn * g_ref[...].astype(F32) + bg_ref[...].astype(F32)
        x = x_ref[...] + _dot(_bf(yr), wo_r_ref[...]) + _dot(att_ref[...], wo_a_ref[...])
        ms = jnp.mean(x * x, axis=-1, keepdims=True)
        h_scr[...] = _bf(x * lax.rsqrt(ms + NORM_EPS) * g2_ref[...])
        acc_scr[...] = x

    h = h_scr[...]
    gate = _dot(h, wg_ref[...])
    up = _dot(h, wu_ref[...])
    act = gate * _sigmoid(gate) * up
    acc_scr[...] += _dot(_bf(act), wd_ref[...])

    @pl.when(j == pl.num_programs(1) - 1)
    def _():
        xo = acc_scr[...]
        ms = jnp.mean(xo * xo, axis=-1, keepdims=True)
        o_ref[...] = xo * lax.rsqrt(ms + NORM_EPS) * gf_ref[...]


def _mix_ffn(x2, y_f, y_b, g, bg, att, p, tm, tf):
    m = x2.shape[0]
    full = lambda shape: pl.BlockSpec(shape, lambda i, j: tuple(0 for _ in shape))
    tok = pl.BlockSpec((tm, RWKV_WIDTH), lambda i, j: (i, 0))
    wide = pl.BlockSpec((tm, D_MODEL), lambda i, j: (i, 0))
    return pl.pallas_call(
        _mix_ffn_kernel,
        grid=(m // tm, D_FF // tf),
        in_specs=[
            wide, tok, tok, tok, tok, tok,
            full((1, RWKV_WIDTH)), full((1, RWKV_WIDTH)), full((RWKV_WIDTH, RWKV_WIDTH)),
            full((RWKV_WIDTH, D_MODEL)), full((ATT_WIDTH, D_MODEL)),
            full((1, D_MODEL)),
            pl.BlockSpec((D_MODEL, tf), lambda i, j: (0, j)),
            pl.BlockSpec((D_MODEL, tf), lambda i, j: (0, j)),
            pl.BlockSpec((tf, D_MODEL), lambda i, j: (j, 0)),
            full((1, D_MODEL)),
        ],
        out_specs=wide,
        out_shape=jax.ShapeDtypeStruct((m, D_MODEL), F32),
        scratch_shapes=[pltpu.VMEM((tm, D_MODEL), BF16), pltpu.VMEM((tm, D_MODEL), F32)],
        compiler_params=_params(("parallel", "arbitrary")),
        name="mix_ffn",
    )(x2, y_f, y_b, g, bg, att, p["lnx_w"], p["lnx_b"], p["head_ones"], p["wo_r"], p["wo_a"],
      p["norm2_g"], p["ffn_gate"], p["ffn_up"], p["ffn_down"], p["norm_f_g"])


def _rope_tables(T):
    t = jnp.arange(T, dtype=jnp.int32)
    row = (t // GRID_W).astype(F32)
    col = (t % GRID_W).astype(F32)
    inv = ROPE_THETA ** (-jnp.arange(ROPE_PAIRS, dtype=F32) / ROPE_PAIRS)
    ar = row[:, None] * inv
    ac = col[:, None] * inv
    cos = jnp.concatenate([jnp.cos(ar), jnp.cos(ar), jnp.cos(ac), jnp.cos(ac)], axis=1)
    sin = jnp.concatenate([-jnp.sin(ar), jnp.sin(ar), -jnp.sin(ac), jnp.sin(ac)], axis=1)
    return jnp.tile(cos, (1, 2)), jnp.tile(sin, (1, 2))


def _block_diag2(a, b):
    za = jnp.zeros_like(a)
    return jnp.concatenate([jnp.concatenate([a, za], axis=1), jnp.concatenate([za, b], axis=1)], axis=0)


def _prepare_params(norm1_g, w_in, mu_prev, mu_next, k_k, k_a, r_k, w0_f, w_lora_f, w0_b, w_lora_b,
                    a0_f, a_lora_f, a0_b, a_lora_b, g_lora, lnx_w, lnx_b, q_gain, k_gain, w_out,
                    norm2_g, ffn_gate, ffn_up, ffn_down, norm_f_g):
    l = 0
    p = {}
    p["norm1_g"] = norm1_g[l][None]
    p["w_r"] = w_in[l][:, :RWKV_COLS].astype(BF16)
    p["w_a"] = w_in[l][:, RWKV_COLS:].astype(BF16)
    p["mu_prev"] = mu_prev[l][None]
    p["mu_next"] = mu_next[l][None]
    p["k_k"] = k_k[l][None]
    p["k_a"] = k_a[l][None]
    p["r_k"] = r_k[l].reshape(1, RWKV_WIDTH)
    p["w0"] = jnp.concatenate([w0_f[l], w0_b[l]])[None]
    p["w_lora"] = _block_diag2(w_lora_f[l], w_lora_b[l]).astype(BF16)
    p["a0"] = jnp.concatenate([a0_f[l], a0_b[l]])[None]
    p["a_lora"] = _block_diag2(a_lora_f[l], a_lora_b[l]).astype(BF16)
    p["g_lora"] = g_lora[l].astype(BF16)
    p["lnx_w"] = lnx_w[l][None]
    p["lnx_b"] = lnx_b[l][None]
    scale = HEAD_DIM ** -0.5 * float(np.log2(np.e))
    p["qk_gain"] = jnp.concatenate([jnp.tile(q_gain[l] * scale, ATT_Q_HEADS), jnp.tile(k_gain[l], ATT_KV_HEADS)])[None]
    hid = np.arange(640) // HEAD_DIM
    ones = (hid[:, None] == hid[None, :]).astype(np.float32)
    p["head_ones_qk"] = jnp.asarray(ones, BF16)
    p["head_ones"] = jnp.asarray(ones[:512, :512], BF16)
    p["wo_r"] = w_out[l][:RWKV_WIDTH].astype(BF16)
    p["wo_a"] = w_out[l][RWKV_WIDTH:].astype(BF16)
    p["norm2_g"] = norm2_g[l][None]
    p["ffn_gate"] = ffn_gate[l].astype(BF16)
    p["ffn_up"] = ffn_up[l].astype(BF16)
    p["ffn_down"] = ffn_down[l].astype(BF16)
    p["norm_f_g"] = norm_f_g[None]
    return p


def _tile(n, pref):
    t = pref
    while n % t:
        t //= 2
    return t


def _trunk(x, p):
    B, T, D = x.shape
    m = B * T
    x2 = x.reshape(m, D)
    cos_t, sin_t = _rope_tables(T)
    tkv = _tile(T, KV_CHUNK)
    q_pos = _tile(T, Q_POS_SHORT if tkv == T else Q_POS)
    z_r, qt, k, vt, qn, kn = _inproj(x2, B, T, p, cos_t, sin_t, _tile(T, 512), q_pos)
    v, ash, rsh, bsh, ksh, rfull, bkt, lend, g, bg = _rwkv_prep(z_r, T, p, _tile(T, 256))
    y_f, y_b = _rwkv_scan(v, ash, rsh, bsh, ksh, rfull, bkt, lend, B, T)
    att = _attention(qt, qn, kn, k, vt, B, T, q_pos, tkv).reshape(m, ATT_WIDTH)
    out = _mix_ffn(x2, y_f, y_b, g, bg, att, p, _tile(m, 512), 1408)
    return out.reshape(B, T, D)


def kernel(x_prompt, x_sample, norm1_g, w_in, mu_prev, mu_next, k_k, k_a, r_k, w0_f, w_lora_f, w0_b, w_lora_b, a0_f, a_lora_f, a0_b, a_lora_b, g_lora, lnx_w, lnx_b, q_gain, k_gain, w_out, norm2_g, ffn_gate, ffn_up, ffn_down, norm_f_g):
    p = _prepare_params(norm1_g, w_in, mu_prev, mu_next, k_k, k_a, r_k, w0_f, w_lora_f, w0_b, w_lora_b,
                        a0_f, a_lora_f, a0_b, a_lora_b, g_lora, lnx_w, lnx_b, q_gain, k_gain, w_out,
                        norm2_g, ffn_gate, ffn_up, ffn_down, norm_f_g)
    return (_trunk(x_prompt, p), _trunk(x_sample, p))
```

```python
import functools

import jax
import jax.numpy as jnp
import numpy as np
from jax import lax
from jax.experimental import pallas as pl
from jax.experimental.pallas import tpu as pltpu

F32 = jnp.float32
BF16 = jnp.bfloat16

D_MODEL = 1024
HEAD_DIM = 64
RWKV_WIDTH = 512
RWKV_HEADS = 8
ATT_WIDTH = 512
ATT_Q_HEADS = 8
ATT_KV_HEADS = 2
ATT_GROUP = 4
KV_WIDTH = 128
RWKV_COLS = 1920
ATT_COLS = 768
D_FF = 2816
GRID_W = 64
ROPE_THETA = 10000.0
ROPE_PAIRS = 16
NORM_EPS = 1e-6
LNX_EPS = 64e-5

LANES = 128
SUBLANES = 8
CHUNK = 128
Q_POS = 256
Q_POS_SHORT = 512
KV_CHUNK = 2048
KV_UNROLL = 8
VT_ROWS = 80
DECAY_SCALE_LOG2 = float(np.exp(-0.5) * np.log2(np.e))
DENOM_FLOOR = 2.0 ** -100
VMEM_LIMIT = 56 * 1024 * 1024


def _dot(a, b):
    return jnp.dot(a, b, preferred_element_type=F32)


def _bf(x):
    return x.astype(BF16)


def _split3(x):
    hi = _bf(x)
    r1 = x - hi.astype(F32)
    mid = _bf(r1)
    lo = _bf(r1 - mid.astype(F32))
    return hi, mid, lo


def _head_sum(x, ones_bf):
    return _dot(_bf(x), ones_bf)


def _sigmoid(x):
    return 0.5 * jnp.tanh(0.5 * x) + 0.5


def _params(sem):
    return pltpu.CompilerParams(dimension_semantics=sem, vmem_limit_bytes=VMEM_LIMIT)


def _inproj_kernel(x_ref, g_ref, wr_ref, wa_ref, gain_ref, cos_ref, sin_ref, hs_ref,
                   zr_ref, qt_o, k_o, vt_o, qn_o, kn_o, *, q_pos, blocks_per_seq):
    x = x_ref[...]
    ms = jnp.mean(x * x, axis=-1, keepdims=True)
    h = _bf(x * lax.rsqrt(ms + NORM_EPS) * g_ref[...])
    z = _dot(h, wa_ref[...])
    ss = _head_sum(z[:, :640] * z[:, :640], hs_ref[...])
    zr_ref[...] = _dot(h, wr_ref[...])
    first_block = pl.program_id(0) % blocks_per_seq == 0
    _att_operands(z, ss, gain_ref, cos_ref, sin_ref, hs_ref, qt_o, k_o, vt_o, qn_o, kn_o, first_block, q_pos)


def _inproj(x2, B, T, p, cos_t, sin_t, tm, q_pos):
    m = x2.shape[0]
    bps = T // tm
    full = lambda shape: pl.BlockSpec(shape, lambda i: tuple(0 for _ in shape))
    kern = functools.partial(_inproj_kernel, q_pos=q_pos, blocks_per_seq=bps)
    return pl.pallas_call(
        kern,
        grid=(m // tm,),
        in_specs=[
            pl.BlockSpec((tm, D_MODEL), lambda i: (i, 0)),
            full((1, D_MODEL)), full((D_MODEL, RWKV_COLS)), full((D_MODEL, ATT_COLS)),
            full((1, 640)),
            pl.BlockSpec((tm, LANES), lambda i: (i % bps, 0)),
            pl.BlockSpec((tm, LANES), lambda i: (i % bps, 0)),
            full((640, 640)),
        ],
        out_specs=[
            pl.BlockSpec((tm, RWKV_COLS), lambda i: (i, 0)),
            pl.BlockSpec((1, ATT_KV_HEADS, HEAD_DIM, ATT_GROUP * tm), lambda i: (i // bps, 0, 0, i % bps)),
            pl.BlockSpec((1, tm, KV_WIDTH), lambda i: (i // bps, i % bps, 0)),
            pl.BlockSpec((1, ATT_KV_HEADS, VT_ROWS, tm), lambda i: (i // bps, 0, 0, i % bps)),
            pl.BlockSpec((1, ATT_KV_HEADS, SUBLANES, ATT_GROUP * tm), lambda i: (i // bps, 0, 0, i % bps)),
            pl.BlockSpec((1, SUBLANES, KV_WIDTH), lambda i: (i // bps, 0, 0)),
        ],
        out_shape=[
            jax.ShapeDtypeStruct((m, RWKV_COLS), F32),
            jax.ShapeDtypeStruct((B, ATT_KV_HEADS, HEAD_DIM, ATT_GROUP * T), BF16),
            jax.ShapeDtypeStruct((B, T, KV_WIDTH), BF16),
            jax.ShapeDtypeStruct((B, ATT_KV_HEADS, VT_ROWS, T), BF16),
            jax.ShapeDtypeStruct((B, ATT_KV_HEADS, SUBLANES, ATT_GROUP * T), F32),
            jax.ShapeDtypeStruct((B, SUBLANES, KV_WIDTH), F32),
        ],
        compiler_params=_params(("arbitrary",)),
        name="inproj",
    )(x2, p["norm1_g"], p["w_r"], p["w_a"], p["qk_gain"], cos_t, sin_t, p["head_ones_qk"])


def _rwkv_prep_kernel(z_ref, zp_ref, zn_ref, mup_ref, mun_ref, kk_ref, ka_ref, rk_ref,
                      w0_ref, wl_ref, a0_ref, al_ref, gl_ref,
                      hs_ref,
                      v_o, ash_o, rsh_o, bsh_o, ksh_o, rfull_o, bkt_o, lend_o, g_o, bg_o,
                      *, tm, blocks_per_seq):
    i = pl.program_id(0)
    pos = i % blocks_per_seq
    z = z_ref[...]
    mup, mun = mup_ref[...], mun_ref[...]
    ri = lax.broadcasted_iota(jnp.int32, (tm, tm), 0)
    ci = lax.broadcasted_iota(jnp.int32, (tm, tm), 1)
    shifts = jnp.concatenate([(ci == ri - 1).astype(BF16), (ci == ri + 1).astype(BF16)], axis=1)
    zf = z * (1.0 - mup - mun) + _dot(shifts, jnp.concatenate([_bf(z * mup), _bf(z * mun)], axis=0))
    prev_row = jnp.where(pos == 0, 0.0, zp_ref[SUBLANES - 1:SUBLANES, :]) * mup
    next_row = jnp.where(pos == blocks_per_seq - 1, 0.0, zn_ref[0:1, :]) * mun
    rows = lax.broadcasted_iota(jnp.int32, (SUBLANES, 1), 0)
    zf = jnp.concatenate([zf[:SUBLANES] + jnp.where(rows == 0, prev_row, 0.0),
                          zf[SUBLANES:tm - SUBLANES],
                          zf[tm - SUBLANES:] + jnp.where(rows == SUBLANES - 1, next_row, 0.0)], axis=0)

    r = zf[:, 0:512]
    k = zf[:, 512:1024]
    v = zf[:, 1024:1536]
    wd = zf[:, 1536:1664]
    ad = zf[:, 1664:1792]
    gd = zf[:, 1792:1920]
    hs = hs_ref[...]

    kk = k * kk_ref[...]
    ss = _head_sum(kk * kk, hs)
    kk = kk * lax.rsqrt(jnp.maximum(ss, 1e-12))

    lw_both = w0_ref[...] + _dot(_bf(jnp.tanh(wd)), wl_ref[...])
    as_both = a0_ref[...] + _dot(_bf(ad), al_ref[...])
    ka_half = 0.5 * ka_ref[...]
    same_chunk = (ri // CHUNK) == (ci // CHUNK)
    n_chunks = tm // CHUNK
    kd_sum = jnp.zeros_like(k)
    for d in range(2):
        lw = -DECAY_SCALE_LOG2 * _sigmoid(lw_both[:, 512 * d:512 * (d + 1)])
        th = jnp.tanh(0.5 * as_both[:, 512 * d:512 * (d + 1)])
        kd = k * (1.0 + (th - 1.0) * ka_half)
        b = kk * (0.5 * th + 0.5)
        kd_sum = kd_sum + kd
        processed = (ci <= ri) if d == 0 else (ci >= ri)
        tri = (same_chunk & processed).astype(BF16)
        hi, mid, lo = _split3(lw)
        cum = _dot(tri, hi) + _dot(tri, mid) + _dot(tri, lo)
        last = CHUNK - 1 if d == 0 else 0
        l_end = [cum[CHUNK * cc + last:CHUNK * cc + last + 1] for cc in range(n_chunks)]
        for cc in range(n_chunks):
            lend_o[d, cc] = jnp.broadcast_to(l_end[cc], (SUBLANES, RWKV_WIDTH))
        rows = lambda vals: jnp.concatenate([jnp.broadcast_to(x, (CHUNK, RWKV_WIDTH)) for x in vals], axis=0)
        l_half = rows([0.5 * x for x in l_end])
        e_half = rows([jnp.exp2(0.5 * x) for x in l_end])
        r_sh = r * jnp.exp2(cum - l_half)
        ash_o[d] = _bf(-kk * jnp.exp2(cum - lw - l_half))
        rsh_o[d] = _bf(r_sh)
        rfull_o[d] = _bf(r_sh * e_half)
        e_b = jnp.exp2(l_half - cum)
        b_sh = b * e_b
        k_sh = kd * e_b
        bsh_o[d] = _bf(b_sh)
        ksh_o[d] = _bf(k_sh)
        b_hat = b_sh * e_half
        k_hat = k_sh * e_half
        for cc in range(n_chunks):
            rs = slice(CHUNK * cc, CHUNK * (cc + 1))
            for p in range(RWKV_HEADS // 2):
                ls = slice(LANES * p, LANES * (p + 1))
                bkt_o[d, cc, ls, 0:CHUNK] = _bf(b_hat[rs, ls].T)
                bkt_o[d, cc, ls, CHUNK:2 * CHUNK] = _bf(k_hat[rs, ls].T)
    coef = _head_sum(r * kd_sum * (0.5 * rk_ref[...]), hs)
    g = _dot(_bf(_sigmoid(gd)), gl_ref[...])
    v_o[...] = _bf(v)
    g_o[...] = _bf(g)
    bg_o[...] = _bf(coef * v * g)


def _rwkv_prep(z_r, T, p, tm):
    m = z_r.shape[0]
    bps = T // tm
    hb = tm // SUBLANES
    nhalo = m // SUBLANES
    full = lambda shape: pl.BlockSpec(shape, lambda i: tuple(0 for _ in shape))
    tok = pl.BlockSpec((tm, RWKV_WIDTH), lambda i: (i, 0))
    tok2 = pl.BlockSpec((2, tm, RWKV_WIDTH), lambda i: (0, i, 0))
    cpb = tm // CHUNK
    nchunk = m // CHUNK
    tok2_shape = jax.ShapeDtypeStruct((2, m, RWKV_WIDTH), BF16)
    kern = functools.partial(_rwkv_prep_kernel, tm=tm, blocks_per_seq=bps)
    return pl.pallas_call(
        kern,
        grid=(m // tm,),
        in_specs=[
            pl.BlockSpec((tm, RWKV_COLS), lambda i: (i, 0)),
            pl.BlockSpec((SUBLANES, RWKV_COLS), lambda i: (jnp.maximum(i * hb - 1, 0), 0)),
            pl.BlockSpec((SUBLANES, RWKV_COLS), lambda i: (jnp.minimum((i + 1) * hb, nhalo - 1), 0)),
            full((1, RWKV_COLS)), full((1, RWKV_COLS)),
            full((1, 512)), full((1, 512)), full((1, 512)),
            full((1, 1024)), full((128, 1024)),
            full((1, 1024)), full((128, 1024)),
            full((128, 512)),
            full((512, 512)),
        ],
        out_specs=[
            tok, tok2, tok2, tok2, tok2, tok2,
            pl.BlockSpec((2, cpb, RWKV_WIDTH, 2 * CHUNK), lambda i: (0, i, 0, 0)),
            pl.BlockSpec((2, cpb, SUBLANES, RWKV_WIDTH), lambda i: (0, i, 0, 0)),
            tok, tok,
        ],
        out_shape=[
            jax.ShapeDtypeStruct((m, RWKV_WIDTH), BF16),
            tok2_shape, tok2_shape, tok2_shape, tok2_shape, tok2_shape,
            jax.ShapeDtypeStruct((2, nchunk, RWKV_WIDTH, 2 * CHUNK), BF16),
            jax.ShapeDtypeStruct((2, nchunk, SUBLANES, RWKV_WIDTH), F32),
            jax.ShapeDtypeStruct((m, RWKV_WIDTH), BF16),
            jax.ShapeDtypeStruct((m, RWKV_WIDTH), BF16),
        ],
        compiler_params=_params(("parallel",)),
        name="rwkv_prep",
    )(z_r, z_r, z_r, p["mu_prev"], p["mu_next"], p["k_k"], p["k_a"], p["r_k"],
      p["w0"], p["w_lora"], p["a0"], p["a_lora"], p["g_lora"],
      p["head_ones"])


def _scan_kernel(*refs):
    C = CHUNK
    c = pl.program_id(1)
    z_scr = refs[-1]

    @pl.when(c == 0)
    def _():
        z_scr[...] = jnp.zeros_like(z_scr)

    row = lax.broadcasted_iota(jnp.int32, (C, C), 0)
    col = lax.broadcasted_iota(jnp.int32, (C, C), 1)
    strict_d = [col < row, col > row]
    incl_d = [col <= row, col >= row]
    eye = (row == col).astype(F32)
    lane = lax.broadcasted_iota(jnp.int32, (1, LANES), 1)
    m0 = lane < HEAD_DIM
    m1 = lane >= HEAD_DIM
    blockdiag = (row < HEAD_DIM) == (col < HEAD_DIM)

    def both_heads(x):
        zero = jnp.zeros((), x.dtype)
        return jnp.concatenate([jnp.where(m0, x, zero), jnp.where(m1, x, zero)], axis=0)

    ppd = RWKV_HEADS // 2
    pairs = range(2 * ppd)
    heads = range(2 * RWKV_HEADS)
    n_in = (len(refs) - 3) // 2
    v, a_sh, r_sh, b_sh, k_sh, r_full, bkt_refs, e_half, p_end, strict, incl = ([] for _ in range(11))
    for pp in pairs:
        d, sl = pp // ppd, slice(LANES * (pp % ppd), LANES * (pp % ppd + 1))
        v_ref, ash_ref, rsh_ref, bsh_ref, ksh_ref, rfull_ref, bkt_ref, lend_ref = refs[n_in * d:n_in * (d + 1)]
        v.append(v_ref[:, sl])
        a_sh.append(ash_ref[0, :, sl])
        r_sh.append(rsh_ref[0, :, sl])
        b_sh.append(bsh_ref[0, :, sl])
        k_sh.append(ksh_ref[0, :, sl])
        r_full.append(rfull_ref[0, :, sl])
        bkt_refs.append((bkt_ref, sl))
        l_end = lend_ref[0, 0, 0:1, sl]
        e_half.append(jnp.exp2(0.5 * l_end))
        p_end.append(jnp.exp2(l_end))
        strict.append(strict_d[d])
        incl.append(incl_d[d])
    y_refs = refs[2 * n_in:2 * n_in + 2]

    amat = []
    for p in pairs:
        ar = jnp.concatenate([a_sh[p], r_sh[p]], axis=0)
        bk = jnp.concatenate([b_sh[p], k_sh[p]], axis=0)
        amat.append(lax.dot_general(both_heads(ar), bk, (((1,), (1,)), ((), ())),
                                    preferred_element_type=F32))
    n_bf, a_ak, a_rb, a_rk, t_inv = [], [], [], [], []
    for hd in heads:
        ah = amat[hd // 2][2 * C * (hd % 2):2 * C * (hd % 2 + 1)]
        n = jnp.where(strict[hd // 2], ah[:C, :C], 0.0)
        n_bf.append(_bf(n))
        t_inv.append(eye + n)
        a_ak.append(jnp.where(strict[hd // 2], ah[:C, C:], 0.0))
        a_rb.append(jnp.where(incl[hd // 2], ah[C:, :C], 0.0))
        a_rk.append(jnp.where(incl[hd // 2], ah[C:, C:], 0.0))
    def pair_products(lhs, rhs):
        out = []
        for p in pairs:
            a, b = rhs[2 * p], rhs[2 * p + 1]
            zero = jnp.zeros_like(a)
            diag = jnp.concatenate([jnp.concatenate([a, zero], axis=1),
                                    jnp.concatenate([zero, b], axis=1)], axis=0)
            prod = _dot(jnp.concatenate(lhs[2 * p:2 * p + 2], axis=1), diag)
            out += [prod[:, :C], prod[:, C:]]
        return out

    pw = [_bf(x) for x in pair_products(n_bf, n_bf)]
    avy = []
    for p in pairs:
        akrk = jnp.concatenate([jnp.concatenate(a_ak[2 * p:2 * p + 2], axis=1),
                                jnp.concatenate(a_rk[2 * p:2 * p + 2], axis=1)], axis=0)
        avy.append(_dot(_bf(akrk), both_heads(v[p])))
    for _ in range(5):
        both = [_dot(pw[hd], jnp.concatenate([pw[hd], _bf(t_inv[hd])], axis=1)) for hd in heads]
        pw = [_bf(both[hd][:, :C]) for hd in heads]
        t_inv = [t_inv[hd] + both[hd][:, C:] for hd in heads]
    last = pair_products(pw, [_bf(t) for t in t_inv])
    t_inv = [t_inv[hd] + last[hd] for hd in heads]

    aw = []
    for p in pairs:
        rhs = jnp.concatenate([both_heads(a_sh[p]), both_heads(_bf(avy[p][:C]))], axis=1)
        aw.append(_dot(_bf(jnp.concatenate(t_inv[2 * p:2 * p + 2], axis=1)), rhs))

    zs = [z_scr[p] for p in pairs]
    xs = []
    for q in range(0, 2 * ppd, 2):
        a_bar = jnp.concatenate([aw[q][:, :LANES] * e_half[q], aw[q + 1][:, :LANES] * e_half[q + 1]], axis=1)
        zero = jnp.zeros((LANES, LANES), BF16)
        z_diag = jnp.concatenate([jnp.concatenate([_bf(zs[q]), zero], axis=1),
                                  jnp.concatenate([zero, _bf(zs[q + 1])], axis=1)], axis=0)
        r_wide = jnp.concatenate([r_full[q], r_full[q + 1]], axis=1)
        x2 = _dot(jnp.concatenate([_bf(a_bar), r_wide], axis=0), z_diag)
        xs += [x2[:, :LANES], x2[:, LANES:]]
    us = [xs[p][:C] + aw[p][:, LANES:] for p in pairs]
    for p in pairs:
        y = xs[p][C:] + avy[p][C:] + _dot(_bf(jnp.concatenate(a_rb[2 * p:2 * p + 2], axis=1)),
                                           _bf(both_heads(us[p])))
        bkt_ref, sl = bkt_refs[p]
        y_refs[p // ppd][:, sl] = y
    for p in pairs:
        bkt_ref, sl = bkt_refs[p]
        bkt = bkt_ref[0, 0, sl, :]
        uv = jnp.concatenate([_bf(us[p]), v[p]], axis=0)
        pend_col = jnp.broadcast_to(p_end[p], (LANES, LANES)).T
        z_new = zs[p] * pend_col + _dot(bkt, uv)
        z_scr[p] = jnp.where(blockdiag, z_new, 0.0)


def _rwkv_scan(v, ash, rsh, bsh, ksh, rfull, bkt, lend, B, T):
    m = v.shape[0]
    nc = T // CHUNK

    def specs(d):
        def blk(bi, c):
            return bi * nc + c + d * (nc - 1 - 2 * c)

        tok = pl.BlockSpec((CHUNK, RWKV_WIDTH), lambda bi, c: (blk(bi, c), 0))
        tok2 = pl.BlockSpec((1, CHUNK, RWKV_WIDTH), lambda bi, c: (d, blk(bi, c), 0))
        return tok, [
            tok, tok2, tok2, tok2, tok2, tok2,
            pl.BlockSpec((1, 1, RWKV_WIDTH, 2 * CHUNK), lambda bi, c: (d, blk(bi, c), 0, 0)),
            pl.BlockSpec((1, 1, SUBLANES, RWKV_WIDTH), lambda bi, c: (d, blk(bi, c), 0, 0)),
        ]

    (out_f, in_f), (out_b, in_b) = specs(0), specs(1)
    operands = (v, ash, rsh, bsh, ksh, rfull, bkt, lend)
    y_shape = jax.ShapeDtypeStruct((m, RWKV_WIDTH), F32)
    return pl.pallas_call(
        _scan_kernel,
        grid=(B, nc),
        in_specs=in_f + in_b,
        out_specs=[out_f, out_b],
        out_shape=[y_shape, y_shape],
        scratch_shapes=[pltpu.VMEM((RWKV_HEADS, LANES, LANES), F32)],
        compiler_params=_params(("parallel", "arbitrary")),
        name="rwkv_scan",
    )(*operands, *operands)


def _att_operands(z, ss, gain_ref, cos_ref, sin_ref, hs_ref, qt_o, k_o, vt_o, qn_o, kn_o, first_block, q_pos):
    tm = z.shape[0]
    qk = z[:, :640]
    qk = qk * lax.rsqrt(ss * (1.0 / HEAD_DIM) + NORM_EPS) * gain_ref[...]
    width = qk.shape[1]
    lane = lax.broadcasted_iota(jnp.int32, (1, width), 1)
    first = (lane % (2 * ROPE_PAIRS)) < ROPE_PAIRS
    partner = jnp.where(first, pltpu.roll(qk, width - ROPE_PAIRS, 1), pltpu.roll(qk, ROPE_PAIRS, 1))
    cos = jnp.concatenate([cos_ref[...]] * 5, axis=1)
    sin = jnp.concatenate([sin_ref[...]] * 5, axis=1)
    qk = qk * cos + partner * sin
    for sb in range(tm // q_pos):
        rows = slice(q_pos * sb, q_pos * (sb + 1))
        for j in range(4):
            st = _bf(qk[rows, LANES * j:LANES * (j + 1)].T)
            h = j // 2
            c0 = q_pos * (ATT_GROUP * sb + (2 * j) % ATT_GROUP)
            qt_o[0, h, :, c0:c0 + q_pos] = st[:HEAD_DIM]
            qt_o[0, h, :, c0 + q_pos:c0 + 2 * q_pos] = st[HEAD_DIM:]
            sq = st.astype(F32)
            sq = sq * sq
            for e in range(2):
                nrm = jnp.sum(sq[HEAD_DIM * e:HEAD_DIM * (e + 1)], axis=0, keepdims=True)
                qn_o[0, h, :, c0 + q_pos * e:c0 + q_pos * (e + 1)] = jnp.broadcast_to(nrm, (SUBLANES, q_pos))
    kb = _bf(qk[:, 512:640])
    k_o[0] = kb
    kf = kb.astype(F32)
    kn = jnp.max(_head_sum(kf * kf, hs_ref[512:640, 512:640]), axis=0, keepdims=True)
    kn = jnp.broadcast_to(kn, (SUBLANES, KV_WIDTH))

    @pl.when(first_block)
    def _():
        kn_o[0] = kn

    @pl.when(jnp.logical_not(first_block))
    def _():
        kn_o[0] = jnp.maximum(kn_o[0], kn)

    vt = _bf(z[:, 640:768].T)
    ones = jnp.ones((VT_ROWS - HEAD_DIM, tm), BF16)
    for h in range(ATT_KV_HEADS):
        vt_o[0, h, :HEAD_DIM, :] = vt[HEAD_DIM * h:HEAD_DIM * (h + 1)]
        vt_o[0, h, HEAD_DIM:, :] = ones


def _attn_kernel(qt_ref, qn_ref, kn_ref, k_ref, vt_ref, o_ref, *, n_kv, tkv, unroll):
    h = pl.program_id(1)
    ncol = qt_ref.shape[-1]
    q_pos = ncol // ATT_GROUP
    qt = qt_ref[0, 0]
    rowh = lax.broadcasted_iota(jnp.int32, (KV_WIDTH, 1), 0) // HEAD_DIM
    q2 = jnp.where(rowh == h, jnp.concatenate([qt, qt], axis=0), jnp.zeros((), BF16))
    laneh = lax.broadcasted_iota(jnp.int32, (1, KV_WIDTH), 1) // HEAD_DIM
    kn = jnp.max(jnp.where(laneh == h, kn_ref[0, 0:1, :], 0.0), axis=1, keepdims=True)
    shift = jnp.sqrt(qn_ref[0, 0, 0:1, :] * kn)

    def chunk_start(j):
        return j * tkv if isinstance(j, int) else pl.multiple_of(j * tkv, tkv)

    def scores(j):
        kc = k_ref[0, pl.ds(chunk_start(j), tkv), :]
        return _dot(kc, q2)

    def pv(j, pt):
        vc = vt_ref[0, 0, :, pl.ds(chunk_start(j), tkv)]
        return _dot(vc, pt)

    def sweep(step, carry):
        def body(j, c):
            carry, s = c
            s_next = scores(j + 1)
            return step(j, carry, s), s_next

        s = scores(0)
        if n_kv > 1:
            carry, s = lax.fori_loop(0, n_kv - 1, body, (carry, s))
        return step(n_kv - 1, carry, s)

    def emit(acc):
        o = acc[:HEAD_DIM] / acc[HEAD_DIM:HEAD_DIM + 1]
        ot = jnp.concatenate([o, jnp.zeros_like(o)], axis=0).T
        for g in range(ATT_GROUP):
            o_ref[0, :, HEAD_DIM * g:HEAD_DIM * (g + 1)] = _bf(ot[q_pos * g:q_pos * (g + 1), :HEAD_DIM])

    def fast_group(base, acc):
        for u in range(unroll):
            acc = acc + pv(base + u, _bf(jnp.exp2(scores(base + u) - shift)))
        return acc

    acc0 = jnp.zeros((VT_ROWS, ncol), F32)
    n_groups = n_kv // unroll
    if n_groups > 1:
        acc = lax.fori_loop(0, n_groups, lambda j, acc: fast_group(j * unroll, acc), acc0)
    else:
        acc = fast_group(0, acc0)
    emit(acc)
    denom_ok = jnp.min(acc[HEAD_DIM:HEAD_DIM + 1]) >= DENOM_FLOOR

    @pl.when(jnp.logical_not(denom_ok))
    def _():
        def step(j, carry, s):
            m, acc = carry
            m_new = jnp.maximum(m, jnp.max(s, axis=0, keepdims=True))
            return m_new, acc * jnp.exp2(m - m_new) + pv(j, _bf(jnp.exp2(s - m_new)))

        _, acc_online = sweep(step, (jnp.full((1, ncol), -jnp.inf, F32), acc0))
        emit(acc_online)


def _attention(qt, qn, kn, k, vt, B, T, q_pos, tkv):
    nb = T // q_pos
    n_kv = T // tkv
    kern = functools.partial(_attn_kernel, n_kv=n_kv, tkv=tkv, unroll=_tile(n_kv, KV_UNROLL))
    return pl.pallas_call(
        kern,
        grid=(B, ATT_KV_HEADS, nb),
        in_specs=[
            pl.BlockSpec((1, 1, HEAD_DIM, ATT_GROUP * q_pos), lambda bi, h, i: (bi, h, 0, i)),
            pl.BlockSpec((1, 1, SUBLANES, ATT_GROUP * q_pos), lambda bi, h, i: (bi, h, 0, i)),
            pl.BlockSpec((1, SUBLANES, KV_WIDTH), lambda bi, h, i: (bi, 0, 0)),
            pl.BlockSpec((1, T, KV_WIDTH), lambda bi, h, i: (bi, 0, 0)),
            pl.BlockSpec((1, 1, VT_ROWS, T), lambda bi, h, i: (bi, h, 0, 0)),
        ],
        out_specs=pl.BlockSpec((1, q_pos, ATT_GROUP * HEAD_DIM), lambda bi, h, i: (bi, i, h)),
        out_shape=jax.ShapeDtypeStruct((B, T, ATT_WIDTH), BF16),
        compiler_params=_params(("parallel", "parallel", "arbitrary")),
        name="attention",
    )(qt, qn, kn, k, vt)


def _mix_ffn_kernel(x_ref, yf_ref, yb_ref, g_ref, bg_ref, att_ref, lnw_ref, lnb_ref, hs_ref, wo_r_ref, wo_a_ref,
                    g2_ref, wg_ref, wu_ref, wd_ref, gf_ref, o_ref, h_scr, acc_scr):
    j = pl.program_id(1)

    @pl.when(j == 0)
    def _():
        y = yf_ref[...] + yb_ref[...]
        hs = hs_ref[...]
        mu = _head_sum(y, hs) * (1.0 / HEAD_DIM)
        dy = y - mu
        var = _head_sum(dy * dy, hs) * (1.0 / HEAD_DIM)
        yn = dy * lax.rsqrt(var + LNX_EPS) * lnw_ref[...] + lnb_ref[...]
        yr = yn * g_ref[...].astype(F32) + bg_ref[...].astype(F32)
        x = x_ref[...] + _dot(_bf(yr), wo_r_ref[...]) + _dot(att_ref[...], wo_a_ref[...])
        ms = jnp.mean(x * x, axis=-1, keepdims=True)
        h_scr[...] = _bf(x * lax.rsqrt(ms + NORM_EPS) * g2_ref[...])
        acc_scr[...] = x

    h = h_scr[...]
    gate = _dot(h, wg_ref[...])
    up = _dot(h, wu_ref[...])
    act = gate * _sigmoid(gate) * up
    acc_scr[...] += _dot(_bf(act), wd_ref[...])

    @pl.when(j == pl.num_programs(1) - 1)
    def _():
        xo = acc_scr[...]
        ms = jnp.mean(xo * xo, axis=-1, keepdims=True)
        o_ref[...] = xo * lax.rsqrt(ms + NORM_EPS) * gf_ref[...]


def _mix_ffn(x2, y_f, y_b, g, bg, att, p, tm, tf):
    m = x2.shape[0]
    full = lambda shape: pl.BlockSpec(shape, lambda i, j: tuple(0 for _ in shape))
    tok = pl.BlockSpec((tm, RWKV_WIDTH), lambda i, j: (i, 0))
    wide = pl.BlockSpec((tm, D_MODEL), lambda i, j: (i, 0))
    return pl.pallas_call(
        _mix_ffn_kernel,
        grid=(m // tm, D_FF // tf),
        in_specs=[
            wide, tok, tok, tok, tok, tok,
            full((1, RWKV_WIDTH)), full((1, RWKV_WIDTH)), full((RWKV_WIDTH, RWKV_WIDTH)),
            full((RWKV_WIDTH, D_MODEL)), full((ATT_WIDTH, D_MODEL)),
            full((1, D_MODEL)),
            pl.BlockSpec((D_MODEL, tf), lambda i, j: (0, j)),
            pl.BlockSpec((D_MODEL, tf), lambda i, j: (0, j)),
            pl.BlockSpec((tf, D_MODEL), lambda i, j: (j, 0)),
            full((1, D_MODEL)),
        ],
        out_specs=wide,
        out_shape=jax.ShapeDtypeStruct((m, D_MODEL), F32),
        scratch_shapes=[pltpu.VMEM((tm, D_MODEL), BF16), pltpu.VMEM((tm, D_MODEL), F32)],
        compiler_params=_params(("parallel", "arbitrary")),
        name="mix_ffn",
    )(x2, y_f, y_b, g, bg, att, p["lnx_w"], p["lnx_b"], p["head_ones"], p["wo_r"], p["wo_a"],
      p["norm2_g"], p["ffn_gate"], p["ffn_up"], p["ffn_down"], p["norm_f_g"])


def _rope_tables(T):
    t = jnp.arange(T, dtype=jnp.int32)
    row = (t // GRID_W).astype(F32)
    col = (t % GRID_W).astype(F32)
    inv = ROPE_THETA ** (-jnp.arange(ROPE_PAIRS, dtype=F32) / ROPE_PAIRS)
    ar = row[:, None] * inv
    ac = col[:, None] * inv
    cos = jnp.concatenate([jnp.cos(ar), jnp.cos(ar), jnp.cos(ac), jnp.cos(ac)], axis=1)
    sin = jnp.concatenate([-jnp.sin(ar), jnp.sin(ar), -jnp.sin(ac), jnp.sin(ac)], axis=1)
    return jnp.tile(cos, (1, 2)), jnp.tile(sin, (1, 2))


def _block_diag2(a, b):
    za = jnp.zeros_like(a)
    return jnp.concatenate([jnp.concatenate([a, za], axis=1), jnp.concatenate([za, b], axis=1)], axis=0)


def _prepare_params(norm1_g, w_in, mu_prev, mu_next, k_k, k_a, r_k, w0_f, w_lora_f, w0_b, w_lora_b,
                    a0_f, a_lora_f, a0_b, a_lora_b, g_lora, lnx_w, lnx_b, q_gain, k_gain, w_out,
                    norm2_g, ffn_gate, ffn_up, ffn_down, norm_f_g):
    l = 0
    p = {}
    p["norm1_g"] = norm1_g[l][None]
    p["w_r"] = w_in[l][:, :RWKV_COLS].astype(BF16)
    p["w_a"] = w_in[l][:, RWKV_COLS:].astype(BF16)
    p["mu_prev"] = mu_prev[l][None]
    p["mu_next"] = mu_next[l][None]
    p["k_k"] = k_k[l][None]
    p["k_a"] = k_a[l][None]
    p["r_k"] = r_k[l].reshape(1, RWKV_WIDTH)
    p["w0"] = jnp.concatenate([w0_f[l], w0_b[l]])[None]
    p["w_lora"] = _block_diag2(w_lora_f[l], w_lora_b[l]).astype(BF16)
    p["a0"] = jnp.concatenate([a0_f[l], a0_b[l]])[None]
    p["a_lora"] = _block_diag2(a_lora_f[l], a_lora_b[l]).astype(BF16)
    p["g_lora"] = g_lora[l].astype(BF16)
    p["lnx_w"] = lnx_w[l][None]
    p["lnx_b"] = lnx_b[l][None]
    scale = HEAD_DIM ** -0.5 * float(np.log2(np.e))
    p["qk_gain"] = jnp.concatenate([jnp.tile(q_gain[l] * scale, ATT_Q_HEADS), jnp.tile(k_gain[l], ATT_KV_HEADS)])[None]
    hid = np.arange(640) // HEAD_DIM
    ones = (hid[:, None] == hid[None, :]).astype(np.float32)
    p["head_ones_qk"] = jnp.asarray(ones, BF16)
    p["head_ones"] = jnp.asarray(ones[:512, :512], BF16)
    p["wo_r"] = w_out[l][:RWKV_WIDTH].astype(BF16)
    p["wo_a"] = w_out[l][RWKV_WIDTH:].astype(BF16)
    p["norm2_g"] = norm2_g[l][None]
    p["ffn_gate"] = ffn_gate[l].astype(BF16)
    p["ffn_up"] = ffn_up[l].astype(BF16)
    p["ffn_down"] = ffn_down[l].astype(BF16)
    p["norm_f_g"] = norm_f_g[None]
    return p


def _tile(n, pref):
    t = pref
    while n % t:
        t //= 2
    return t


def _trunk(x, p):
    B, T, D = x.shape
    m = B * T
    x2 = x.reshape(m, D)
    cos_t, sin_t = _rope_tables(T)
    tkv = _tile(T, KV_CHUNK)
    q_pos = _tile(T, Q_POS_SHORT if tkv == T else Q_POS)
    z_r, qt, k, vt, qn, kn = _inproj(x2, B, T, p, cos_t, sin_t, _tile(T, 1024), q_pos)
    v, ash, rsh, bsh, ksh, rfull, bkt, lend, g, bg = _rwkv_prep(z_r, T, p, _tile(T, 256))
    y_f, y_b = _rwkv_scan(v, ash, rsh, bsh, ksh, rfull, bkt, lend, B, T)
    att = _attention(qt, qn, kn, k, vt, B, T, q_pos, tkv).reshape(m, ATT_WIDTH)
    out = _mix_ffn(x2, y_f, y_b, g, bg, att, p, _tile(m, 512), 1408)
    return out.reshape(B, T, D)


def kernel(x_prompt, x_sample, norm1_g, w_in, mu_prev, mu_next, k_k, k_a, r_k, w0_f, w_lora_f, w0_b, w_lora_b, a0_f, a_lora_f, a0_b, a_lora_b, g_lora, lnx_w, lnx_b, q_gain, k_gain, w_out, norm2_g, ffn_gate, ffn_up, ffn_down, norm_f_g):
    p = _prepare_params(norm1_g, w_in, mu_prev, mu_next, k_k, k_a, r_k, w0_f, w_lora_f, w0_b, w_lora_b,
                        a0_f, a_lora_f, a0_b, a_lora_b, g_lora, lnx_w, lnx_b, q_gain, k_gain, w_out,
                        norm2_g, ffn_gate, ffn_up, ffn_down, norm_f_g)
    return (_trunk(x_prompt, p), _trunk(x_sample, p))
```
